```python
import jax
import jax.numpy as jnp
from jax import lax
import numpy as np

D_MODEL = 2048
BATCH = 32
SEQ = 256
DEPTH = 1
DEC_BATCH = 2
DEC_SEQ = 4096
PAST_LEN = 512

GRID_W = 64
H_RET = 8
DK_RET = 128
DV_RET = 256
H_DN = 16
DK_DN = 128
DV_DN = 128
CONV_K = 5
CHUNK = 64
N_KEYS = 128
N_EXPERTS = N_KEYS * N_KEYS
PEER_HEADS = 8
PEER_QDIM = 256
PEER_TOPK = 16
PEER_BLOCK = 128
EPS = 1e-6

RET_QK = H_RET * DK_RET
RET_V = H_RET * DV_RET
DN_QK = H_DN * DK_DN
DN_V = H_DN * DV_DN
IN_SPLITS = (RET_QK, RET_QK, RET_V, RET_V, 2 * DN_QK + DN_V, DN_V, 4 * H_DN, 2 * D_MODEL)
N_IN = 2 * RET_QK + 2 * RET_V + 2 * DN_QK + 2 * DN_V + 4 * H_DN + 2 * D_MODEL

kernel_name = 'bidir_retention_deltanet_peer_diffusion_step'


def rmsnorm(x, g):
    xf = x.astype(jnp.float32)
    xf = xf * lax.rsqrt(jnp.mean(xf * xf, axis=-1, keepdims=True) + EPS)
    return xf.astype(x.dtype) * g


def to_heads(t, n_heads):
    B, L, _ = t.shape
    return t.reshape(B, L, n_heads, -1).transpose(0, 2, 1, 3).astype(jnp.float32)


def from_heads(t):
    B, H, L, d = t.shape
    return t.transpose(0, 2, 1, 3).reshape(B, L, H * d)


def to_chunks(t):
    B, H, L = t.shape[:3]
    t = t.reshape(B, H, L // CHUNK, CHUNK, *t.shape[3:])
    return jnp.moveaxis(t, 2, 0)


def from_chunks(t):
    n, B, H, C, d = t.shape
    return jnp.moveaxis(t, 0, 2).reshape(B, H, n * C, d)


def l2norm(t):
    return t * lax.rsqrt(jnp.sum(t * t, axis=-1, keepdims=True) + EPS)


def dwconv_rows(x, w, n_rows):
    B, L, C = x.shape
    row_len = L // n_rows
    xr = x.reshape(B * n_rows, row_len, C)
    p = CONV_K // 2
    xp = jnp.pad(xr, ((0, 0), (p, p), (0, 0)))
    out = sum(xp[:, i:i + row_len] * w[i] for i in range(CONV_K))
    return out.reshape(B, L, C)


def retention_scan(q, k, v, log_gamma, s0):
    qc, kc, vc = to_chunks(q), to_chunks(k), to_chunks(v)
    pos = jnp.arange(CHUNK, dtype=jnp.float32)
    diff = pos[:, None] - pos[None, :]
    lg = log_gamma[:, None, None]
    dmat = jnp.where(diff >= 0, jnp.exp(lg * jnp.maximum(diff, 0.0)), 0.0)
    inner = jnp.einsum('nbhcd,nbhsd->nbhcs', qc, kc) * dmat
    o_in = jnp.einsum('nbhcs,nbhse->nbhce', inner, vc)
    xi = jnp.exp(log_gamma[:, None] * (pos + 1.0))
    zeta = jnp.exp(log_gamma[:, None] * (CHUNK - 1.0 - pos))
    g_chunk = jnp.exp(log_gamma * CHUNK)[:, None, None]
    qx = qc * xi[:, :, None]
    kz = kc * zeta[:, :, None]

    def step(S, inp):
        q_i, k_i, v_i = inp
        o = jnp.einsum('bhcd,bhde->bhce', q_i, S)
        S = S * g_chunk + jnp.einsum('bhcd,bhce->bhde', k_i, v_i)
        return S, o

    s_final, o_x = lax.scan(step, s0, (qx, kz, vc))
    return from_chunks(o_in + o_x), s_final


def gated_delta_scan(q, k, v, beta, g, s0):
    dv = v.shape[-1]
    qc, kc, vc = to_chunks(q), to_chunks(k), to_chunks(v)
    bc = to_chunks(beta)
    G = jnp.cumsum(to_chunks(g), axis=-1)
    pos = jnp.arange(CHUNK)
    tri = pos[:, None] >= pos[None, :]
    strict = pos[:, None] > pos[None, :]
    lmat = jnp.exp(jnp.where(tri, G[..., :, None] - G[..., None, :], -jnp.inf))
    kb = kc * bc[..., None]
    m = jnp.where(strict, jnp.einsum('nbhcd,nbhsd->nbhcs', kb, kc) * lmat, 0.0)
    a = m + jnp.eye(CHUNK, dtype=m.dtype)
    rhs = jnp.concatenate([vc * bc[..., None], kb * jnp.exp(G)[..., None]], axis=-1)
    sol = lax.linalg.triangular_solve(a, rhs, left_side=True, lower=True)
    u, w = sol[..., :dv], sol[..., dv:]
    aqk = jnp.einsum('nbhcd,nbhsd->nbhcs', qc, kc) * lmat
    qg = qc * jnp.exp(G)[..., None]
    g_last = G[..., -1:]
    kdec = kc * jnp.exp(g_last - G)[..., None]
    decay_last = jnp.exp(g_last)[..., None]

    def step(S, inp):
        u_i, w_i, qg_i, aqk_i, kd_i, dl_i = inp
        v_new = u_i - jnp.einsum('bhcd,bhde->bhce', w_i, S)
        o = jnp.einsum('bhcd,bhde->bhce', qg_i, S) + jnp.einsum('bhcs,bhse->bhce', aqk_i, v_new)
        S = S * dl_i + jnp.einsum('bhcd,bhce->bhde', kd_i, v_new)
        return S, o

    s_final, o = lax.scan(step, s0, (u, w, qg, aqk, kdec, decay_last))
    return from_chunks(o), s_final


def mixer(h, n_rows, ret_s0, dn_s0, p):
    B, L, _ = h.shape
    offs = np.cumsum(IN_SPLITS)[:-1].tolist()
    proj = h @ p['w_in']
    rq, rk, rv, rg, dqkv, dz, dgl, bgate = jnp.split(proj, offs, axis=-1)

    q = to_heads(rq, H_RET)
    k = to_heads(rk, H_RET) * (DK_RET ** -0.5)
    v = to_heads(rv, H_RET)
    lg = jax.nn.log_sigmoid(p['ret_logit'].astype(jnp.float32))
    o_f, rs_f = retention_scan(q, k, v, lg[0], ret_s0[:, 0].astype(jnp.float32))
    o_b, rs_b = retention_scan(jnp.flip(q, 2), jnp.flip(k, 2), jnp.flip(v, 2), lg[1],
                               ret_s0[:, 1].astype(jnp.float32))
    o = o_f + jnp.flip(o_b, 2)
    mu = jnp.mean(o, axis=-1, keepdims=True)
    var = jnp.mean(jnp.square(o - mu), axis=-1, keepdims=True)
    o = (o - mu) * lax.rsqrt(var + EPS)
    ret_out = from_heads(o).astype(h.dtype) * p['ret_gn'] * jax.nn.silu(rg)

    dqkv = jax.nn.silu(dwconv_rows(dqkv, p['dn_conv'], n_rows))
    dq, dk, dvv = jnp.split(dqkv, [DN_QK, 2 * DN_QK], axis=-1)
    q = l2norm(to_heads(dq, H_DN)) * (DK_DN ** -0.5)
    k = l2norm(to_heads(dk, H_DN))
    v = to_heads(dvv, H_DN)
    bgl = dgl.astype(jnp.float32).reshape(B, L, 4, H_DN).transpose(2, 0, 3, 1)
    beta = jax.nn.sigmoid(bgl[:2])
    a_log = p['dn_a_log'].astype(jnp.float32)[:, None, :, None]
    dt_bias = p['dn_dt_bias'].astype(jnp.float32)[:, None, :, None]
    g = -jnp.exp(a_log) * jax.nn.softplus(bgl[2:] + dt_bias)
    d_f, ds_f = gated_delta_scan(q, k, v, beta[0], g[0], dn_s0[:, 0].astype(jnp.float32))
    d_b, ds_b = gated_delta_scan(jnp.flip(q, 2), jnp.flip(k, 2), jnp.flip(v, 2),
                                 jnp.flip(beta[1], -1), jnp.flip(g[1], -1),
                                 dn_s0[:, 1].astype(jnp.float32))
    d = d_f + jnp.flip(d_b, 2)
    d = d * lax.rsqrt(jnp.mean(d * d, axis=-1, keepdims=True) + EPS)
    dn_out = (from_heads(d).astype(h.dtype) * jnp.tile(p['dn_norm'], H_DN)) * jax.nn.silu(dz)

    ga, gb = jnp.split(jax.nn.sigmoid(bgate), 2, axis=-1)
    merged = ga * (ret_out @ p['w_br_a']) + gb * (dn_out @ p['w_br_b'])
    y = merged @ p['w_out']
    ret_state = jnp.stack([rs_f, rs_b], axis=1)
    dn_state = jnp.stack([ds_f, ds_b], axis=1)
    return y, ret_state, dn_state


def peer(h, w_q, subkeys, u_tab, v_tab):
    B, L, D = h.shape
    blocks = h.reshape(B * L // PEER_BLOCK, PEER_BLOCK, D)

    def block(xb):
        q = (xb @ w_q).reshape(PEER_BLOCK, PEER_HEADS, 2, PEER_QDIM // 2)
        s = jnp.einsum('thpd,pkd->thpk', q, subkeys).astype(jnp.float32)
        sv, si = lax.top_k(s, PEER_TOPK)
        cand = sv[:, :, 0, :, None] + sv[:, :, 1, None, :]
        cidx = si[:, :, 0, :, None] * N_KEYS + si[:, :, 1, None, :]
        cand = cand.reshape(PEER_BLOCK, PEER_HEADS, PEER_TOPK * PEER_TOPK)
        cidx = cidx.reshape(PEER_BLOCK, PEER_HEADS, PEER_TOPK * PEER_TOPK)
        top_s, sel = lax.top_k(cand, PEER_TOPK)
        eidx = jnp.take_along_axis(cidx, sel, axis=-1).reshape(PEER_BLOCK, PEER_HEADS * PEER_TOPK)
        gate = jax.nn.softmax(top_s, axis=-1).reshape(PEER_BLOCK, PEER_HEADS * PEER_TOPK)
        act = jax.nn.gelu(jnp.einsum('td,ted->te', xb, u_tab[eidx]))
        return jnp.einsum('te,ted->td', gate.astype(xb.dtype) * act, v_tab[eidx])

    return lax.map(block, blocks).reshape(B, L, D)


def trunk_layer(x, mod, n_rows, ret_s0, dn_s0, p):
    sh1, sc1, gt1, sh2, sc2, gt2 = jnp.split(mod, 6, axis=-1)
    h = rmsnorm(x, p['norm_mix']) * (1.0 + sc1) + sh1
    y, ret_state, dn_state = mixer(h, n_rows, ret_s0, dn_s0, p)
    x = x + gt1 * y
    h = rmsnorm(x, p['norm_ffn']) * (1.0 + sc2) + sh2
    x = x + gt2 * peer(h, p['peer_wq'], p['peer_subkeys'], p['peer_u'], p['peer_v'])
    return x, ret_state, dn_state


def setup_inputs(seed: int = 0) -> dict:
    key = jax.random.key(seed)
    ks = jax.random.split(key, 32)
    D = D_MODEL

    def nrm(k, shape, s):
        return jax.random.normal(k, shape, jnp.float32) * s

    ret_logit0 = jnp.asarray(np.log(2.0 ** np.arange(5, 5 + H_RET) - 1.0).astype(np.float32))
    dt = jnp.exp(jax.random.uniform(ks[14], (DEPTH, 2, H_DN), jnp.float32,
                                    float(np.log(1e-3)), float(np.log(1e-1))))
    return {
        'x_prompt': nrm(ks[0], (BATCH, SEQ, D), 1.0),
        'x_sample': nrm(ks[1], (DEC_BATCH, DEC_SEQ, D), 1.0),
        'state_ret': nrm(ks[2], (DEC_BATCH, DEPTH, 2, H_RET, DK_RET, DV_RET), 1.0),
        'state_dn': nrm(ks[3], (DEC_BATCH, DEPTH, 2, H_DN, DK_DN, DV_DN), 0.1),
        'c': nrm(ks[4], (DEC_BATCH, D), 1.0),
        'c_ctx': nrm(ks[5], (D,), 1.0),
        'w_ada': nrm(ks[6], (DEPTH, D, 6 * D), 0.5 * D ** -0.5),
        'b_ada': nrm(ks[7], (DEPTH, 6 * D), 0.02),
        'norm_mix': 1.0 + nrm(ks[8], (DEPTH, D), 0.1),
        'norm_ffn': 1.0 + nrm(ks[9], (DEPTH, D), 0.1),
        'w_in': nrm(ks[10], (DEPTH, D, N_IN), D ** -0.5),
        'ret_logit': ret_logit0 + nrm(ks[11], (DEPTH, 2, H_RET), 0.1),
        'ret_gn': 1.0 + nrm(ks[12], (DEPTH, RET_V), 0.1),
        'dn_conv': nrm(ks[13], (DEPTH, CONV_K, 2 * DN_QK + DN_V), CONV_K ** -0.5),
        'dn_a_log': jnp.log(jax.random.uniform(ks[15], (DEPTH, 2, H_DN), jnp.float32, 1.0, 16.0)),
        'dn_dt_bias': jnp.log(jnp.expm1(dt)),
        'dn_norm': 1.0 + nrm(ks[16], (DEPTH, DV_DN), 0.1),
        'w_br_a': nrm(ks[17], (DEPTH, RET_V, D), RET_V ** -0.5),
        'w_br_b': nrm(ks[18], (DEPTH, DN_V, D), DN_V ** -0.5),
        'w_out': nrm(ks[19], (DEPTH, D, D), D ** -0.5),
        'peer_wq': nrm(ks[20], (DEPTH, D, PEER_HEADS * PEER_QDIM), D ** -0.5),
        'peer_subkeys': nrm(ks[21], (DEPTH, 2, N_KEYS, PEER_QDIM // 2), (PEER_QDIM // 2) ** -0.5),
        'peer_u': nrm(ks[22], (DEPTH, N_EXPERTS, D), D ** -0.5),
        'peer_v': nrm(ks[23], (DEPTH, N_EXPERTS, D), 0.5),
        'final_norm': 1.0 + nrm(ks[24], (D,), 0.1),
    }


def reference(x_prompt, x_sample, state_ret, state_dn, c, c_ctx, w_ada, b_ada, norm_mix, norm_ffn,
              w_in, ret_logit, ret_gn, dn_conv, dn_a_log, dn_dt_bias, dn_norm, w_br_a, w_br_b, w_out,
              peer_wq, peer_subkeys, peer_u, peer_v, final_norm):
    n_rows = x_sample.shape[1] // GRID_W
    B = x_prompt.shape[0]
    xc, xl = x_prompt, x_sample
    zero_ret = jnp.zeros((B, 2, H_RET, DK_RET, DV_RET), jnp.float32)
    zero_dn = jnp.zeros((B, 2, H_DN, DK_DN, DV_DN), jnp.float32)
    ret_states, dn_states = [], []
    for l in range(DEPTH):
        p = {
            'norm_mix': norm_mix[l], 'norm_ffn': norm_ffn[l], 'w_in': w_in[l],
            'ret_logit': ret_logit[l], 'ret_gn': ret_gn[l], 'dn_conv': dn_conv[l],
            'dn_a_log': dn_a_log[l], 'dn_dt_bias': dn_dt_bias[l], 'dn_norm': dn_norm[l],
            'w_br_a': w_br_a[l], 'w_br_b': w_br_b[l], 'w_out': w_out[l],
            'peer_wq': peer_wq[l], 'peer_subkeys': peer_subkeys[l],
            'peer_u': peer_u[l], 'peer_v': peer_v[l],
        }
        mod_ctx = (jax.nn.silu(c_ctx) @ w_ada[l] + b_ada[l])[None, None, :]
        mod_lat = (jax.nn.silu(c) @ w_ada[l] + b_ada[l])[:, None, :]
        xc, rs, ds = trunk_layer(xc, mod_ctx, 1, zero_ret, zero_dn, p)
        ret_states.append(rs.astype(x_prompt.dtype))
        dn_states.append(ds.astype(x_prompt.dtype))
        xl, _, _ = trunk_layer(xl, mod_lat, n_rows, state_ret[:, l], state_dn[:, l], p)
    y_prompt = rmsnorm(xc, final_norm)
    y_sample = rmsnorm(xl, final_norm)
    new_state_ret = jnp.stack(ret_states, axis=1)
    new_state_dn = jnp.stack(dn_states, axis=1)
    return (y_prompt, y_sample, new_state_ret, new_state_dn)
```

```python
import functools
import math

import jax
import jax.numpy as jnp
from jax import lax
from jax.experimental import pallas as pl
from jax.experimental.pallas import tpu as pltpu

F32 = jnp.float32
BF16 = jnp.bfloat16

H_RET, DK_RET, DV_RET = 8, 128, 256
H_DN, DK_DN, DV_DN = 16, 128, 128
CONV_K = 5
GRID_W = 64
DN_CHUNK = 64
DN_BLOCK = 256
RET_CHUNK = 256
N_KEYS = 128
PEER_HEADS = 8
PEER_QDIM = 256
PEER_TOPK = 16
EPS = 1e-6

RET_QK = H_RET * DK_RET
RET_V = H_RET * DV_RET
DN_QK = H_DN * DK_DN
DN_V = H_DN * DV_DN
RQ0 = 0
RK0 = RQ0 + RET_QK
RV0 = RK0 + RET_QK
RG0 = RV0 + RET_V
DQ0 = RG0 + RET_V
DK0 = DQ0 + DN_QK
DV0 = DK0 + DN_QK
DZ0 = DV0 + DN_V
DGL0 = DZ0 + DN_V
BG0 = DZ0 + DN_V

VMEM_LIMIT = 56 * 1024 * 1024
NEG_INF = float("-inf")


def _cparams(sem):
    return pltpu.CompilerParams(dimension_semantics=sem, vmem_limit_bytes=VMEM_LIMIT)


def _nt_dot(a, b):
    return lax.dot_general(a, b, (((1,), (1,)), ((), ())), preferred_element_type=F32)


def _dot(a, b):
    return jnp.dot(a, b, preferred_element_type=F32)


def _silu(x):
    return x * jax.nn.sigmoid(x)


def _pick_tile(n, cands):
    for c in cands:
        if n % c == 0:
            return c
    raise ValueError(f"no tile for {n}")


def _adaln_kernel(c_ref, w_ref, b_ref, o_ref):
    a = _silu(c_ref[...])
    o_ref[...] = jnp.dot(a, w_ref[...], precision=lax.Precision.HIGHEST,
                         preferred_element_type=F32) + b_ref[...]


def _adaln(cvec, w_ada, b_ada):
    d, n = w_ada.shape
    tn = _pick_tile(n, (1024, 512, 256, 128))
    return pl.pallas_call(
        _adaln_kernel,
        grid=(n // tn,),
        in_specs=[pl.BlockSpec((8, d), lambda j: (0, 0)),
                  pl.BlockSpec((d, tn), lambda j: (0, j)),
                  pl.BlockSpec((1, tn), lambda j: (0, j))],
        out_specs=pl.BlockSpec((8, tn), lambda j: (0, j)),
        out_shape=jax.ShapeDtypeStruct((8, n), F32),
        compiler_params=_cparams(("parallel",)),
        name="adaln",
    )(cvec, w_ada, b_ada.reshape(1, n))


def _modulated_norm(x, g, shift, scale):
    ms = jnp.mean(x * x, axis=-1, keepdims=True)
    return (x * lax.rsqrt(ms + EPS)) * g * (1.0 + scale) + shift


def _inproj_kernel(x_ref, mod_ref, g_ref, w_ref, wg_ref, o_ref, gt_ref, h_scr):
    @pl.when(pl.program_id(1) == 0)
    def _():
        h = _modulated_norm(x_ref[...], g_ref[...], mod_ref[0:1, :], mod_ref[1:2, :]).astype(BF16)
        h_scr[...] = h
        gt_ref[...] = _nt_dot(wg_ref[...], h)

    o_ref[...] = _dot(h_scr[...], w_ref[...]).astype(o_ref.dtype)


def _inproj(x, mod8, seg_of_block, tm, norm_g, w_main, w_gate_t):
    t, d = x.shape
    nm = w_main.shape[1]
    tn = _pick_tile(nm, (1024, 512, 256))
    return pl.pallas_call(
        _inproj_kernel,
        grid=(t // tm, nm // tn),
        in_specs=[pl.BlockSpec((tm, d), lambda i, j: (i, 0)),
                  pl.BlockSpec((None, 8, d), lambda i, j: (seg_of_block(i), 0, 0)),
                  pl.BlockSpec((1, d), lambda i, j: (0, 0)),
                  pl.BlockSpec((d, tn), lambda i, j: (0, j)),
                  pl.BlockSpec((128, d), lambda i, j: (0, 0))],
        out_specs=[pl.BlockSpec((tm, tn), lambda i, j: (i, j)),
                   pl.BlockSpec((128, tm), lambda i, j: (0, i))],
        out_shape=[jax.ShapeDtypeStruct((t, nm), BF16),
                   jax.ShapeDtypeStruct((128, t), F32)],
        scratch_shapes=[pltpu.VMEM((tm, d), BF16)],
        compiler_params=_cparams(("parallel", "arbitrary")),
        name="inproj",
    )(x, mod8, norm_g.reshape(1, d), w_main, w_gate_t)


def _merge_kernel(a_ref, b_ref, wa_ref, wb_ref, ga_ref, gb_ref, o_ref):
    ya = _dot(a_ref[...], wa_ref[...])
    yb = _dot(b_ref[...], wb_ref[...])
    ga = jax.nn.sigmoid(ga_ref[...].astype(F32))
    gb = jax.nn.sigmoid(gb_ref[...].astype(F32))
    o_ref[...] = (ga * ya + gb * yb).astype(o_ref.dtype)


def _merge(ret_out, dn_out, w_a, w_b, proj, d, tm):
    t = ret_out.shape[0]
    tn = _pick_tile(d, (512, 256, 128))
    ga0 = BG0 // tn
    gb0 = (BG0 + d) // tn
    return pl.pallas_call(
        _merge_kernel,
        grid=(t // tm, d // tn),
        in_specs=[pl.BlockSpec((tm, RET_V), lambda i, j: (i, 0)),
                  pl.BlockSpec((tm, DN_V), lambda i, j: (i, 0)),
                  pl.BlockSpec((RET_V, tn), lambda i, j: (0, j)),
                  pl.BlockSpec((DN_V, tn), lambda i, j: (0, j)),
                  pl.BlockSpec((tm, tn), lambda i, j: (i, ga0 + j)),
                  pl.BlockSpec((tm, tn), lambda i, j: (i, gb0 + j))],
        out_specs=pl.BlockSpec((tm, tn), lambda i, j: (i, j)),
        out_shape=jax.ShapeDtypeStruct((t, d), BF16),
        compiler_params=_cparams(("parallel", "arbitrary")),
        name="merge",
    )(ret_out, dn_out, w_a, w_b, proj, proj)


def _outproj_kernel(m_ref, w_ref, x_ref, mod_ref, o_ref):
    y = _dot(m_ref[...], w_ref[...])
    o_ref[...] = x_ref[...] + mod_ref[2:3, :] * y


def _outproj(merged, w_out, x, mod8, seg_of_block, tm):
    t, d = x.shape
    tn = _pick_tile(d, (512, 256, 128))
    return pl.pallas_call(
        _outproj_kernel,
        grid=(t // tm, d // tn),
        in_specs=[pl.BlockSpec((tm, d), lambda i, j: (i, 0)),
                  pl.BlockSpec((d, tn), lambda i, j: (0, j)),
                  pl.BlockSpec((tm, tn), lambda i, j: (i, j)),
                  pl.BlockSpec((None, 8, tn), lambda i, j: (seg_of_block(i), 0, j))],
        out_specs=pl.BlockSpec((tm, tn), lambda i, j: (i, j)),
        out_shape=jax.ShapeDtypeStruct((t, d), F32),
        compiler_params=_cparams(("parallel", "arbitrary")),
        name="outproj",
    )(merged, w_out, x, mod8)


def _peerq_kernel(x_ref, mod_ref, g_ref, w_ref, h_ref, q_ref, h_scr):
    @pl.when(pl.program_id(1) == 0)
    def _():
        h = _modulated_norm(x_ref[...], g_ref[...], mod_ref[3:4, :], mod_ref[4:5, :]).astype(BF16)
        h_scr[...] = h
        h_ref[...] = h

    q = _dot(h_scr[...], w_ref[...]).astype(BF16)
    for i in range(q_ref.shape[0]):
        q_ref[i] = q[:, i * 128:(i + 1) * 128]


def _peerq(x1, mod8, seg_of_block, tm, norm_g, w_q):
    t, d = x1.shape
    nq = w_q.shape[1]
    tn = _pick_tile(nq, (512, 256, 128))
    return pl.pallas_call(
        _peerq_kernel,
        grid=(t // tm, nq // tn),
        in_specs=[pl.BlockSpec((tm, d), lambda i, j: (i, 0)),
                  pl.BlockSpec((None, 8, d), lambda i, j: (seg_of_block(i), 0, 0)),
                  pl.BlockSpec((1, d), lambda i, j: (0, 0)),
                  pl.BlockSpec((d, tn), lambda i, j: (0, j))],
        out_specs=[pl.BlockSpec((tm, d), lambda i, j: (i, 0)),
                   pl.BlockSpec((tn // 128, tm, 128), lambda i, j: (j, i, 0))],
        out_shape=[jax.ShapeDtypeStruct((t, d), BF16),
                   jax.ShapeDtypeStruct((nq // 128, t, 128), BF16)],
        scratch_shapes=[pltpu.VMEM((tm, d), BF16)],
        compiler_params=_cparams(("parallel", "arbitrary")),
        name="peer_query",
    )(x1, mod8, norm_g.reshape(1, d), w_q)


def _ret_kernel(lg_ref, q_ref, k_ref, v_ref, g_ref, gn_ref, *rest, n_chunks, has_init, emit_state):
    rest = list(rest)
    s0_ref = rest.pop(0) if has_init else None
    o_ref = rest.pop(0)
    st_ref = rest.pop(0) if emit_state else None
    if n_chunks > 1:
        o_scr, sf_scr, sb_scr = rest
    c = RET_CHUNK
    scale = DK_RET ** -0.5
    h = pl.program_id(1)
    lgf = lg_ref[0, h]
    lgb = lg_ref[1, h]

    row = lax.broadcasted_iota(jnp.int32, (c, c), 0)
    col = lax.broadcasted_iota(jnp.int32, (c, c), 1)
    dlt = (row - col).astype(F32)
    dmat = (jnp.where(dlt >= 0, jnp.exp(lgf * jnp.maximum(dlt, 0.0)), 0.0)
            + jnp.where(dlt <= 0, jnp.exp(lgb * jnp.maximum(-dlt, 0.0)), 0.0)) * scale
    pos = lax.broadcasted_iota(jnp.int32, (c, DK_RET), 0).astype(F32)
    xi_f = jnp.exp(lgf * (pos + 1.0))
    xi_b = jnp.exp(lgb * (c - pos))
    zeta_f = jnp.exp(lgf * (c - 1.0 - pos)) * scale
    zeta_b = jnp.exp(lgb * pos) * scale
    dec_f = jnp.exp(jnp.full((1, DV_RET), lgf, F32) * float(c))
    dec_b = jnp.exp(jnp.full((1, DV_RET), lgb, F32) * float(c))

    def rows(i):
        return pl.ds(pl.multiple_of(i * c, c), c)

    def intra(r):
        q = q_ref[r, :]
        k = k_ref[r, :]
        v = v_ref[r, :]
        s = _nt_dot(q, k) * dmat
        return q.astype(F32), k.astype(F32), v, _dot(s.astype(BF16), v)

    def state_inc(kf, zeta, v):
        return _dot((kf * zeta).T.astype(BF16), v)

    def finish(o, r):
        mu = jnp.mean(o, axis=-1, keepdims=True)
        oc = o - mu
        var = jnp.mean(oc * oc, axis=-1, keepdims=True)
        on = oc * lax.rsqrt(var + EPS)
        o_ref[r, :] = (on * gn_ref[...] * _silu(g_ref[r, :].astype(F32))).astype(o_ref.dtype)

    if n_chunks == 1:
        r = pl.ds(0, c)
        qf, kf, v, o = intra(r)
        s_f = state_inc(kf, zeta_f, v)
        s_b = state_inc(kf, zeta_b, v)
        if has_init:
            s0f = s0_ref[0]
            s0b = s0_ref[1]
            o = o + _dot((qf * xi_f).astype(BF16), s0f.astype(BF16))
            o = o + _dot((qf * xi_b).astype(BF16), s0b.astype(BF16))
            s_f = s_f + dec_f * s0f
            s_b = s_b + dec_b * s0b
        finish(o, r)
        if emit_state:
            st_ref[0] = s_f
            st_ref[1] = s_b
        return

    if has_init:
        sf_scr[...] = s0_ref[0]
        sb_scr[...] = s0_ref[1]
    else:
        sf_scr[...] = jnp.zeros_like(sf_scr)
        sb_scr[...] = jnp.zeros_like(sb_scr)

    def fwd(i, carry):
        r = rows(i)
        qf, kf, v, o = intra(r)
        s = sf_scr[...]
        o_scr[r, :] = o + _dot((qf * xi_f).astype(BF16), s.astype(BF16))
        sf_scr[...] = dec_f * s + state_inc(kf, zeta_f, v)
        return carry

    lax.fori_loop(0, n_chunks, fwd, 0)

    def bwd(ii, carry):
        i = n_chunks - 1 - ii
        r = rows(i)
        qf = q_ref[r, :].astype(F32)
        kf = k_ref[r, :].astype(F32)
        v = v_ref[r, :]
        s = sb_scr[...]
        o = o_scr[r, :] + _dot((qf * xi_b).astype(BF16), s.astype(BF16))
        sb_scr[...] = dec_b * s + state_inc(kf, zeta_b, v)
        finish(o, r)
        return carry

    lax.fori_loop(0, n_chunks, bwd, 0)
    if emit_state:
        st_ref[0] = sf_scr[...]
        st_ref[1] = sb_scr[...]


def _retention(proj, lg, ret_gn, s0, *, n_batch, seq, row0, emit_state):
    n_chunks = seq // RET_CHUNK
    rb0 = row0 // seq
    has_init = s0 is not None
    in_specs = [pl.BlockSpec(memory_space=pltpu.SMEM),
                pl.BlockSpec((seq, DK_RET), lambda b, h: (rb0 + b, RQ0 // DK_RET + h)),
                pl.BlockSpec((seq, DK_RET), lambda b, h: (rb0 + b, RK0 // DK_RET + h)),
                pl.BlockSpec((seq, DV_RET), lambda b, h: (rb0 + b, RV0 // DV_RET + h)),
                pl.BlockSpec((seq, DV_RET), lambda b, h: (rb0 + b, RG0 // DV_RET + h)),
                pl.BlockSpec((1, DV_RET), lambda b, h: (0, h))]
    args = [lg, proj, proj, proj, proj, ret_gn.reshape(1, RET_V)]
    if has_init:
        in_specs.append(pl.BlockSpec((None, 2, None, DK_RET, DV_RET), lambda b, h: (b, 0, h, 0, 0)))
        args.append(s0)
    out_specs = [pl.BlockSpec((seq, DV_RET), lambda b, h: (b, h))]
    out_shape = [jax.ShapeDtypeStruct((n_batch * seq, RET_V), BF16)]
    if emit_state:
        out_specs.append(pl.BlockSpec((None, 2, None, DK_RET, DV_RET), lambda b, h: (b, 0, h, 0, 0)))
        out_shape.append(jax.ShapeDtypeStruct((n_batch, 2, H_RET, DK_RET, DV_RET), F32))
    scratch = []
    if n_chunks > 1:
        scratch = [pltpu.VMEM((seq, DV_RET), F32), pltpu.VMEM((DK_RET, DV_RET), F32),
                   pltpu.VMEM((DK_RET, DV_RET), F32)]
    return pl.pallas_call(
        functools.partial(_ret_kernel, n_chunks=n_chunks, has_init=has_init, emit_state=emit_state),
        grid=(n_batch, H_RET),
        in_specs=in_specs,
        out_specs=out_specs,
        out_shape=out_shape,
        scratch_shapes=scratch,
        compiler_params=_cparams(("parallel", "arbitrary")),
        name="retention_ctx" if emit_state else "retention_lat",
    )(*args)


def _dn_kernel(par_ref, q_ref, k_ref, v_ref, z_ref, gt_ref, cq_ref, ck_ref, cv_ref, nrm_ref, *rest,
               seq, row_len, has_init, emit_state):
    rest = list(rest)
    s0_ref = rest.pop(0) if has_init else None
    o_ref = rest.pop(0)
    st_ref = rest.pop(0) if emit_state else None
    qs, ks, vs, rows_scr, o_scr, s_scr, vn_scr = rest
    cb = DN_BLOCK
    cc = DN_CHUNK
    n_blocks = seq // cb
    n_sub = cb // cc
    h = pl.program_id(1)

    tpos = lax.broadcasted_iota(jnp.int32, (seq, DK_DN), 0) & (row_len - 1)

    def conv(x_ref, w_ref):
        x = x_ref[...].astype(F32)
        acc = x * w_ref[CONV_K // 2:CONV_K // 2 + 1, :]
        for i in range(CONV_K):
            d = i - CONV_K // 2
            if d == 0:
                continue
            sh = pltpu.roll(x, (-d) % seq, 0)
            ok = (tpos + d >= 0) & (tpos + d < row_len)
            acc = acc + jnp.where(ok, sh, 0.0) * w_ref[i:i + 1, :]
        return _silu(acc)

    def l2n(x):
        return x * lax.rsqrt(jnp.sum(x * x, axis=-1, keepdims=True) + EPS)

    qs[...] = l2n(conv(q_ref, cq_ref)) * (DK_DN ** -0.5)
    ks[...] = l2n(conv(k_ref, ck_ref))
    vs[...] = conv(v_ref, cv_ref)

    lane = lax.broadcasted_iota(jnp.int32, (1, seq), 1) & (cc - 1)

    def splat(v):
        return jnp.full((1, seq), v, F32)

    beta_f = jax.nn.sigmoid(gt_ref[0:1, :])
    beta_b = jax.nn.sigmoid(gt_ref[1:2, :])
    g_f = -jnp.exp(splat(par_ref[0, h])) * jax.nn.softplus(gt_ref[2:3, :] + splat(par_ref[2, h]))
    g_b = -jnp.exp(splat(par_ref[1, h])) * jax.nn.softplus(gt_ref[3:4, :] + splat(par_ref[3, h]))

    def prefix(x):
        s = 1
        while s < cc:
            x = x + jnp.where(lane >= s, pltpu.roll(x, s, 1), 0.0)
            s *= 2
        return x

    def suffix(x):
        s = 1
        while s < cc:
            x = x + jnp.where(lane < cc - s, pltpu.roll(x, seq - s, 1), 0.0)
            s *= 2
        return x

    pf = prefix(g_f)
    sf = suffix(g_f)
    pb = prefix(g_b)
    sb = suffix(g_b)
    rows_scr[0:1, :] = beta_f
    rows_scr[1:2, :] = pf
    rows_scr[2:3, :] = sf - g_f
    rows_scr[3:4, :] = beta_b
    rows_scr[4:5, :] = sb
    rows_scr[5:6, :] = pb - g_b
    rows_scr[6:8, :] = jnp.zeros((2, seq), F32)

    ri = lax.broadcasted_iota(jnp.int32, (cb, cb), 0)
    ci = lax.broadcasted_iota(jnp.int32, (cb, cb), 1)
    same = (ri // cc) == (ci // cc)
    eye = (ri == ci).astype(F32)

    if has_init:
        s_init = (s0_ref[0], s0_ref[1])
    else:
        s_init = (jnp.zeros((DK_DN, DV_DN), F32), jnp.zeros((DK_DN, DV_DN), F32))

    def col_of(row):
        parts = []
        for t in range(cb // 128):
            parts.append(jnp.broadcast_to(row[:, t * 128:(t + 1) * 128], (128, 128)).T)
        return jnp.concatenate(parts, axis=0)

    def block(bi, direction):
        off = pl.multiple_of(bi * cb, cb)
        r = pl.ds(off, cb)
        base = 0 if direction == 0 else 3
        beta_row = rows_scr[base:base + 1, r]
        g_row = rows_scr[base + 1:base + 2, r]
        e_row = rows_scr[base + 2:base + 3, r]
        beta_c = col_of(beta_row)
        g_c = col_of(g_row)
        dec_c = jnp.exp(col_of(e_row))
        eg_c = jnp.exp(g_c)
        q = qs[r, :]
        k = ks[r, :]
        v = vs[r, :]
        k16 = k.astype(BF16)
        kk = _nt_dot(k16, k16)
        qk = _nt_dot(q.astype(BF16), k16)
        g_cw = jnp.concatenate([g_c] * (cb // 128), axis=1)
        beta_cw = jnp.concatenate([beta_c] * (cb // 128), axis=1)
        if direction == 0:
            incl = same & (ri >= ci)
            strict = same & (ri > ci)
        else:
            incl = same & (ri <= ci)
            strict = same & (ri < ci)
        lmat = jnp.exp(jnp.where(incl, g_cw - g_row, NEG_INF))
        m = jnp.where(strict, kk * beta_cw * lmat, 0.0)
        x = eye - jnp.where((ri // 2) == (ci // 2), m, 0.0)
        sz = 2
        while sz < cc:
            pair = ((ri // (2 * sz)) == (ci // (2 * sz))) & ((ri // sz) != (ci // sz))
            x16 = x.astype(BF16)
            x = x - _dot(_dot(x16, jnp.where(pair, m, 0.0).astype(BF16)).astype(BF16), x16)
            sz *= 2
        rhs = jnp.concatenate([v * beta_c, k * (beta_c * eg_c)], axis=1).astype(BF16)
        uw = _dot(x.astype(BF16), rhs)
        u = uw[:, :DV_DN]
        w16 = uw[:, DV_DN:].astype(BF16)
        aqk = jnp.where(incl, qk * lmat, 0.0).astype(BF16)
        qg16 = (q * eg_c).astype(BF16)
        kdec = k * dec_c
        vn_scr[direction] = jnp.zeros((cb, DV_DN), BF16)
        outs = [None] * n_sub
        for ci_ in range(n_sub):
            c = ci_ if direction == 0 else n_sub - 1 - ci_
            rc = slice(c * cc, (c + 1) * cc)
            s = s_scr[direction]
            s16 = s.astype(BF16)
            v_new = u[rc, :] - _dot(w16[rc, :], s16)
            vn16 = v_new.astype(BF16)
            vn_scr[direction, rc, :] = vn16
            outs[c] = _dot(qg16[rc, :], s16) + _dot(aqk[rc, :], vn_scr[direction])
            last = (c + 1) * cc - 1 if direction == 0 else c * cc
            dl = jnp.exp(g_c[last:last + 1, :])
            s_scr[direction] = s * dl + _dot(kdec[rc, :].T.astype(BF16), vn16)
        return outs

    def stash(bi, outs):
        for c in range(n_sub):
            o_scr[pl.ds(bi * cb + c * cc, cc), :] = outs[c]

    def finish(bi, outs, other=None):
        for c in range(n_sub):
            ro = pl.ds(bi * cb + c * cc, cc)
            d = outs[c] + (o_scr[ro, :] if other is None else other[c])
            d = d * lax.rsqrt(jnp.mean(d * d, axis=-1, keepdims=True) + EPS)
            o_ref[ro, :] = (d * nrm_ref[...] * _silu(z_ref[ro, :].astype(F32))).astype(o_ref.dtype)

    s_scr[0] = s_init[0]
    s_scr[1] = s_init[1]
    if n_blocks == 1:
        finish(0, block(0, 0), block(0, 1))
    else:
        def first_half(it, carry):
            stash(it, block(it, 0))
            stash(n_blocks - 1 - it, block(n_blocks - 1 - it, 1))
            return carry

        def second_half(it, carry):
            finish(it, block(it, 0))
            finish(n_blocks - 1 - it, block(n_blocks - 1 - it, 1))
            return carry

        lax.fori_loop(0, n_blocks // 2, first_half, 0)
        lax.fori_loop(n_blocks // 2, n_blocks, second_half, 0)
    if emit_state:
        st_ref[0] = s_scr[0]
        st_ref[1] = s_scr[1]


def _deltanet(proj, gates_t, dn_par, dn_conv, dn_norm, s0, *, n_batch, seq, row0, row_len, emit_state):
    rb0 = row0 // seq
    has_init = s0 is not None
    in_specs = [pl.BlockSpec(memory_space=pltpu.SMEM),
                pl.BlockSpec((seq, DK_DN), lambda b, h: (rb0 + b, DQ0 // DK_DN + h)),
                pl.BlockSpec((seq, DK_DN), lambda b, h: (rb0 + b, DK0 // DK_DN + h)),
                pl.BlockSpec((seq, DV_DN), lambda b, h: (rb0 + b, DV0 // DV_DN + h)),
                pl.BlockSpec((seq, DV_DN), lambda b, h: (rb0 + b, DZ0 // DV_DN + h)),
                pl.BlockSpec((8, seq), lambda b, h: (h, rb0 + b)),
                pl.BlockSpec((CONV_K, DK_DN), lambda b, h: (0, h)),
                pl.BlockSpec((CONV_K, DK_DN), lambda b, h: (0, H_DN + h)),
                pl.BlockSpec((CONV_K, DV_DN), lambda b, h: (0, 2 * H_DN + h)),
                pl.BlockSpec((1, DV_DN), lambda b, h: (0, 0))]
    args = [dn_par, proj, proj, proj, proj, gates_t, dn_conv, dn_conv, dn_conv, dn_norm.reshape(1, DV_DN)]
    if has_init:
        in_specs.append(pl.BlockSpec((None, 2, None, DK_DN, DV_DN), lambda b, h: (b, 0, h, 0, 0)))
        args.append(s0)
    out_specs = [pl.BlockSpec((seq, DV_DN), lambda b, h: (b, h))]
    out_shape = [jax.ShapeDtypeStruct((n_batch * seq, DN_V), BF16)]
    if emit_state:
        out_specs.append(pl.BlockSpec((None, 2, None, DK_DN, DV_DN), lambda b, h: (b, 0, h, 0, 0)))
        out_shape.append(jax.ShapeDtypeStruct((n_batch, 2, H_DN, DK_DN, DV_DN), F32))
    scratch = [pltpu.VMEM((seq, DK_DN), F32), pltpu.VMEM((seq, DK_DN), F32), pltpu.VMEM((seq, DV_DN), F32),
               pltpu.VMEM((8, seq), F32), pltpu.VMEM((seq, DV_DN), F32), pltpu.VMEM((2, DK_DN, DV_DN), F32),
               pltpu.VMEM((2, DN_BLOCK, DV_DN), BF16)]
    assert seq == DN_BLOCK or (seq // DN_BLOCK) % 2 == 0
    return pl.pallas_call(
        functools.partial(_dn_kernel, seq=seq, row_len=row_len, has_init=has_init, emit_state=emit_state),
        grid=(n_batch, H_DN),
        in_specs=in_specs,
        out_specs=out_specs,
        out_shape=out_shape,
        scratch_shapes=scratch,
        compiler_params=_cparams(("parallel", "arbitrary")),
        name="deltanet_ctx" if emit_state else "deltanet_lat",
    )(*args)


def _cand_pairs():
    n = PEER_TOPK + 1
    return [(a, b) for a in range(n) for b in range(n) if (a + 1) * (b + 1) <= n]


def _peer_score_kernel(q_ref, sk_ref, pk_ref, s1_scr, sv_scr, gp_scr):
    def per_head(h, carry):
        for p in range(2):
            s = _nt_dot(sk_ref[p], q_ref[2 * h + p])
            if p == 0:
                s1_scr[h] = s
            else:
                pk_ref[0, h] = s
            cur = s
            for r in range(PEER_TOPK + 1):
                m = jnp.max(cur, axis=0, keepdims=True)
                sv_scr[p, r, pl.ds(h, 1), :] = m
                cur = jnp.where(cur == m, NEG_INF, cur)
        return carry

    lax.fori_loop(0, PEER_HEADS, per_head, 0)
    cands = [sv_scr[0, a] + sv_scr[1, b] for a, b in _cand_pairs()]
    mx = cands[0]
    z = jnp.zeros_like(mx)
    for r in range(PEER_TOPK):
        m = functools.reduce(jnp.maximum, cands)
        z = z + jnp.exp(m - mx)
        cands = [jnp.where(c == m, NEG_INF, c) for c in cands]
    runner_up = functools.reduce(jnp.maximum, cands)
    gp_scr[0] = 0.5 * (m + runner_up)
    gp_scr[1] = 1.0 / z

    def finish_head(h, carry):
        row = pl.ds(h, 1)
        s1 = s1_scr[h]
        pk_ref[1, h] = jnp.exp(pk_ref[0, h] - sv_scr[1, 0, row, :])
        pk_ref[2, h] = jnp.exp(s1 - sv_scr[0, 0, row, :]) * gp_scr[1, row, :]
        pk_ref[3, h] = gp_scr[0, row, :] - s1
        return carry

    lax.fori_loop(0, PEER_HEADS, finish_head, 0)


def _peer_scores(q3, subkeys, tb):
    nhp, t, _ = q3.shape
    return pl.pallas_call(
        _peer_score_kernel,
        grid=(t // tb,),
        in_specs=[pl.BlockSpec((nhp, tb, 128), lambda i: (0, i, 0)),
                  pl.BlockSpec((2, N_KEYS, PEER_QDIM // 2), lambda i: (0, 0, 0))],
        out_specs=pl.BlockSpec((4, PEER_HEADS, N_KEYS, tb), lambda i: (0, 0, 0, i)),
        out_shape=jax.ShapeDtypeStruct((4, PEER_HEADS, N_KEYS, t), F32),
        scratch_shapes=[pltpu.VMEM((PEER_HEADS, N_KEYS, tb), F32),
                        pltpu.VMEM((2, PEER_TOPK + 1, PEER_HEADS, tb), F32),
                        pltpu.VMEM((2, PEER_HEADS, tb), F32)],
        compiler_params=_cparams(("parallel",)),
        name="peer_scores",
    )(q3, subkeys)


def _gelu_tanh(x):
    return 0.5 * x * (1.0 + jnp.tanh(0.7978845608028654 * (x + 0.044715 * (x * x * x))))


def _peer_dense_kernel(h_ref, pk_ref, u_ref, vt_ref, o_ref, w_scr, *, n_i1):
    e = pl.program_id(1)
    n_e = pl.num_programs(1)

    def build_gates(tile, slot):
        for ii in range(n_i1):
            row = pl.ds(tile * n_i1 + ii, 1)
            acc = None
            for h in range(PEER_HEADS):
                g = jnp.where(pk_ref[0, h] >= pk_ref[3, h, row, :], pk_ref[1, h] * pk_ref[2, h, row, :], 0.0)
                acc = g if acc is None else acc + g
            w_scr[slot, ii * N_KEYS:(ii + 1) * N_KEYS, :] = acc

    @pl.when(e == 0)
    def _():
        o_ref[...] = jnp.zeros_like(o_ref)
        build_gates(0, 0)

    slot = e % 2
    build_gates(jnp.minimum(e + 1, n_e - 1), 1 - slot)
    act = _gelu_tanh(_nt_dot(u_ref[...], h_ref[...]))
    wa = (w_scr[slot] * act).astype(BF16)
    o_ref[...] += _dot(vt_ref[...], wa)


def _peer_dense(h2, pk, u_tab, v_tab_t, tb, et):
    t, d = h2.shape
    ne = u_tab.shape[0]
    once = pl.Buffered(1)
    return pl.pallas_call(
        functools.partial(_peer_dense_kernel, n_i1=et // N_KEYS),
        grid=(t // tb, ne // et),
        in_specs=[pl.BlockSpec((tb, d), lambda i, e: (i, 0), pipeline_mode=once),
                  pl.BlockSpec((4, PEER_HEADS, N_KEYS, tb), lambda i, e: (0, 0, 0, i), pipeline_mode=once),
                  pl.BlockSpec((et, d), lambda i, e: (e, 0)),
                  pl.BlockSpec((d, et), lambda i, e: (0, e))],
        out_specs=pl.BlockSpec((d, tb), lambda i, e: (0, i)),
        out_shape=jax.ShapeDtypeStruct((d, t), F32),
        scratch_shapes=[pltpu.VMEM((2, et, tb), F32)],
        compiler_params=_cparams(("parallel", "arbitrary")),
        name="peer_dense",
    )(h2, pk, u_tab, v_tab_t)


def _final_kernel(x_ref, p_ref, mod_ref, g_ref, o_ref):
    x = x_ref[...] + mod_ref[5:6, :] * p_ref[...].T
    ms = jnp.mean(x * x, axis=-1, keepdims=True)
    o_ref[...] = (x * lax.rsqrt(ms + EPS)) * g_ref[...]


def _final(x1, peer_t, mod8, seg_of_block, tm_seg, final_norm):
    t, d = x1.shape
    tm = 256
    sub = tm_seg // tm
    return pl.pallas_call(
        _final_kernel,
        grid=(t // tm,),
        in_specs=[pl.BlockSpec((tm, d), lambda i: (i, 0)),
                  pl.BlockSpec((d, tm), lambda i: (0, i)),
                  pl.BlockSpec((None, 8, d), lambda i: (seg_of_block(i // sub), 0, 0)),
                  pl.BlockSpec((1, d), lambda i: (0, 0))],
        out_specs=pl.BlockSpec((tm, d), lambda i: (i, 0)),
        out_shape=jax.ShapeDtypeStruct((t, d), F32),
        compiler_params=_cparams(("parallel",)),
        name="final_norm",
    )(x1, peer_t, mod8, final_norm.reshape(1, d))


def kernel(x_prompt, x_sample, state_ret, state_dn, c, c_ctx, w_ada, b_ada, norm_mix, norm_ffn, w_in, ret_logit,
           ret_gn, dn_conv, dn_a_log, dn_dt_bias, dn_norm, w_br_a, w_br_b, w_out, peer_wq, peer_subkeys, peer_u,
           peer_v, final_norm):
    bc, lc, d = x_prompt.shape
    bl, ll, _ = x_sample.shape
    depth = w_ada.shape[0]
    tc = bc * lc
    t = tc + bl * ll
    assert bl + 1 <= 8 and tc % ll == 0 and lc % DN_BLOCK == 0 and ll % DN_BLOCK == 0

    tm = _pick_tile(math.gcd(tc, ll), (1024, 512, 256))
    n_ctx_blocks = tc // tm
    lat_blocks = ll // tm

    def seg_of_block(i):
        return jnp.where(i < n_ctx_blocks, 0, 1 + (i - n_ctx_blocks) // lat_blocks)

    x = jnp.concatenate([x_prompt.reshape(tc, d), x_sample.reshape(bl * ll, d)], axis=0)
    cvec = jnp.concatenate([c_ctx[None, :], c, jnp.zeros((8 - 1 - bl, d), F32)], axis=0)

    ret_states, dn_states = [], []
    for l in range(depth):
        mod = _adaln(cvec, w_ada[l], b_ada[l])
        mod8 = jnp.pad(mod.reshape(8, 6, d), ((0, 0), (0, 2), (0, 0)))

        w = w_in[l]
        w_main = jnp.concatenate([w[:, :DGL0], w[:, DGL0 + 4 * H_DN:]], axis=1).astype(BF16)
        wg = w[:, DGL0:DGL0 + 4 * H_DN].reshape(d, 4, H_DN).transpose(2, 1, 0)
        w_gate_t = jnp.pad(wg, ((0, 0), (0, 4), (0, 0))).reshape(8 * H_DN, d).astype(BF16)

        proj, gates_t = _inproj(x, mod8, seg_of_block, tm, norm_mix[l], w_main, w_gate_t)

        lg = jax.nn.log_sigmoid(ret_logit[l].astype(F32))
        ret_c, rs = _retention(proj, lg, ret_gn[l], None, n_batch=bc, seq=lc, row0=0, emit_state=True)
        (ret_l,) = _retention(proj, lg, ret_gn[l], state_ret[:, l], n_batch=bl, seq=ll, row0=tc,
                              emit_state=False)
        dn_par = jnp.concatenate([dn_a_log[l], dn_dt_bias[l]], axis=0).astype(F32)
        dn_c, ds = _deltanet(proj, gates_t, dn_par, dn_conv[l], dn_norm[l], None, n_batch=bc, seq=lc, row0=0,
                             row_len=lc, emit_state=True)
        (dn_l,) = _deltanet(proj, gates_t, dn_par, dn_conv[l], dn_norm[l], state_dn[:, l], n_batch=bl, seq=ll,
                            row0=tc, row_len=GRID_W, emit_state=False)
        ret_states.append(rs)
        dn_states.append(ds)
        ret_out = jnp.concatenate([ret_c, ret_l], axis=0)
        dn_out = jnp.concatenate([dn_c, dn_l], axis=0)

        merged = _merge(ret_out, dn_out, w_br_a[l].astype(BF16), w_br_b[l].astype(BF16), proj, d, tm)
        x1 = _outproj(merged, w_out[l].astype(BF16), x, mod8, seg_of_block, tm)

        h2, q3 = _peerq(x1, mod8, seg_of_block, tm, norm_ffn[l], peer_wq[l].astype(BF16))
        tb = min(512, tm)
        pk = _peer_scores(q3, peer_subkeys[l].astype(BF16), tb)
        peer_t = _peer_dense(h2, pk, peer_u[l].astype(BF16), peer_v[l].T.astype(BF16), tb, 1024)
        x = x1
        x_last_peer = peer_t
        assert depth == 1

    y = _final(x, x_last_peer, mod8, seg_of_block, tm, final_norm)
    y_prompt = y[:tc].reshape(bc, lc, d)
    y_sample = y[tc:].reshape(bl, ll, d)
    new_state_ret = jnp.stack(ret_states, axis=1)
    new_state_dn = jnp.stack(dn_states, axis=1)
    return (y_prompt, y_sample, new_state_ret, new_state_dn)
```

```python
import functools
import math

import jax
import jax.numpy as jnp
from jax import lax
from jax.experimental import pallas as pl
from jax.experimental.pallas import tpu as pltpu

F32 = jnp.float32
BF16 = jnp.bfloat16

H_RET, DK_RET, DV_RET = 8, 128, 256
H_DN, DK_DN, DV_DN = 16, 128, 128
CONV_K = 5
GRID_W = 64
DN_CHUNK = 256
RET_CHUNK = 256
N_KEYS = 128
PEER_HEADS = 8
PEER_QDIM = 256
PEER_TOPK = 16
EPS = 1e-6

RET_QK = H_RET * DK_RET
RET_V = H_RET * DV_RET
DN_QK = H_DN * DK_DN
DN_V = H_DN * DV_DN
RQ0 = 0
RK0 = RQ0 + RET_QK
RV0 = RK0 + RET_QK
RG0 = RV0 + RET_V
DQ0 = RG0 + RET_V
DK0 = DQ0 + DN_QK
DV0 = DK0 + DN_QK
DZ0 = DV0 + DN_V
DGL0 = DZ0 + DN_V
BG0 = DZ0 + DN_V

VMEM_LIMIT = 56 * 1024 * 1024
NEG_INF = float("-inf")


def _cparams(sem):
    return pltpu.CompilerParams(dimension_semantics=sem, vmem_limit_bytes=VMEM_LIMIT)


def _nt_dot(a, b):
    return lax.dot_general(a, b, (((1,), (1,)), ((), ())), preferred_element_type=F32)


def _dot(a, b):
    return jnp.dot(a, b, preferred_element_type=F32)


def _silu(x):
    return x * jax.nn.sigmoid(x)


def _pick_tile(n, cands):
    for c in cands:
        if n % c == 0:
            return c
    raise ValueError(f"no tile for {n}")


def _adaln_kernel(c_ref, w_ref, b_ref, o_ref):
    a = _silu(c_ref[...])
    o_ref[...] = jnp.dot(a, w_ref[...], precision=lax.Precision.HIGHEST,
                         preferred_element_type=F32) + b_ref[...]


def _adaln(cvec, w_ada, b_ada):
    d, n = w_ada.shape
    tn = _pick_tile(n, (1024, 512, 256, 128))
    return pl.pallas_call(
        _adaln_kernel,
        grid=(n // tn,),
        in_specs=[pl.BlockSpec((8, d), lambda j: (0, 0)),
                  pl.BlockSpec((d, tn), lambda j: (0, j)),
                  pl.BlockSpec((1, tn), lambda j: (0, j))],
        out_specs=pl.BlockSpec((8, tn), lambda j: (0, j)),
        out_shape=jax.ShapeDtypeStruct((8, n), F32),
        compiler_params=_cparams(("parallel",)),
        name="adaln",
    )(cvec, w_ada, b_ada.reshape(1, n))


def _modulated_norm(x, g, shift, scale):
    ms = jnp.mean(x * x, axis=-1, keepdims=True)
    return (x * lax.rsqrt(ms + EPS)) * g * (1.0 + scale) + shift


def _inproj_kernel(x_ref, mod_ref, g_ref, w_ref, wg_ref, o_ref, gt_ref, h_scr):
    @pl.when(pl.program_id(1) == 0)
    def _():
        h = _modulated_norm(x_ref[...], g_ref[...], mod_ref[0:1, :], mod_ref[1:2, :]).astype(BF16)
        h_scr[...] = h
        gt_ref[...] = _nt_dot(wg_ref[...], h)

    o_ref[...] = _dot(h_scr[...], w_ref[...]).astype(o_ref.dtype)


def _inproj(x, mod8, seg_of_block, tm, norm_g, w_main, w_gate_t):
    t, d = x.shape
    nm = w_main.shape[1]
    tn = _pick_tile(nm, (1024, 512, 256))
    return pl.pallas_call(
        _inproj_kernel,
        grid=(t // tm, nm // tn),
        in_specs=[pl.BlockSpec((tm, d), lambda i, j: (i, 0)),
                  pl.BlockSpec((None, 8, d), lambda i, j: (seg_of_block(i), 0, 0)),
                  pl.BlockSpec((1, d), lambda i, j: (0, 0)),
                  pl.BlockSpec((d, tn), lambda i, j: (0, j)),
                  pl.BlockSpec((128, d), lambda i, j: (0, 0))],
        out_specs=[pl.BlockSpec((tm, tn), lambda i, j: (i, j)),
                   pl.BlockSpec((128, tm), lambda i, j: (0, i))],
        out_shape=[jax.ShapeDtypeStruct((t, nm), BF16),
                   jax.ShapeDtypeStruct((128, t), F32)],
        scratch_shapes=[pltpu.VMEM((tm, d), BF16)],
        compiler_params=_cparams(("parallel", "arbitrary")),
        name="inproj",
    )(x, mod8, norm_g.reshape(1, d), w_main, w_gate_t)


def _merge_kernel(a_ref, b_ref, wa_ref, wb_ref, ga_ref, gb_ref, o_ref):
    ya = _dot(a_ref[...], wa_ref[...])
    yb = _dot(b_ref[...], wb_ref[...])
    ga = jax.nn.sigmoid(ga_ref[...].astype(F32))
    gb = jax.nn.sigmoid(gb_ref[...].astype(F32))
    o_ref[...] = (ga * ya + gb * yb).astype(o_ref.dtype)


def _merge(ret_out, dn_out, w_a, w_b, proj, d, tm):
    t = ret_out.shape[0]
    tn = _pick_tile(d, (512, 256, 128))
    ga0 = BG0 // tn
    gb0 = (BG0 + d) // tn
    return pl.pallas_call(
        _merge_kernel,
        grid=(t // tm, d // tn),
        in_specs=[pl.BlockSpec((tm, RET_V), lambda i, j: (i, 0)),
                  pl.BlockSpec((tm, DN_V), lambda i, j: (i, 0)),
                  pl.BlockSpec((RET_V, tn), lambda i, j: (0, j)),
                  pl.BlockSpec((DN_V, tn), lambda i, j: (0, j)),
                  pl.BlockSpec((tm, tn), lambda i, j: (i, ga0 + j)),
                  pl.BlockSpec((tm, tn), lambda i, j: (i, gb0 + j))],
        out_specs=pl.BlockSpec((tm, tn), lambda i, j: (i, j)),
        out_shape=jax.ShapeDtypeStruct((t, d), BF16),
        compiler_params=_cparams(("parallel", "arbitrary")),
        name="merge",
    )(ret_out, dn_out, w_a, w_b, proj, proj)


def _outproj_kernel(m_ref, w_ref, x_ref, mod_ref, o_ref):
    y = _dot(m_ref[...], w_ref[...])
    o_ref[...] = x_ref[...] + mod_ref[2:3, :] * y


def _outproj(merged, w_out, x, mod8, seg_of_block, tm):
    t, d = x.shape
    tn = _pick_tile(d, (512, 256, 128))
    return pl.pallas_call(
        _outproj_kernel,
        grid=(t // tm, d // tn),
        in_specs=[pl.BlockSpec((tm, d), lambda i, j: (i, 0)),
                  pl.BlockSpec((d, tn), lambda i, j: (0, j)),
                  pl.BlockSpec((tm, tn), lambda i, j: (i, j)),
                  pl.BlockSpec((None, 8, tn), lambda i, j: (seg_of_block(i), 0, j))],
        out_specs=pl.BlockSpec((tm, tn), lambda i, j: (i, j)),
        out_shape=jax.ShapeDtypeStruct((t, d), F32),
        compiler_params=_cparams(("parallel", "arbitrary")),
        name="outproj",
    )(merged, w_out, x, mod8)


def _peerq_kernel(x_ref, mod_ref, g_ref, w_ref, h_ref, q_ref, h_scr):
    @pl.when(pl.program_id(1) == 0)
    def _():
        h = _modulated_norm(x_ref[...], g_ref[...], mod_ref[3:4, :], mod_ref[4:5, :]).astype(BF16)
        h_scr[...] = h
        h_ref[...] = h

    q = _dot(h_scr[...], w_ref[...]).astype(BF16)
    for i in range(q_ref.shape[0]):
        q_ref[i] = q[:, i * 128:(i + 1) * 128]


def _peerq(x1, mod8, seg_of_block, tm, norm_g, w_q):
    t, d = x1.shape
    nq = w_q.shape[1]
    tn = _pick_tile(nq, (512, 256, 128))
    return pl.pallas_call(
        _peerq_kernel,
        grid=(t // tm, nq // tn),
        in_specs=[pl.BlockSpec((tm, d), lambda i, j: (i, 0)),
                  pl.BlockSpec((None, 8, d), lambda i, j: (seg_of_block(i), 0, 0)),
                  pl.BlockSpec((1, d), lambda i, j: (0, 0)),
                  pl.BlockSpec((d, tn), lambda i, j: (0, j))],
        out_specs=[pl.BlockSpec((tm, d), lambda i, j: (i, 0)),
                   pl.BlockSpec((tn // 128, tm, 128), lambda i, j: (j, i, 0))],
        out_shape=[jax.ShapeDtypeStruct((t, d), BF16),
                   jax.ShapeDtypeStruct((nq // 128, t, 128), BF16)],
        scratch_shapes=[pltpu.VMEM((tm, d), BF16)],
        compiler_params=_cparams(("parallel", "arbitrary")),
        name="peer_query",
    )(x1, mod8, norm_g.reshape(1, d), w_q)


def _ret_kernel(lg_ref, q_ref, k_ref, v_ref, g_ref, gn_ref, *rest, n_chunks, has_init, emit_state):
    rest = list(rest)
    s0_ref = rest.pop(0) if has_init else None
    o_ref = rest.pop(0)
    st_ref = rest.pop(0) if emit_state else None
    if n_chunks > 1:
        o_scr, sf_scr, sb_scr = rest
    c = RET_CHUNK
    scale = DK_RET ** -0.5
    h = pl.program_id(1)
    lgf = lg_ref[0, h]
    lgb = lg_ref[1, h]

    row = lax.broadcasted_iota(jnp.int32, (c, c), 0)
    col = lax.broadcasted_iota(jnp.int32, (c, c), 1)
    dlt = (row - col).astype(F32)
    dmat = (jnp.where(dlt >= 0, jnp.exp(lgf * jnp.maximum(dlt, 0.0)), 0.0)
            + jnp.where(dlt <= 0, jnp.exp(lgb * jnp.maximum(-dlt, 0.0)), 0.0)) * scale
    pos = lax.broadcasted_iota(jnp.int32, (c, DK_RET), 0).astype(F32)
    xi_f = jnp.exp(lgf * (pos + 1.0))
    xi_b = jnp.exp(lgb * (c - pos))
    zeta_f = jnp.exp(lgf * (c - 1.0 - pos)) * scale
    zeta_b = jnp.exp(lgb * pos) * scale
    dec_f = jnp.exp(jnp.full((1, DV_RET), lgf, F32) * float(c))
    dec_b = jnp.exp(jnp.full((1, DV_RET), lgb, F32) * float(c))

    def rows(i):
        return pl.ds(pl.multiple_of(i * c, c), c)

    def intra(r):
        q = q_ref[r, :]
        k = k_ref[r, :]
        v = v_ref[r, :]
        s = _nt_dot(q, k) * dmat
        return q.astype(F32), k.astype(F32), v, _dot(s.astype(BF16), v)

    def state_inc(kf, zeta, v):
        return _dot((kf * zeta).T.astype(BF16), v)

    def finish(o, r):
        mu = jnp.mean(o, axis=-1, keepdims=True)
        oc = o - mu
        var = jnp.mean(oc * oc, axis=-1, keepdims=True)
        on = oc * lax.rsqrt(var + EPS)
        o_ref[r, :] = (on * gn_ref[...] * _silu(g_ref[r, :].astype(F32))).astype(o_ref.dtype)

    if n_chunks == 1:
        r = pl.ds(0, c)
        qf, kf, v, o = intra(r)
        s_f = state_inc(kf, zeta_f, v)
        s_b = state_inc(kf, zeta_b, v)
        if has_init:
            s0f = s0_ref[0]
            s0b = s0_ref[1]
            o = o + _dot((qf * xi_f).astype(BF16), s0f.astype(BF16))
            o = o + _dot((qf * xi_b).astype(BF16), s0b.astype(BF16))
            s_f = s_f + dec_f * s0f
            s_b = s_b + dec_b * s0b
        finish(o, r)
        if emit_state:
            st_ref[0] = s_f
            st_ref[1] = s_b
        return

    if has_init:
        sf_scr[...] = s0_ref[0]
        sb_scr[...] = s0_ref[1]
    else:
        sf_scr[...] = jnp.zeros_like(sf_scr)
        sb_scr[...] = jnp.zeros_like(sb_scr)

    def fwd(i, carry):
        r = rows(i)
        qf, kf, v, o = intra(r)
        s = sf_scr[...]
        o_scr[r, :] = o + _dot((qf * xi_f).astype(BF16), s.astype(BF16))
        sf_scr[...] = dec_f * s + state_inc(kf, zeta_f, v)
        return carry

    lax.fori_loop(0, n_chunks, fwd, 0)

    def bwd(ii, carry):
        i = n_chunks - 1 - ii
        r = rows(i)
        qf = q_ref[r, :].astype(F32)
        kf = k_ref[r, :].astype(F32)
        v = v_ref[r, :]
        s = sb_scr[...]
        o = o_scr[r, :] + _dot((qf * xi_b).astype(BF16), s.astype(BF16))
        sb_scr[...] = dec_b * s + state_inc(kf, zeta_b, v)
        finish(o, r)
        return carry

    lax.fori_loop(0, n_chunks, bwd, 0)
    if emit_state:
        st_ref[0] = sf_scr[...]
        st_ref[1] = sb_scr[...]


def _retention(proj, lg, ret_gn, s0, *, n_batch, seq, row0, emit_state):
    n_chunks = seq // RET_CHUNK
    rb0 = row0 // seq
    has_init = s0 is not None
    in_specs = [pl.BlockSpec(memory_space=pltpu.SMEM),
                pl.BlockSpec((seq, DK_RET), lambda b, h: (rb0 + b, RQ0 // DK_RET + h)),
                pl.BlockSpec((seq, DK_RET), lambda b, h: (rb0 + b, RK0 // DK_RET + h)),
                pl.BlockSpec((seq, DV_RET), lambda b, h: (rb0 + b, RV0 // DV_RET + h)),
                pl.BlockSpec((seq, DV_RET), lambda b, h: (rb0 + b, RG0 // DV_RET + h)),
                pl.BlockSpec((1, DV_RET), lambda b, h: (0, h))]
    args = [lg, proj, proj, proj, proj, ret_gn.reshape(1, RET_V)]
    if has_init:
        in_specs.append(pl.BlockSpec((None, 2, None, DK_RET, DV_RET), lambda b, h: (b, 0, h, 0, 0)))
        args.append(s0)
    out_specs = [pl.BlockSpec((seq, DV_RET), lambda b, h: (b, h))]
    out_shape = [jax.ShapeDtypeStruct((n_batch * seq, RET_V), BF16)]
    if emit_state:
        out_specs.append(pl.BlockSpec((None, 2, None, DK_RET, DV_RET), lambda b, h: (b, 0, h, 0, 0)))
        out_shape.append(jax.ShapeDtypeStruct((n_batch, 2, H_RET, DK_RET, DV_RET), F32))
    scratch = []
    if n_chunks > 1:
        scratch = [pltpu.VMEM((seq, DV_RET), F32), pltpu.VMEM((DK_RET, DV_RET), F32),
                   pltpu.VMEM((DK_RET, DV_RET), F32)]
    return pl.pallas_call(
        functools.partial(_ret_kernel, n_chunks=n_chunks, has_init=has_init, emit_state=emit_state),
        grid=(n_batch, H_RET),
        in_specs=in_specs,
        out_specs=out_specs,
        out_shape=out_shape,
        scratch_shapes=scratch,
        compiler_params=_cparams(("parallel", "arbitrary")),
        name="retention_ctx" if emit_state else "retention_lat",
    )(*args)


def _dn_kernel(par_ref, q_ref, k_ref, v_ref, z_ref, gt_ref, cq_ref, ck_ref, cv_ref, nrm_ref, *rest,
               seq, row_len, heads, has_init, emit_state):
    rest = list(rest)
    s0_ref = rest.pop(0) if has_init else None
    o_ref = rest.pop(0)
    st_ref = rest.pop(0) if emit_state else None
    rows_scr, o_scr, s_scr, qkv_scr = rest
    cb = DN_CHUNK
    n_blocks = seq // cb
    hg = pl.program_id(1)

    lane = lax.broadcasted_iota(jnp.int32, (1, seq), 1) & (cb - 1)

    def splat(v):
        return jnp.full((1, seq), v, F32)

    def prefix(x):
        s = 1
        while s < cb:
            x = x + jnp.where(lane >= s, pltpu.roll(x, s, 1), 0.0)
            s *= 2
        return x

    def suffix(x):
        s = 1
        while s < cb:
            x = x + jnp.where(lane < cb - s, pltpu.roll(x, seq - s, 1), 0.0)
            s *= 2
        return x

    for g in range(heads):
        h = hg * heads + g
        r0 = 8 * g
        g_f = (-jnp.exp(splat(par_ref[0, h]))
               * jax.nn.softplus(gt_ref[r0 + 2:r0 + 3, :] + splat(par_ref[2, h])))
        g_b = (-jnp.exp(splat(par_ref[1, h]))
               * jax.nn.softplus(gt_ref[r0 + 3:r0 + 4, :] + splat(par_ref[3, h])))
        rows_scr[r0:r0 + 1, :] = jax.nn.sigmoid(gt_ref[r0:r0 + 1, :])
        rows_scr[r0 + 1:r0 + 2, :] = prefix(g_f)
        rows_scr[r0 + 2:r0 + 3, :] = suffix(g_f) - g_f
        rows_scr[r0 + 3:r0 + 4, :] = jax.nn.sigmoid(gt_ref[r0 + 1:r0 + 2, :])
        rows_scr[r0 + 4:r0 + 5, :] = suffix(g_b)
        rows_scr[r0 + 5:r0 + 6, :] = prefix(g_b) - g_b
        rows_scr[r0 + 6:r0 + 8, :] = jnp.zeros((2, seq), F32)
        for direction in range(2):
            if has_init:
                s_scr[g, direction] = s0_ref[direction, g]
            else:
                s_scr[g, direction] = jnp.zeros((DK_DN, DV_DN), F32)

    ri = lax.broadcasted_iota(jnp.int32, (cb, cb), 0)
    ci = lax.broadcasted_iota(jnp.int32, (cb, cb), 1)
    eye = (ri == ci).astype(F32)
    tpos = lax.broadcasted_iota(jnp.int32, (cb, DK_DN), 0) & (row_len - 1)

    def conv(x, w_ref, lanes):
        acc = x * w_ref[CONV_K // 2:CONV_K // 2 + 1, lanes]
        for i in range(CONV_K):
            d = i - CONV_K // 2
            if d == 0:
                continue
            sh = pltpu.roll(x, (-d) % cb, 0)
            ok = (tpos + d >= 0) & (tpos + d < row_len)
            acc = acc + jnp.where(ok, sh, 0.0) * w_ref[i:i + 1, lanes]
        return _silu(acc)

    def l2n(x):
        return x * lax.rsqrt(jnp.sum(x * x, axis=-1, keepdims=True) + EPS)

    def col_of(row):
        parts = []
        for t in range(cb // 128):
            parts.append(jnp.broadcast_to(row[:, t * 128:(t + 1) * 128], (128, 128)).T)
        return jnp.concatenate(parts, axis=0)

    def rows_of(bi):
        return pl.ds(pl.multiple_of(bi * cb, cb), cb)

    def prepare(bi, carry):
        r = rows_of(bi)
        for g in range(heads):
            lanes = slice(g * DK_DN, (g + 1) * DK_DN)
            qkv_scr[g, 0, r, :] = l2n(conv(q_ref[r, lanes].astype(F32), cq_ref, lanes)) * (DK_DN ** -0.5)
            qkv_scr[g, 1, r, :] = l2n(conv(k_ref[r, lanes].astype(F32), ck_ref, lanes))
            qkv_scr[g, 2, r, :] = conv(v_ref[r, lanes].astype(F32), cv_ref, lanes)
        return carry

    if n_blocks == 1:
        prepare(0, 0)
    else:
        lax.fori_loop(0, n_blocks, prepare, 0)

    def blocks(chains):
        cs = []
        for g, bi, direction in chains:
            r = rows_of(bi)
            base = 8 * g + (0 if direction == 0 else 3)
            cs.append(dict(g=g, d=direction, r=r, beta_row=rows_scr[base:base + 1, r],
                           g_row=rows_scr[base + 1:base + 2, r], e_row=rows_scr[base + 2:base + 3, r]))
        for c in cs:
            c["q"] = qkv_scr[c["g"], 0, c["r"], :]
            c["k"] = qkv_scr[c["g"], 1, c["r"], :]
            c["v"] = qkv_scr[c["g"], 2, c["r"], :]
            c["k16"] = c["k"].astype(BF16)
            c["kk"] = _nt_dot(c["k16"], c["k16"])
        for c in cs:
            c["beta_c"] = col_of(c["beta_row"])
            c["g_c"] = col_of(c["g_row"])
            c["incl"] = (ri >= ci) if c["d"] == 0 else (ri <= ci)
            strict = (ri > ci) if c["d"] == 0 else (ri < ci)
            g_cw = jnp.concatenate([c["g_c"]] * (cb // 128), axis=1)
            beta_cw = jnp.concatenate([c["beta_c"]] * (cb // 128), axis=1)
            c["lmat"] = jnp.exp(jnp.where(c["incl"], g_cw - c["g_row"], NEG_INF))
            c["m"] = jnp.where(strict, c["kk"] * beta_cw * c["lmat"], 0.0)
            c["x"] = eye - jnp.where((ri // 2) == (ci // 2), c["m"], 0.0)
        sz = 2
        while sz < cb:
            pair = ((ri // (2 * sz)) == (ci // (2 * sz))) & ((ri // sz) != (ci // sz))
            for c in cs:
                c["x16"] = c["x"].astype(BF16)
                c["t16"] = _dot(c["x16"], jnp.where(pair, c["m"], 0.0).astype(BF16)).astype(BF16)
            for c in cs:
                c["x"] = c["x"] - _dot(c["t16"], c["x16"])
            sz *= 2
        for c in cs:
            eg_c = jnp.exp(c["g_c"])
            rhs = jnp.concatenate([c["v"] * c["beta_c"], c["k"] * (c["beta_c"] * eg_c)], axis=1).astype(BF16)
            uw = _dot(c["x"].astype(BF16), rhs)
            c["u"] = uw[:, :DV_DN]
            c["w16"] = uw[:, DV_DN:].astype(BF16)
            c["qg16"] = (c["q"] * eg_c).astype(BF16)
            c["aqk"] = jnp.where(c["incl"], _nt_dot(c["q"].astype(BF16), c["k16"]) * c["lmat"], 0.0).astype(BF16)
            c["kdec_t"] = (c["k"] * jnp.exp(col_of(c["e_row"]))).T.astype(BF16)
        outs = []
        for c in cs:
            s = s_scr[c["g"], c["d"]]
            s16 = s.astype(BF16)
            vn16 = (c["u"] - _dot(c["w16"], s16)).astype(BF16)
            outs.append(_dot(c["qg16"], s16) + _dot(c["aqk"], vn16))
            last = cb - 1 if c["d"] == 0 else 0
            s_scr[c["g"], c["d"]] = s * jnp.exp(c["g_c"][last:last + 1, :]) + _dot(c["kdec_t"], vn16)
        return outs

    def finish(g, bi, d):
        r = rows_of(bi)
        lanes = slice(g * DV_DN, (g + 1) * DV_DN)
        d = d * lax.rsqrt(jnp.mean(d * d, axis=-1, keepdims=True) + EPS)
        o_ref[r, lanes] = (d * nrm_ref[...] * _silu(z_ref[r, lanes].astype(F32))).astype(o_ref.dtype)

    if n_blocks == 1:
        outs = blocks([(g, 0, d) for g in range(heads) for d in range(2)])
        for g in range(heads):
            finish(g, 0, outs[2 * g] + outs[2 * g + 1])
    else:
        per = 2 if n_blocks % 4 == 0 else 1

        def chains_of(it):
            return [(g, per * it + j if d == 0 else n_blocks - 1 - per * it - j, d)
                    for g in range(heads) for d in range(2) for j in range(per)]

        def first_half(it, carry):
            chains = chains_of(it)
            for (g, bi, _), out in zip(chains, blocks(chains)):
                o_scr[g, rows_of(bi), :] = out
            return carry

        def second_half(it, carry):
            chains = chains_of(it)
            for (g, bi, _), out in zip(chains, blocks(chains)):
                finish(g, bi, out + o_scr[g, rows_of(bi), :])
            return carry

        n_it = n_blocks // per
        lax.fori_loop(0, n_it // 2, first_half, 0)
        lax.fori_loop(n_it // 2, n_it, second_half, 0)
    if emit_state:
        for g in range(heads):
            st_ref[0, g] = s_scr[g, 0]
            st_ref[1, g] = s_scr[g, 1]


def _deltanet(proj, gates_t, dn_par, dn_conv, dn_norm, s0, *, n_batch, seq, row0, row_len, heads, emit_state):
    rb0 = row0 // seq
    has_init = s0 is not None
    wd = heads * DK_DN
    in_specs = [pl.BlockSpec(memory_space=pltpu.SMEM),
                pl.BlockSpec((seq, wd), lambda b, h: (rb0 + b, DQ0 // wd + h)),
                pl.BlockSpec((seq, wd), lambda b, h: (rb0 + b, DK0 // wd + h)),
                pl.BlockSpec((seq, wd), lambda b, h: (rb0 + b, DV0 // wd + h)),
                pl.BlockSpec((seq, wd), lambda b, h: (rb0 + b, DZ0 // wd + h)),
                pl.BlockSpec((8 * heads, seq), lambda b, h: (h, rb0 + b)),
                pl.BlockSpec((CONV_K, wd), lambda b, h: (0, h)),
                pl.BlockSpec((CONV_K, wd), lambda b, h: (0, DN_QK // wd + h)),
                pl.BlockSpec((CONV_K, wd), lambda b, h: (0, 2 * DN_QK // wd + h)),
                pl.BlockSpec((1, DV_DN), lambda b, h: (0, 0))]
    args = [dn_par, proj, proj, proj, proj, gates_t, dn_conv, dn_conv, dn_conv, dn_norm.reshape(1, DV_DN)]
    if has_init:
        in_specs.append(pl.BlockSpec((None, 2, heads, DK_DN, DV_DN), lambda b, h: (b, 0, h, 0, 0)))
        args.append(s0)
    out_specs = [pl.BlockSpec((seq, wd), lambda b, h: (b, h))]
    out_shape = [jax.ShapeDtypeStruct((n_batch * seq, DN_V), BF16)]
    if emit_state:
        out_specs.append(pl.BlockSpec((None, 2, heads, DK_DN, DV_DN), lambda b, h: (b, 0, h, 0, 0)))
        out_shape.append(jax.ShapeDtypeStruct((n_batch, 2, H_DN, DK_DN, DV_DN), F32))
    scratch = [pltpu.VMEM((8 * heads, seq), F32), pltpu.VMEM((heads, seq, DV_DN), F32),
               pltpu.VMEM((heads, 2, DK_DN, DV_DN), F32), pltpu.VMEM((heads, 3, seq, DK_DN), F32)]
    n_blocks = seq // DN_CHUNK
    assert seq % DN_CHUNK == 0 and (n_blocks == 1 or n_blocks % 2 == 0)
    assert DN_CHUNK % row_len == 0 and row_len & (row_len - 1) == 0 and H_DN % heads == 0
    return pl.pallas_call(
        functools.partial(_dn_kernel, seq=seq, row_len=row_len, heads=heads, has_init=has_init,
                          emit_state=emit_state),
        grid=(n_batch, H_DN // heads),
        in_specs=in_specs,
        out_specs=out_specs,
        out_shape=out_shape,
        scratch_shapes=scratch,
        compiler_params=_cparams(("parallel", "arbitrary")),
        name="deltanet_ctx" if emit_state else "deltanet_lat",
    )(*args)


def _cand_pairs():
    n = PEER_TOPK + 1
    return [(a, b) for a in range(n) for b in range(n) if (a + 1) * (b + 1) <= n]


def _peer_score_kernel(q_ref, sk_ref, pk_ref, s1_scr, sv_scr, gp_scr):
    def per_head(h, carry):
        for p in range(2):
            s = _nt_dot(sk_ref[p], q_ref[2 * h + p])
            if p == 0:
                s1_scr[h] = s
            else:
                pk_ref[0, h] = s
            cur = s
            for r in range(PEER_TOPK + 1):
                m = jnp.max(cur, axis=0, keepdims=True)
                sv_scr[p, r, pl.ds(h, 1), :] = m
                cur = jnp.where(cur == m, NEG_INF, cur)
        return carry

    lax.fori_loop(0, PEER_HEADS, per_head, 0)
    cands = [sv_scr[0, a] + sv_scr[1, b] for a, b in _cand_pairs()]
    mx = cands[0]
    z = jnp.zeros_like(mx)
    for r in range(PEER_TOPK):
        m = functools.reduce(jnp.maximum, cands)
        z = z + jnp.exp(m - mx)
        cands = [jnp.where(c == m, NEG_INF, c) for c in cands]
    runner_up = functools.reduce(jnp.maximum, cands)
    gp_scr[0] = 0.5 * (m + runner_up)
    gp_scr[1] = 1.0 / z

    def finish_head(h, carry):
        row = pl.ds(h, 1)
        s1 = s1_scr[h]
        pk_ref[1, h] = jnp.exp(pk_ref[0, h] - sv_scr[1, 0, row, :])
        pk_ref[2, h] = jnp.exp(s1 - sv_scr[0, 0, row, :]) * gp_scr[1, row, :]
        pk_ref[3, h] = gp_scr[0, row, :] - s1
        return carry

    lax.fori_loop(0, PEER_HEADS, finish_head, 0)


def _peer_scores(q3, subkeys, tb):
    nhp, t, _ = q3.shape
    return pl.pallas_call(
        _peer_score_kernel,
        grid=(t // tb,),
        in_specs=[pl.BlockSpec((nhp, tb, 128), lambda i: (0, i, 0)),
                  pl.BlockSpec((2, N_KEYS, PEER_QDIM // 2), lambda i: (0, 0, 0))],
        out_specs=pl.BlockSpec((4, PEER_HEADS, N_KEYS, tb), lambda i: (0, 0, 0, i)),
        out_shape=jax.ShapeDtypeStruct((4, PEER_HEADS, N_KEYS, t), F32),
        scratch_shapes=[pltpu.VMEM((PEER_HEADS, N_KEYS, tb), F32),
                        pltpu.VMEM((2, PEER_TOPK + 1, PEER_HEADS, tb), F32),
                        pltpu.VMEM((2, PEER_HEADS, tb), F32)],
        compiler_params=_cparams(("parallel",)),
        name="peer_scores",
    )(q3, subkeys)


def _gelu_tanh(x):
    return 0.5 * x * (1.0 + jnp.tanh(0.7978845608028654 * (x + 0.044715 * (x * x * x))))


def _peer_dense_kernel(h_ref, pk_ref, u_ref, vt_ref, o_ref, w_scr, *, n_i1):
    e = pl.program_id(1)
    n_e = pl.num_programs(1)

    def build_gates(tile, slot):
        for ii in range(n_i1):
            row = pl.ds(tile * n_i1 + ii, 1)
            acc = None
            for h in range(PEER_HEADS):
                g = jnp.where(pk_ref[0, h] >= pk_ref[3, h, row, :], pk_ref[1, h] * pk_ref[2, h, row, :], 0.0)
                acc = g if acc is None else acc + g
            w_scr[slot, ii * N_KEYS:(ii + 1) * N_KEYS, :] = acc

    @pl.when(e == 0)
    def _():
        o_ref[...] = jnp.zeros_like(o_ref)
        build_gates(0, 0)

    slot = e % 2
    build_gates(jnp.minimum(e + 1, n_e - 1), 1 - slot)
    act = _gelu_tanh(_nt_dot(u_ref[...], h_ref[...]))
    wa = (w_scr[slot] * act).astype(BF16)
    o_ref[...] += _dot(vt_ref[...], wa)


def _peer_dense(h2, pk, u_tab, v_tab_t, tb, et):
    t, d = h2.shape
    ne = u_tab.shape[0]
    once = pl.Buffered(1)
    return pl.pallas_call(
        functools.partial(_peer_dense_kernel, n_i1=et // N_KEYS),
        grid=(t // tb, ne // et),
        in_specs=[pl.BlockSpec((tb, d), lambda i, e: (i, 0), pipeline_mode=once),
                  pl.BlockSpec((4, PEER_HEADS, N_KEYS, tb), lambda i, e: (0, 0, 0, i), pipeline_mode=once),
                  pl.BlockSpec((et, d), lambda i, e: (e, 0)),
                  pl.BlockSpec((d, et), lambda i, e: (0, e))],
        out_specs=pl.BlockSpec((d, tb), lambda i, e: (0, i)),
        out_shape=jax.ShapeDtypeStruct((d, t), F32),
        scratch_shapes=[pltpu.VMEM((2, et, tb), F32)],
        compiler_params=_cparams(("parallel", "arbitrary")),
        name="peer_dense",
    )(h2, pk, u_tab, v_tab_t)


def _final_kernel(x_ref, p_ref, mod_ref, g_ref, o_ref):
    x = x_ref[...] + mod_ref[5:6, :] * p_ref[...].T
    ms = jnp.mean(x * x, axis=-1, keepdims=True)
    o_ref[...] = (x * lax.rsqrt(ms + EPS)) * g_ref[...]


def _final(x1, peer_t, mod8, seg_of_block, tm_seg, final_norm):
    t, d = x1.shape
    tm = 256
    sub = tm_seg // tm
    return pl.pallas_call(
        _final_kernel,
        grid=(t // tm,),
        in_specs=[pl.BlockSpec((tm, d), lambda i: (i, 0)),
                  pl.BlockSpec((d, tm), lambda i: (0, i)),
                  pl.BlockSpec((None, 8, d), lambda i: (seg_of_block(i // sub), 0, 0)),
                  pl.BlockSpec((1, d), lambda i: (0, 0))],
        out_specs=pl.BlockSpec((tm, d), lambda i: (i, 0)),
        out_shape=jax.ShapeDtypeStruct((t, d), F32),
        compiler_params=_cparams(("parallel",)),
        name="final_norm",
    )(x1, peer_t, mod8, final_norm.reshape(1, d))


def kernel(x_prompt, x_sample, state_ret, state_dn, c, c_ctx, w_ada, b_ada, norm_mix, norm_ffn, w_in, ret_logit,
           ret_gn, dn_conv, dn_a_log, dn_dt_bias, dn_norm, w_br_a, w_br_b, w_out, peer_wq, peer_subkeys, peer_u,
           peer_v, final_norm):
    bc, lc, d = x_prompt.shape
    bl, ll, _ = x_sample.shape
    depth = w_ada.shape[0]
    tc = bc * lc
    t = tc + bl * ll
    assert bl + 1 <= 8 and tc % ll == 0

    tm = _pick_tile(math.gcd(tc, ll), (1024, 512, 256))
    n_ctx_blocks = tc // tm
    lat_blocks = ll // tm

    def seg_of_block(i):
        return jnp.where(i < n_ctx_blocks, 0, 1 + (i - n_ctx_blocks) // lat_blocks)

    x = jnp.concatenate([x_prompt.reshape(tc, d), x_sample.reshape(bl * ll, d)], axis=0)
    cvec = jnp.concatenate([c_ctx[None, :], c, jnp.zeros((8 - 1 - bl, d), F32)], axis=0)

    ret_states, dn_states = [], []
    for l in range(depth):
        mod = _adaln(cvec, w_ada[l], b_ada[l])
        mod8 = jnp.pad(mod.reshape(8, 6, d), ((0, 0), (0, 2), (0, 0)))

        w = w_in[l]
        w_main = jnp.concatenate([w[:, :DGL0], w[:, DGL0 + 4 * H_DN:]], axis=1).astype(BF16)
        wg = w[:, DGL0:DGL0 + 4 * H_DN].reshape(d, 4, H_DN).transpose(2, 1, 0)
        w_gate_t = jnp.pad(wg, ((0, 0), (0, 4), (0, 0))).reshape(8 * H_DN, d).astype(BF16)

        proj, gates_t = _inproj(x, mod8, seg_of_block, tm, norm_mix[l], w_main, w_gate_t)

        lg = jax.nn.log_sigmoid(ret_logit[l].astype(F32))
        ret_c, rs = _retention(proj, lg, ret_gn[l], None, n_batch=bc, seq=lc, row0=0, emit_state=True)
        (ret_l,) = _retention(proj, lg, ret_gn[l], state_ret[:, l], n_batch=bl, seq=ll, row0=tc,
                              emit_state=False)
        dn_par = jnp.concatenate([dn_a_log[l], dn_dt_bias[l]], axis=0).astype(F32)
        dn_c, ds = _deltanet(proj, gates_t, dn_par, dn_conv[l], dn_norm[l], None, n_batch=bc, seq=lc, row0=0,
                             row_len=lc, heads=4, emit_state=True)
        (dn_l,) = _deltanet(proj, gates_t, dn_par, dn_conv[l], dn_norm[l], state_dn[:, l], n_batch=bl, seq=ll,
                            row0=tc, row_len=GRID_W, heads=2, emit_state=False)
        ret_states.append(rs)
        dn_states.append(ds)
        ret_out = jnp.concatenate([ret_c, ret_l], axis=0)
        dn_out = jnp.concatenate([dn_c, dn_l], axis=0)

        merged = _merge(ret_out, dn_out, w_br_a[l].astype(BF16), w_br_b[l].astype(BF16), proj, d, tm)
        x1 = _outproj(merged, w_out[l].astype(BF16), x, mod8, seg_of_block, tm)

        h2, q3 = _peerq(x1, mod8, seg_of_block, tm, norm_ffn[l], peer_wq[l].astype(BF16))
        tb = min(512, tm)
        pk = _peer_scores(q3, peer_subkeys[l].astype(BF16), tb)
        peer_t = _peer_dense(h2, pk, peer_u[l].astype(BF16), peer_v[l].T.astype(BF16), tb, 1024)
        x = x1
        x_last_peer = peer_t
        assert depth == 1

    y = _final(x, x_last_peer, mod8, seg_of_block, tm, final_norm)
    y_prompt = y[:tc].reshape(bc, lc, d)
    y_sample = y[tc:].reshape(bl, ll, d)
    new_state_ret = jnp.stack(ret_states, axis=1)
    new_state_dn = jnp.stack(dn_states, axis=1)
    return (y_prompt, y_sample, new_state_ret, new_state_dn)
```

```python
import functools
import math

import jax
import jax.numpy as jnp
from jax import lax
from jax.experimental import pallas as pl
from jax.experimental.pallas import tpu as pltpu

F32 = jnp.float32
BF16 = jnp.bfloat16

H_RET, DK_RET, DV_RET = 8, 128, 256
H_DN, DK_DN, DV_DN = 16, 128, 128
CONV_K = 5
GRID_W = 64
DN_CHUNK = 256
RET_CHUNK = 256
N_KEYS = 128
PEER_HEADS = 8
PEER_QDIM = 256
PEER_TOPK = 16
EPS = 1e-6

RET_QK = H_RET * DK_RET
RET_V = H_RET * DV_RET
DN_QK = H_DN * DK_DN
DN_V = H_DN * DV_DN
RQ0 = 0
RK0 = RQ0 + RET_QK
RV0 = RK0 + RET_QK
RG0 = RV0 + RET_V
DQ0 = RG0 + RET_V
DK0 = DQ0 + DN_QK
DV0 = DK0 + DN_QK
DZ0 = DV0 + DN_V
DGL0 = DZ0 + DN_V
BG0 = DZ0 + DN_V

VMEM_LIMIT = 56 * 1024 * 1024
NEG_INF = float("-inf")


def _cparams(sem):
    return pltpu.CompilerParams(dimension_semantics=sem, vmem_limit_bytes=VMEM_LIMIT)


def _nt_dot(a, b):
    return lax.dot_general(a, b, (((1,), (1,)), ((), ())), preferred_element_type=F32)


def _dot(a, b):
    return jnp.dot(a, b, preferred_element_type=F32)


def _silu(x):
    return x * jax.nn.sigmoid(x)


def _pick_tile(n, cands):
    for c in cands:
        if n % c == 0:
            return c
    raise ValueError(f"no tile for {n}")


def _adaln_kernel(c_ref, w_ref, b_ref, o_ref):
    a = _silu(c_ref[...])
    o_ref[...] = jnp.dot(a, w_ref[...], precision=lax.Precision.HIGHEST,
                         preferred_element_type=F32) + b_ref[...]


def _adaln(cvec, w_ada, b_ada):
    d, n = w_ada.shape
    tn = _pick_tile(n, (1024, 512, 256, 128))
    return pl.pallas_call(
        _adaln_kernel,
        grid=(n // tn,),
        in_specs=[pl.BlockSpec((8, d), lambda j: (0, 0)),
                  pl.BlockSpec((d, tn), lambda j: (0, j)),
                  pl.BlockSpec((1, tn), lambda j: (0, j))],
        out_specs=pl.BlockSpec((8, tn), lambda j: (0, j)),
        out_shape=jax.ShapeDtypeStruct((8, n), F32),
        compiler_params=_cparams(("parallel",)),
        name="adaln",
    )(cvec, w_ada, b_ada.reshape(1, n))


def _modulated_norm(x, g, shift, scale):
    ms = jnp.mean(x * x, axis=-1, keepdims=True)
    return (x * lax.rsqrt(ms + EPS)) * g * (1.0 + scale) + shift


def _inproj_kernel(x_ref, mod_ref, g_ref, w_ref, wg_ref, o_ref, gt_ref, h_scr):
    @pl.when(pl.program_id(1) == 0)
    def _():
        h = _modulated_norm(x_ref[...], g_ref[...], mod_ref[0:1, :], mod_ref[1:2, :]).astype(BF16)
        h_scr[...] = h
        gt_ref[...] = _nt_dot(wg_ref[...], h)

    o_ref[...] = _dot(h_scr[...], w_ref[...]).astype(o_ref.dtype)


def _inproj(x, mod8, seg_of_block, tm, norm_g, w_main, w_gate_t):
    t, d = x.shape
    nm = w_main.shape[1]
    tn = _pick_tile(nm, (1024, 512, 256))
    return pl.pallas_call(
        _inproj_kernel,
        grid=(t // tm, nm // tn),
        in_specs=[pl.BlockSpec((tm, d), lambda i, j: (i, 0)),
                  pl.BlockSpec((None, 8, d), lambda i, j: (seg_of_block(i), 0, 0)),
                  pl.BlockSpec((1, d), lambda i, j: (0, 0)),
                  pl.BlockSpec((d, tn), lambda i, j: (0, j)),
                  pl.BlockSpec((128, d), lambda i, j: (0, 0))],
        out_specs=[pl.BlockSpec((tm, tn), lambda i, j: (i, j)),
                   pl.BlockSpec((128, tm), lambda i, j: (0, i))],
        out_shape=[jax.ShapeDtypeStruct((t, nm), BF16),
                   jax.ShapeDtypeStruct((128, t), F32)],
        scratch_shapes=[pltpu.VMEM((tm, d), BF16)],
        compiler_params=_cparams(("parallel", "arbitrary")),
        name="inproj",
    )(x, mod8, norm_g.reshape(1, d), w_main, w_gate_t)


def _merge_kernel(a_ref, b_ref, wa_ref, wb_ref, ga_ref, gb_ref, o_ref):
    ya = _dot(a_ref[...], wa_ref[...])
    yb = _dot(b_ref[...], wb_ref[...])
    ga = jax.nn.sigmoid(ga_ref[...].astype(F32))
    gb = jax.nn.sigmoid(gb_ref[...].astype(F32))
    o_ref[...] = (ga * ya + gb * yb).astype(o_ref.dtype)


def _merge(ret_out, dn_out, w_a, w_b, proj, d, tm):
    t = ret_out.shape[0]
    tn = _pick_tile(d, (512, 256, 128))
    ga0 = BG0 // tn
    gb0 = (BG0 + d) // tn
    return pl.pallas_call(
        _merge_kernel,
        grid=(t // tm, d // tn),
        in_specs=[pl.BlockSpec((tm, RET_V), lambda i, j: (i, 0)),
                  pl.BlockSpec((tm, DN_V), lambda i, j: (i, 0)),
                  pl.BlockSpec((RET_V, tn), lambda i, j: (0, j)),
                  pl.BlockSpec((DN_V, tn), lambda i, j: (0, j)),
                  pl.BlockSpec((tm, tn), lambda i, j: (i, ga0 + j)),
                  pl.BlockSpec((tm, tn), lambda i, j: (i, gb0 + j))],
        out_specs=pl.BlockSpec((tm, tn), lambda i, j: (i, j)),
        out_shape=jax.ShapeDtypeStruct((t, d), BF16),
        compiler_params=_cparams(("parallel", "arbitrary")),
        name="merge",
    )(ret_out, dn_out, w_a, w_b, proj, proj)


def _outproj_kernel(m_ref, w_ref, x_ref, mod_ref, o_ref):
    y = _dot(m_ref[...], w_ref[...])
    o_ref[...] = x_ref[...] + mod_ref[2:3, :] * y


def _outproj(merged, w_out, x, mod8, seg_of_block, tm):
    t, d = x.shape
    tn = _pick_tile(d, (512, 256, 128))
    return pl.pallas_call(
        _outproj_kernel,
        grid=(t // tm, d // tn),
        in_specs=[pl.BlockSpec((tm, d), lambda i, j: (i, 0)),
                  pl.BlockSpec((d, tn), lambda i, j: (0, j)),
                  pl.BlockSpec((tm, tn), lambda i, j: (i, j)),
                  pl.BlockSpec((None, 8, tn), lambda i, j: (seg_of_block(i), 0, j))],
        out_specs=pl.BlockSpec((tm, tn), lambda i, j: (i, j)),
        out_shape=jax.ShapeDtypeStruct((t, d), F32),
        compiler_params=_cparams(("parallel", "arbitrary")),
        name="outproj",
    )(merged, w_out, x, mod8)


def _peerq_kernel(x_ref, mod_ref, g_ref, w_ref, h_ref, q_ref, h_scr):
    @pl.when(pl.program_id(1) == 0)
    def _():
        h = _modulated_norm(x_ref[...], g_ref[...], mod_ref[3:4, :], mod_ref[4:5, :]).astype(BF16)
        h_scr[...] = h
        h_ref[...] = h

    q = _dot(h_scr[...], w_ref[...]).astype(BF16)
    for i in range(q_ref.shape[0]):
        q_ref[i] = q[:, i * 128:(i + 1) * 128]


def _peerq(x1, mod8, seg_of_block, tm, norm_g, w_q):
    t, d = x1.shape
    nq = w_q.shape[1]
    tn = _pick_tile(nq, (512, 256, 128))
    return pl.pallas_call(
        _peerq_kernel,
        grid=(t // tm, nq // tn),
        in_specs=[pl.BlockSpec((tm, d), lambda i, j: (i, 0)),
                  pl.BlockSpec((None, 8, d), lambda i, j: (seg_of_block(i), 0, 0)),
                  pl.BlockSpec((1, d), lambda i, j: (0, 0)),
                  pl.BlockSpec((d, tn), lambda i, j: (0, j))],
        out_specs=[pl.BlockSpec((tm, d), lambda i, j: (i, 0)),
                   pl.BlockSpec((tn // 128, tm, 128), lambda i, j: (j, i, 0))],
        out_shape=[jax.ShapeDtypeStruct((t, d), BF16),
                   jax.ShapeDtypeStruct((nq // 128, t, 128), BF16)],
        scratch_shapes=[pltpu.VMEM((tm, d), BF16)],
        compiler_params=_cparams(("parallel", "arbitrary")),
        name="peer_query",
    )(x1, mod8, norm_g.reshape(1, d), w_q)


def _ret_kernel(lg_ref, q_ref, k_ref, v_ref, g_ref, gn_ref, *rest, n_chunks, has_init, emit_state):
    rest = list(rest)
    s0_ref = rest.pop(0) if has_init else None
    o_ref = rest.pop(0)
    st_ref = rest.pop(0) if emit_state else None
    if n_chunks > 1:
        o_scr, sf_scr, sb_scr = rest
    c = RET_CHUNK
    scale = DK_RET ** -0.5
    h = pl.program_id(1)
    lgf = lg_ref[0, h]
    lgb = lg_ref[1, h]

    row = lax.broadcasted_iota(jnp.int32, (c, c), 0)
    col = lax.broadcasted_iota(jnp.int32, (c, c), 1)
    dlt = (row - col).astype(F32)
    dmat = (jnp.where(dlt >= 0, jnp.exp(lgf * jnp.maximum(dlt, 0.0)), 0.0)
            + jnp.where(dlt <= 0, jnp.exp(lgb * jnp.maximum(-dlt, 0.0)), 0.0)) * scale
    pos = lax.broadcasted_iota(jnp.int32, (c, DK_RET), 0).astype(F32)
    xi_f = jnp.exp(lgf * (pos + 1.0))
    xi_b = jnp.exp(lgb * (c - pos))
    zeta_f = jnp.exp(lgf * (c - 1.0 - pos)) * scale
    zeta_b = jnp.exp(lgb * pos) * scale
    dec_f = jnp.exp(jnp.full((1, DV_RET), lgf, F32) * float(c))
    dec_b = jnp.exp(jnp.full((1, DV_RET), lgb, F32) * float(c))

    def rows(i):
        return pl.ds(pl.multiple_of(i * c, c), c)

    def intra(r):
        q = q_ref[r, :]
        k = k_ref[r, :]
        v = v_ref[r, :]
        s = _nt_dot(q, k) * dmat
        return q.astype(F32), k.astype(F32), v, _dot(s.astype(BF16), v)

    def state_inc(kf, zeta, v):
        return _dot((kf * zeta).T.astype(BF16), v)

    def finish(o, r):
        mu = jnp.mean(o, axis=-1, keepdims=True)
        oc = o - mu
        var = jnp.mean(oc * oc, axis=-1, keepdims=True)
        on = oc * lax.rsqrt(var + EPS)
        o_ref[r, :] = (on * gn_ref[...] * _silu(g_ref[r, :].astype(F32))).astype(o_ref.dtype)

    if n_chunks == 1:
        r = pl.ds(0, c)
        qf, kf, v, o = intra(r)
        s_f = state_inc(kf, zeta_f, v)
        s_b = state_inc(kf, zeta_b, v)
        if has_init:
            s0f = s0_ref[0]
            s0b = s0_ref[1]
            o = o + _dot((qf * xi_f).astype(BF16), s0f.astype(BF16))
            o = o + _dot((qf * xi_b).astype(BF16), s0b.astype(BF16))
            s_f = s_f + dec_f * s0f
            s_b = s_b + dec_b * s0b
        finish(o, r)
        if emit_state:
            st_ref[0] = s_f
            st_ref[1] = s_b
        return

    if has_init:
        sf_scr[...] = s0_ref[0]
        sb_scr[...] = s0_ref[1]
    else:
        sf_scr[...] = jnp.zeros_like(sf_scr)
        sb_scr[...] = jnp.zeros_like(sb_scr)

    def fwd(i, carry):
        r = rows(i)
        qf, kf, v, o = intra(r)
        s = sf_scr[...]
        o_scr[r, :] = o + _dot((qf * xi_f).astype(BF16), s.astype(BF16))
        sf_scr[...] = dec_f * s + state_inc(kf, zeta_f, v)
        return carry

    lax.fori_loop(0, n_chunks, fwd, 0)

    def bwd(ii, carry):
        i = n_chunks - 1 - ii
        r = rows(i)
        qf = q_ref[r, :].astype(F32)
        kf = k_ref[r, :].astype(F32)
        v = v_ref[r, :]
        s = sb_scr[...]
        o = o_scr[r, :] + _dot((qf * xi_b).astype(BF16), s.astype(BF16))
        sb_scr[...] = dec_b * s + state_inc(kf, zeta_b, v)
        finish(o, r)
        return carry

    lax.fori_loop(0, n_chunks, bwd, 0)
    if emit_state:
        st_ref[0] = sf_scr[...]
        st_ref[1] = sb_scr[...]


def _retention(proj, lg, ret_gn, s0, *, n_batch, seq, row0, emit_state):
    n_chunks = seq // RET_CHUNK
    rb0 = row0 // seq
    has_init = s0 is not None
    in_specs = [pl.BlockSpec(memory_space=pltpu.SMEM),
                pl.BlockSpec((seq, DK_RET), lambda b, h: (rb0 + b, RQ0 // DK_RET + h)),
                pl.BlockSpec((seq, DK_RET), lambda b, h: (rb0 + b, RK0 // DK_RET + h)),
                pl.BlockSpec((seq, DV_RET), lambda b, h: (rb0 + b, RV0 // DV_RET + h)),
                pl.BlockSpec((seq, DV_RET), lambda b, h: (rb0 + b, RG0 // DV_RET + h)),
                pl.BlockSpec((1, DV_RET), lambda b, h: (0, h))]
    args = [lg, proj, proj, proj, proj, ret_gn.reshape(1, RET_V)]
    if has_init:
        in_specs.append(pl.BlockSpec((None, 2, None, DK_RET, DV_RET), lambda b, h: (b, 0, h, 0, 0)))
        args.append(s0)
    out_specs = [pl.BlockSpec((seq, DV_RET), lambda b, h: (b, h))]
    out_shape = [jax.ShapeDtypeStruct((n_batch * seq, RET_V), BF16)]
    if emit_state:
        out_specs.append(pl.BlockSpec((None, 2, None, DK_RET, DV_RET), lambda b, h: (b, 0, h, 0, 0)))
        out_shape.append(jax.ShapeDtypeStruct((n_batch, 2, H_RET, DK_RET, DV_RET), F32))
    scratch = []
    if n_chunks > 1:
        scratch = [pltpu.VMEM((seq, DV_RET), F32), pltpu.VMEM((DK_RET, DV_RET), F32),
                   pltpu.VMEM((DK_RET, DV_RET), F32)]
    return pl.pallas_call(
        functools.partial(_ret_kernel, n_chunks=n_chunks, has_init=has_init, emit_state=emit_state),
        grid=(n_batch, H_RET),
        in_specs=in_specs,
        out_specs=out_specs,
        out_shape=out_shape,
        scratch_shapes=scratch,
        compiler_params=_cparams(("parallel", "arbitrary")),
        name="retention_ctx" if emit_state else "retention_lat",
    )(*args)


def _dn_kernel(par_ref, q_ref, k_ref, v_ref, z_ref, gt_ref, cq_ref, ck_ref, cv_ref, nrm_ref, *rest,
               seq, row_len, heads, has_init, emit_state):
    rest = list(rest)
    s0_ref = rest.pop(0) if has_init else None
    o_ref = rest.pop(0)
    st_ref = rest.pop(0) if emit_state else None
    rows_scr, o_scr, s_scr, qkv_scr = rest
    cb = DN_CHUNK
    n_blocks = seq // cb
    hg = pl.program_id(1)

    lane = lax.broadcasted_iota(jnp.int32, (1, seq), 1) & (cb - 1)

    def splat(v):
        return jnp.full((1, seq), v, F32)

    def prefix(x):
        s = 1
        while s < cb:
            x = x + jnp.where(lane >= s, pltpu.roll(x, s, 1), 0.0)
            s *= 2
        return x

    def suffix(x):
        s = 1
        while s < cb:
            x = x + jnp.where(lane < cb - s, pltpu.roll(x, seq - s, 1), 0.0)
            s *= 2
        return x

    for g in range(heads):
        h = hg * heads + g
        r0 = 8 * g
        g_f = (-jnp.exp(splat(par_ref[0, h]))
               * jax.nn.softplus(gt_ref[r0 + 2:r0 + 3, :] + splat(par_ref[2, h])))
        g_b = (-jnp.exp(splat(par_ref[1, h]))
               * jax.nn.softplus(gt_ref[r0 + 3:r0 + 4, :] + splat(par_ref[3, h])))
        rows_scr[r0:r0 + 1, :] = jax.nn.sigmoid(gt_ref[r0:r0 + 1, :])
        rows_scr[r0 + 1:r0 + 2, :] = prefix(g_f)
        rows_scr[r0 + 2:r0 + 3, :] = suffix(g_f) - g_f
        rows_scr[r0 + 3:r0 + 4, :] = jax.nn.sigmoid(gt_ref[r0 + 1:r0 + 2, :])
        rows_scr[r0 + 4:r0 + 5, :] = suffix(g_b)
        rows_scr[r0 + 5:r0 + 6, :] = prefix(g_b) - g_b
        rows_scr[r0 + 6:r0 + 8, :] = jnp.zeros((2, seq), F32)
        for direction in range(2):
            if has_init:
                s_scr[g, direction] = s0_ref[direction, g]
            else:
                s_scr[g, direction] = jnp.zeros((DK_DN, DV_DN), F32)

    ri = lax.broadcasted_iota(jnp.int32, (cb, cb), 0)
    ci = lax.broadcasted_iota(jnp.int32, (cb, cb), 1)
    eye = (ri == ci).astype(F32)
    tpos = lax.broadcasted_iota(jnp.int32, (cb, DK_DN), 0) & (row_len - 1)

    def conv(x, w_ref, lanes):
        acc = x * w_ref[CONV_K // 2:CONV_K // 2 + 1, lanes]
        for i in range(CONV_K):
            d = i - CONV_K // 2
            if d == 0:
                continue
            sh = pltpu.roll(x, (-d) % cb, 0)
            ok = (tpos + d >= 0) & (tpos + d < row_len)
            acc = acc + jnp.where(ok, sh, 0.0) * w_ref[i:i + 1, lanes]
        return _silu(acc)

    def l2n(x):
        return x * lax.rsqrt(jnp.sum(x * x, axis=-1, keepdims=True) + EPS)

    def col_of(row):
        parts = []
        for t in range(cb // 128):
            parts.append(jnp.broadcast_to(row[:, t * 128:(t + 1) * 128], (128, 128)).T)
        return jnp.concatenate(parts, axis=0)

    def rows_of(bi):
        return pl.ds(pl.multiple_of(bi * cb, cb), cb)

    def prepare(bi, carry):
        r = rows_of(bi)
        for g in range(heads):
            lanes = slice(g * DK_DN, (g + 1) * DK_DN)
            qkv_scr[g, 0, r, :] = l2n(conv(q_ref[r, lanes].astype(F32), cq_ref, lanes)) * (DK_DN ** -0.5)
            qkv_scr[g, 1, r, :] = l2n(conv(k_ref[r, lanes].astype(F32), ck_ref, lanes))
            qkv_scr[g, 2, r, :] = conv(v_ref[r, lanes].astype(F32), cv_ref, lanes)
        return carry

    if n_blocks == 1:
        prepare(0, 0)
    else:
        lax.fori_loop(0, n_blocks, prepare, 0)

    def blocks(chains):
        cs = []
        for g, bi, direction in chains:
            r = rows_of(bi)
            base = 8 * g + (0 if direction == 0 else 3)
            cs.append(dict(g=g, d=direction, r=r, beta_row=rows_scr[base:base + 1, r],
                           g_row=rows_scr[base + 1:base + 2, r], e_row=rows_scr[base + 2:base + 3, r]))
        for c in cs:
            c["q"] = qkv_scr[c["g"], 0, c["r"], :]
            c["k"] = qkv_scr[c["g"], 1, c["r"], :]
            c["v"] = qkv_scr[c["g"], 2, c["r"], :]
            c["k16"] = c["k"].astype(BF16)
            c["kk"] = _nt_dot(c["k16"], c["k16"])
        for c in cs:
            c["beta_c"] = col_of(c["beta_row"])
            c["g_c"] = col_of(c["g_row"])
            c["incl"] = (ri >= ci) if c["d"] == 0 else (ri <= ci)
            strict = (ri > ci) if c["d"] == 0 else (ri < ci)
            g_cw = jnp.concatenate([c["g_c"]] * (cb // 128), axis=1)
            beta_cw = jnp.concatenate([c["beta_c"]] * (cb // 128), axis=1)
            c["lmat"] = jnp.exp(jnp.where(c["incl"], g_cw - c["g_row"], NEG_INF))
            c["m"] = jnp.where(strict, c["kk"] * beta_cw * c["lmat"], 0.0)
            c["x"] = eye - jnp.where((ri // 2) == (ci // 2), c["m"], 0.0)
        sz = 2
        while sz < cb:
            pair = ((ri // (2 * sz)) == (ci // (2 * sz))) & ((ri // sz) != (ci // sz))
            for c in cs:
                c["x16"] = c["x"].astype(BF16)
                c["t16"] = _dot(c["x16"], jnp.where(pair, c["m"], 0.0).astype(BF16)).astype(BF16)
            for c in cs:
                c["x"] = c["x"] - _dot(c["t16"], c["x16"])
            sz *= 2
        for c in cs:
            eg_c = jnp.exp(c["g_c"])
            rhs = jnp.concatenate([c["v"] * c["beta_c"], c["k"] * (c["beta_c"] * eg_c)], axis=1).astype(BF16)
            uw = _dot(c["x"].astype(BF16), rhs)
            c["u"] = uw[:, :DV_DN]
            c["w16"] = uw[:, DV_DN:].astype(BF16)
            c["qg16"] = (c["q"] * eg_c).astype(BF16)
            c["aqk"] = jnp.where(c["incl"], _nt_dot(c["q"].astype(BF16), c["k16"]) * c["lmat"], 0.0).astype(BF16)
            c["kdec_t"] = (c["k"] * jnp.exp(col_of(c["e_row"]))).T.astype(BF16)
        outs = []
        for c in cs:
            s = s_scr[c["g"], c["d"]]
            s16 = s.astype(BF16)
            vn16 = (c["u"] - _dot(c["w16"], s16)).astype(BF16)
            outs.append(_dot(c["qg16"], s16) + _dot(c["aqk"], vn16))
            last = cb - 1 if c["d"] == 0 else 0
            s_scr[c["g"], c["d"]] = s * jnp.exp(c["g_c"][last:last + 1, :]) + _dot(c["kdec_t"], vn16)
        return outs

    def finish(g, bi, d):
        r = rows_of(bi)
        lanes = slice(g * DV_DN, (g + 1) * DV_DN)
        d = d * lax.rsqrt(jnp.mean(d * d, axis=-1, keepdims=True) + EPS)
        o_ref[r, lanes] = (d * nrm_ref[...] * _silu(z_ref[r, lanes].astype(F32))).astype(o_ref.dtype)

    if n_blocks == 1:
        outs = blocks([(g, 0, d) for g in range(heads) for d in range(2)])
        for g in range(heads):
            finish(g, 0, outs[2 * g] + outs[2 * g + 1])
    else:
        per = 2 if n_blocks % 4 == 0 else 1

        def chains_of(it):
            return [(g, per * it + j if d == 0 else n_blocks - 1 - per * it - j, d)
                    for g in range(heads) for d in range(2) for j in range(per)]

        def first_half(it, carry):
            chains = chains_of(it)
            for (g, bi, _), out in zip(chains, blocks(chains)):
                o_scr[g, rows_of(bi), :] = out
            return carry

        def second_half(it, carry):
            chains = chains_of(it)
            for (g, bi, _), out in zip(chains, blocks(chains)):
                finish(g, bi, out + o_scr[g, rows_of(bi), :])
            return carry

        n_it = n_blocks // per
        lax.fori_loop(0, n_it // 2, first_half, 0)
        lax.fori_loop(n_it // 2, n_it, second_half, 0)
    if emit_state:
        for g in range(heads):
            st_ref[0, g] = s_scr[g, 0]
            st_ref[1, g] = s_scr[g, 1]


def _deltanet(proj, gates_t, dn_par, dn_conv, dn_norm, s0, *, n_batch, seq, row0, row_len, heads, emit_state):
    rb0 = row0 // seq
    has_init = s0 is not None
    wd = heads * DK_DN
    in_specs = [pl.BlockSpec(memory_space=pltpu.SMEM),
                pl.BlockSpec((seq, wd), lambda b, h: (rb0 + b, DQ0 // wd + h)),
                pl.BlockSpec((seq, wd), lambda b, h: (rb0 + b, DK0 // wd + h)),
                pl.BlockSpec((seq, wd), lambda b, h: (rb0 + b, DV0 // wd + h)),
                pl.BlockSpec((seq, wd), lambda b, h: (rb0 + b, DZ0 // wd + h)),
                pl.BlockSpec((8 * heads, seq), lambda b, h: (h, rb0 + b)),
                pl.BlockSpec((CONV_K, wd), lambda b, h: (0, h)),
                pl.BlockSpec((CONV_K, wd), lambda b, h: (0, DN_QK // wd + h)),
                pl.BlockSpec((CONV_K, wd), lambda b, h: (0, 2 * DN_QK // wd + h)),
                pl.BlockSpec((1, DV_DN), lambda b, h: (0, 0))]
    args = [dn_par, proj, proj, proj, proj, gates_t, dn_conv, dn_conv, dn_conv, dn_norm.reshape(1, DV_DN)]
    if has_init:
        in_specs.append(pl.BlockSpec((None, 2, heads, DK_DN, DV_DN), lambda b, h: (b, 0, h, 0, 0)))
        args.append(s0)
    out_specs = [pl.BlockSpec((seq, wd), lambda b, h: (b, h))]
    out_shape = [jax.ShapeDtypeStruct((n_batch * seq, DN_V), BF16)]
    if emit_state:
        out_specs.append(pl.BlockSpec((None, 2, heads, DK_DN, DV_DN), lambda b, h: (b, 0, h, 0, 0)))
        out_shape.append(jax.ShapeDtypeStruct((n_batch, 2, H_DN, DK_DN, DV_DN), F32))
    scratch = [pltpu.VMEM((8 * heads, seq), F32), pltpu.VMEM((heads, seq, DV_DN), F32),
               pltpu.VMEM((heads, 2, DK_DN, DV_DN), F32), pltpu.VMEM((heads, 3, seq, DK_DN), F32)]
    n_blocks = seq // DN_CHUNK
    assert seq % DN_CHUNK == 0 and (n_blocks == 1 or n_blocks % 2 == 0)
    assert DN_CHUNK % row_len == 0 and row_len & (row_len - 1) == 0 and H_DN % heads == 0
    return pl.pallas_call(
        functools.partial(_dn_kernel, seq=seq, row_len=row_len, heads=heads, has_init=has_init,
                          emit_state=emit_state),
        grid=(n_batch, H_DN // heads),
        in_specs=in_specs,
        out_specs=out_specs,
        out_shape=out_shape,
        scratch_shapes=scratch,
        compiler_params=_cparams(("parallel", "arbitrary")),
        name="deltanet_ctx" if emit_state else "deltanet_lat",
    )(*args)


def _cand_pairs():
    n = PEER_TOPK + 1
    return [(a, b) for a in range(n) for b in range(n) if (a + 1) * (b + 1) <= n]


def _peer_score_kernel(q_ref, sk_ref, pb_ref, pf_ref, s_scr, sv_scr, gp_scr):
    big_rank = float(N_KEYS)

    def per_head(h, carry):
        for p in range(2):
            s = _nt_dot(sk_ref[p], q_ref[2 * h + p])
            s_scr[p, h] = s
            cur = s
            rank = jnp.full(s.shape, big_rank, F32)
            for r in range(PEER_TOPK + 1):
                m = jnp.max(cur, axis=0, keepdims=True)
                sv_scr[p, r, pl.ds(h, 1), :] = m
                if p == 1:
                    rank = jnp.where(cur == m, float(r), rank)
                cur = jnp.where(cur == m, NEG_INF, cur)
            if p == 1:
                pb_ref[0, h] = rank.astype(BF16)
        return carry

    lax.fori_loop(0, PEER_HEADS, per_head, 0)
    cands = [sv_scr[0, a] + sv_scr[1, b] for a, b in _cand_pairs()]
    mx = cands[0]
    z = jnp.zeros_like(mx)
    for r in range(PEER_TOPK):
        m = functools.reduce(jnp.maximum, cands)
        z = z + jnp.exp(m - mx)
        cands = [jnp.where(c == m, NEG_INF, c) for c in cands]
    runner_up = functools.reduce(jnp.maximum, cands)
    gp_scr[0] = 0.5 * (m + runner_up)
    gp_scr[1] = 1.0 / z

    def finish_head(h, carry):
        row = pl.ds(h, 1)
        s1 = s_scr[0, h]
        pb_ref[1, h] = jnp.exp(s_scr[1, h] - sv_scr[1, 0, row, :]).astype(BF16)
        pf_ref[0, h] = jnp.exp(s1 - sv_scr[0, 0, row, :]) * gp_scr[1, row, :]
        need = gp_scr[0, row, :] - s1
        cnt = jnp.zeros_like(need)
        for r in range(PEER_TOPK + 1):
            cnt = cnt + jnp.where(sv_scr[1, r, row, :] >= need, 1.0, 0.0)
        pf_ref[1, h] = cnt
        return carry

    lax.fori_loop(0, PEER_HEADS, finish_head, 0)


def _peer_scores(q3, subkeys, tb):
    nhp, t, _ = q3.shape
    return pl.pallas_call(
        _peer_score_kernel,
        grid=(t // tb,),
        in_specs=[pl.BlockSpec((nhp, tb, 128), lambda i: (0, i, 0)),
                  pl.BlockSpec((2, N_KEYS, PEER_QDIM // 2), lambda i: (0, 0, 0))],
        out_specs=[pl.BlockSpec((2, PEER_HEADS, N_KEYS, tb), lambda i: (0, 0, 0, i)),
                   pl.BlockSpec((2, PEER_HEADS, N_KEYS, tb), lambda i: (0, 0, 0, i))],
        out_shape=[jax.ShapeDtypeStruct((2, PEER_HEADS, N_KEYS, t), BF16),
                   jax.ShapeDtypeStruct((2, PEER_HEADS, N_KEYS, t), F32)],
        scratch_shapes=[pltpu.VMEM((2, PEER_HEADS, N_KEYS, tb), F32),
                        pltpu.VMEM((2, PEER_TOPK + 1, PEER_HEADS, tb), F32),
                        pltpu.VMEM((2, PEER_HEADS, tb), F32)],
        compiler_params=_cparams(("parallel",)),
        name="peer_scores",
    )(q3, subkeys)


def _gelu_tanh(x):
    return 0.5 * x * (1.0 + jnp.tanh(0.7978845608028654 * (x + 0.044715 * (x * x * x))))


def _peer_dense_kernel(h_ref, pb_ref, pf_ref, u_ref, vt_ref, o_ref, w_scr, *, n_i1):
    e = pl.program_id(1)
    n_e = pl.num_programs(1)
    zero = jnp.zeros((), BF16)

    def build_gates(tile, slot):
        for ii in range(n_i1):
            row = pl.ds(tile * n_i1 + ii, 1)
            acc = None
            for h in range(PEER_HEADS):
                e1 = pf_ref[0, h, row, :].astype(BF16)
                cnt = pf_ref[1, h, row, :].astype(BF16)
                g = jnp.where(pb_ref[0, h] < cnt, pb_ref[1, h] * e1, zero)
                acc = g if acc is None else acc + g
            w_scr[slot, ii * N_KEYS:(ii + 1) * N_KEYS, :] = acc

    @pl.when(e == 0)
    def _():
        o_ref[...] = jnp.zeros_like(o_ref)
        build_gates(0, 0)

    slot = e % 2
    build_gates(jnp.minimum(e + 1, n_e - 1), 1 - slot)
    act = _gelu_tanh(_nt_dot(u_ref[...], h_ref[...]))
    wa = w_scr[slot] * act.astype(BF16)
    o_ref[...] += _dot(vt_ref[...], wa)


def _peer_dense(h2, pb, pf, u_tab, v_tab_t, tb, et):
    t, d = h2.shape
    ne = u_tab.shape[0]
    once = pl.Buffered(1)
    return pl.pallas_call(
        functools.partial(_peer_dense_kernel, n_i1=et // N_KEYS),
        grid=(t // tb, ne // et),
        in_specs=[pl.BlockSpec((tb, d), lambda i, e: (i, 0), pipeline_mode=once),
                  pl.BlockSpec((2, PEER_HEADS, N_KEYS, tb), lambda i, e: (0, 0, 0, i), pipeline_mode=once),
                  pl.BlockSpec((2, PEER_HEADS, N_KEYS, tb), lambda i, e: (0, 0, 0, i), pipeline_mode=once),
                  pl.BlockSpec((et, d), lambda i, e: (e, 0)),
                  pl.BlockSpec((d, et), lambda i, e: (0, e))],
        out_specs=pl.BlockSpec((d, tb), lambda i, e: (0, i)),
        out_shape=jax.ShapeDtypeStruct((d, t), F32),
        scratch_shapes=[pltpu.VMEM((2, et, tb), BF16)],
        compiler_params=_cparams(("parallel", "arbitrary")),
        name="peer_dense",
    )(h2, pb, pf, u_tab, v_tab_t)


def _final_kernel(x_ref, p_ref, mod_ref, g_ref, o_ref):
    x = x_ref[...] + mod_ref[5:6, :] * p_ref[...].T
    ms = jnp.mean(x * x, axis=-1, keepdims=True)
    o_ref[...] = (x * lax.rsqrt(ms + EPS)) * g_ref[...]


def _final(x1, peer_t, mod8, seg_of_block, tm_seg, final_norm):
    t, d = x1.shape
    tm = 256
    sub = tm_seg // tm
    return pl.pallas_call(
        _final_kernel,
        grid=(t // tm,),
        in_specs=[pl.BlockSpec((tm, d), lambda i: (i, 0)),
                  pl.BlockSpec((d, tm), lambda i: (0, i)),
                  pl.BlockSpec((None, 8, d), lambda i: (seg_of_block(i // sub), 0, 0)),
                  pl.BlockSpec((1, d), lambda i: (0, 0))],
        out_specs=pl.BlockSpec((tm, d), lambda i: (i, 0)),
        out_shape=jax.ShapeDtypeStruct((t, d), F32),
        compiler_params=_cparams(("parallel",)),
        name="final_norm",
    )(x1, peer_t, mod8, final_norm.reshape(1, d))


def kernel(x_prompt, x_sample, state_ret, state_dn, c, c_ctx, w_ada, b_ada, norm_mix, norm_ffn, w_in, ret_logit,
           ret_gn, dn_conv, dn_a_log, dn_dt_bias, dn_norm, w_br_a, w_br_b, w_out, peer_wq, peer_subkeys, peer_u,
           peer_v, final_norm):
    bc, lc, d = x_prompt.shape
    bl, ll, _ = x_sample.shape
    depth = w_ada.shape[0]
    tc = bc * lc
    t = tc + bl * ll
    assert bl + 1 <= 8 and tc % ll == 0

    tm = _pick_tile(math.gcd(tc, ll), (1024, 512, 256))
    n_ctx_blocks = tc // tm
    lat_blocks = ll // tm

    def seg_of_block(i):
        return jnp.where(i < n_ctx_blocks, 0, 1 + (i - n_ctx_blocks) // lat_blocks)

    x = jnp.concatenate([x_prompt.reshape(tc, d), x_sample.reshape(bl * ll, d)], axis=0)
    cvec = jnp.concatenate([c_ctx[None, :], c, jnp.zeros((8 - 1 - bl, d), F32)], axis=0)

    ret_states, dn_states = [], []
    for l in range(depth):
        mod = _adaln(cvec, w_ada[l], b_ada[l])
        mod8 = jnp.pad(mod.reshape(8, 6, d), ((0, 0), (0, 2), (0, 0)))

        w = w_in[l]
        w_main = jnp.concatenate([w[:, :DGL0], w[:, DGL0 + 4 * H_DN:]], axis=1).astype(BF16)
        wg = w[:, DGL0:DGL0 + 4 * H_DN].reshape(d, 4, H_DN).transpose(2, 1, 0)
        w_gate_t = jnp.pad(wg, ((0, 0), (0, 4), (0, 0))).reshape(8 * H_DN, d).astype(BF16)

        proj, gates_t = _inproj(x, mod8, seg_of_block, tm, norm_mix[l], w_main, w_gate_t)

        lg = jax.nn.log_sigmoid(ret_logit[l].astype(F32))
        ret_c, rs = _retention(proj, lg, ret_gn[l], None, n_batch=bc, seq=lc, row0=0, emit_state=True)
        (ret_l,) = _retention(proj, lg, ret_gn[l], state_ret[:, l], n_batch=bl, seq=ll, row0=tc,
                              emit_state=False)
        dn_par = jnp.concatenate([dn_a_log[l], dn_dt_bias[l]], axis=0).astype(F32)
        dn_c, ds = _deltanet(proj, gates_t, dn_par, dn_conv[l], dn_norm[l], None, n_batch=bc, seq=lc, row0=0,
                             row_len=lc, heads=4, emit_state=True)
        (dn_l,) = _deltanet(proj, gates_t, dn_par, dn_conv[l], dn_norm[l], state_dn[:, l], n_batch=bl, seq=ll,
                            row0=tc, row_len=GRID_W, heads=2, emit_state=False)
        ret_states.append(rs)
        dn_states.append(ds)
        ret_out = jnp.concatenate([ret_c, ret_l], axis=0)
        dn_out = jnp.concatenate([dn_c, dn_l], axis=0)

        merged = _merge(ret_out, dn_out, w_br_a[l].astype(BF16), w_br_b[l].astype(BF16), proj, d, tm)
        x1 = _outproj(merged, w_out[l].astype(BF16), x, mod8, seg_of_block, tm)

        h2, q3 = _peerq(x1, mod8, seg_of_block, tm, norm_ffn[l], peer_wq[l].astype(BF16))
        tb = min(512, tm)
        pb, pf = _peer_scores(q3, peer_subkeys[l].astype(BF16), tb)
        peer_t = _peer_dense(h2, pb, pf, peer_u[l].astype(BF16), peer_v[l].T.astype(BF16), tb, 1024)
        x = x1
        x_last_peer = peer_t
        assert depth == 1

    y = _final(x, x_last_peer, mod8, seg_of_block, tm, final_norm)
    y_prompt = y[:tc].reshape(bc, lc, d)
    y_sample = y[tc:].reshape(bl, ll, d)
    new_state_ret = jnp.stack(ret_states, axis=1)
    new_state_dn = jnp.stack(dn_states, axis=1)
    return (y_prompt, y_sample, new_state_ret, new_state_dn)
```

```python
import functools
import math

import jax
import jax.numpy as jnp
from jax import lax
from jax.experimental import pallas as pl
from jax.experimental.pallas import tpu as pltpu

F32 = jnp.float32
BF16 = jnp.bfloat16

H_RET, DK_RET, DV_RET = 8, 128, 256
H_DN, DK_DN, DV_DN = 16, 128, 128
CONV_K = 5
GRID_W = 64
DN_CHUNK = 256
RET_CHUNK = 256
N_KEYS = 128
PEER_HEADS = 8
PEER_QDIM = 256
PEER_TOPK = 16
EPS = 1e-6

RET_QK = H_RET * DK_RET
RET_V = H_RET * DV_RET
DN_QK = H_DN * DK_DN
DN_V = H_DN * DV_DN
RQ0 = 0
RK0 = RQ0 + RET_QK
RV0 = RK0 + RET_QK
RG0 = RV0 + RET_V
DQ0 = RG0 + RET_V
DK0 = DQ0 + DN_QK
DV0 = DK0 + DN_QK
DZ0 = DV0 + DN_V
DGL0 = DZ0 + DN_V
BG0 = DZ0 + DN_V

VMEM_LIMIT = 56 * 1024 * 1024
NEG_INF = float("-inf")


def _cparams(sem):
    return pltpu.CompilerParams(dimension_semantics=sem, vmem_limit_bytes=VMEM_LIMIT)


def _nt_dot(a, b):
    return lax.dot_general(a, b, (((1,), (1,)), ((), ())), preferred_element_type=F32)


def _dot(a, b):
    return jnp.dot(a, b, preferred_element_type=F32)


def _silu(x):
    return x * jax.nn.sigmoid(x)


def _pick_tile(n, cands):
    for c in cands:
        if n % c == 0:
            return c
    raise ValueError(f"no tile for {n}")


def _adaln_kernel(c_ref, w_ref, b_ref, o_ref):
    a = _silu(c_ref[...])
    o_ref[...] = jnp.dot(a, w_ref[...], precision=lax.Precision.HIGHEST,
                         preferred_element_type=F32) + b_ref[...]


def _adaln(cvec, w_ada, b_ada):
    d, n = w_ada.shape
    tn = _pick_tile(n, (1024, 512, 256, 128))
    return pl.pallas_call(
        _adaln_kernel,
        grid=(n // tn,),
        in_specs=[pl.BlockSpec((8, d), lambda j: (0, 0)),
                  pl.BlockSpec((d, tn), lambda j: (0, j)),
                  pl.BlockSpec((1, tn), lambda j: (0, j))],
        out_specs=pl.BlockSpec((8, tn), lambda j: (0, j)),
        out_shape=jax.ShapeDtypeStruct((8, n), F32),
        compiler_params=_cparams(("parallel",)),
        name="adaln",
    )(cvec, w_ada, b_ada.reshape(1, n))


def _modulated_norm(x, g, shift, scale):
    ms = jnp.mean(x * x, axis=-1, keepdims=True)
    return (x * lax.rsqrt(ms + EPS)) * g * (1.0 + scale) + shift


def _inproj_kernel(x_ref, mod_ref, g_ref, w_ref, wg_ref, o_ref, gt_ref, h_scr):
    @pl.when(pl.program_id(1) == 0)
    def _():
        h = _modulated_norm(x_ref[...], g_ref[...], mod_ref[0:1, :], mod_ref[1:2, :]).astype(BF16)
        h_scr[...] = h
        gt_ref[...] = _nt_dot(wg_ref[...], h)

    o_ref[...] = _dot(h_scr[...], w_ref[...]).astype(o_ref.dtype)


def _inproj(x, mod8, seg_of_block, tm, norm_g, w_main, w_gate_t):
    t, d = x.shape
    nm = w_main.shape[1]
    tn = _pick_tile(nm, (1024, 512, 256))
    return pl.pallas_call(
        _inproj_kernel,
        grid=(t // tm, nm // tn),
        in_specs=[pl.BlockSpec((tm, d), lambda i, j: (i, 0)),
                  pl.BlockSpec((None, 8, d), lambda i, j: (seg_of_block(i), 0, 0)),
                  pl.BlockSpec((1, d), lambda i, j: (0, 0)),
                  pl.BlockSpec((d, tn), lambda i, j: (0, j)),
                  pl.BlockSpec((128, d), lambda i, j: (0, 0))],
        out_specs=[pl.BlockSpec((tm, tn), lambda i, j: (i, j)),
                   pl.BlockSpec((128, tm), lambda i, j: (0, i))],
        out_shape=[jax.ShapeDtypeStruct((t, nm), BF16),
                   jax.ShapeDtypeStruct((128, t), F32)],
        scratch_shapes=[pltpu.VMEM((tm, d), BF16)],
        compiler_params=_cparams(("parallel", "arbitrary")),
        name="inproj",
    )(x, mod8, norm_g.reshape(1, d), w_main, w_gate_t)


def _merge_kernel(a_ref, b_ref, wa_ref, wb_ref, ga_ref, gb_ref, o_ref):
    ya = _dot(a_ref[...], wa_ref[...])
    yb = _dot(b_ref[...], wb_ref[...])
    ga = jax.nn.sigmoid(ga_ref[...].astype(F32))
    gb = jax.nn.sigmoid(gb_ref[...].astype(F32))
    o_ref[...] = (ga * ya + gb * yb).astype(o_ref.dtype)


def _merge(ret_out, dn_out, w_a, w_b, proj, d, tm):
    t = ret_out.shape[0]
    tn = _pick_tile(d, (512, 256, 128))
    ga0 = BG0 // tn
    gb0 = (BG0 + d) // tn
    return pl.pallas_call(
        _merge_kernel,
        grid=(t // tm, d // tn),
        in_specs=[pl.BlockSpec((tm, RET_V), lambda i, j: (i, 0)),
                  pl.BlockSpec((tm, DN_V), lambda i, j: (i, 0)),
                  pl.BlockSpec((RET_V, tn), lambda i, j: (0, j)),
                  pl.BlockSpec((DN_V, tn), lambda i, j: (0, j)),
                  pl.BlockSpec((tm, tn), lambda i, j: (i, ga0 + j)),
                  pl.BlockSpec((tm, tn), lambda i, j: (i, gb0 + j))],
        out_specs=pl.BlockSpec((tm, tn), lambda i, j: (i, j)),
        out_shape=jax.ShapeDtypeStruct((t, d), BF16),
        compiler_params=_cparams(("parallel", "arbitrary")),
        name="merge",
    )(ret_out, dn_out, w_a, w_b, proj, proj)


def _outproj_kernel(m_ref, w_ref, x_ref, mod_ref, o_ref):
    y = _dot(m_ref[...], w_ref[...])
    o_ref[...] = x_ref[...] + mod_ref[2:3, :] * y


def _outproj(merged, w_out, x, mod8, seg_of_block, tm):
    t, d = x.shape
    tn = _pick_tile(d, (512, 256, 128))
    return pl.pallas_call(
        _outproj_kernel,
        grid=(t // tm, d // tn),
        in_specs=[pl.BlockSpec((tm, d), lambda i, j: (i, 0)),
                  pl.BlockSpec((d, tn), lambda i, j: (0, j)),
                  pl.BlockSpec((tm, tn), lambda i, j: (i, j)),
                  pl.BlockSpec((None, 8, tn), lambda i, j: (seg_of_block(i), 0, j))],
        out_specs=pl.BlockSpec((tm, tn), lambda i, j: (i, j)),
        out_shape=jax.ShapeDtypeStruct((t, d), F32),
        compiler_params=_cparams(("parallel", "arbitrary")),
        name="outproj",
    )(merged, w_out, x, mod8)


def _peerq_kernel(x_ref, mod_ref, g_ref, w_ref, h_ref, q_ref, h_scr):
    @pl.when(pl.program_id(1) == 0)
    def _():
        h = _modulated_norm(x_ref[...], g_ref[...], mod_ref[3:4, :], mod_ref[4:5, :]).astype(BF16)
        h_scr[...] = h
        h_ref[...] = h

    q = _dot(h_scr[...], w_ref[...]).astype(BF16)
    for i in range(q_ref.shape[0]):
        q_ref[i] = q[:, i * 128:(i + 1) * 128]


def _peerq(x1, mod8, seg_of_block, tm, norm_g, w_q):
    t, d = x1.shape
    nq = w_q.shape[1]
    tn = _pick_tile(nq, (512, 256, 128))
    return pl.pallas_call(
        _peerq_kernel,
        grid=(t // tm, nq // tn),
        in_specs=[pl.BlockSpec((tm, d), lambda i, j: (i, 0)),
                  pl.BlockSpec((None, 8, d), lambda i, j: (seg_of_block(i), 0, 0)),
                  pl.BlockSpec((1, d), lambda i, j: (0, 0)),
                  pl.BlockSpec((d, tn), lambda i, j: (0, j))],
        out_specs=[pl.BlockSpec((tm, d), lambda i, j: (i, 0)),
                   pl.BlockSpec((tn // 128, tm, 128), lambda i, j: (j, i, 0))],
        out_shape=[jax.ShapeDtypeStruct((t, d), BF16),
                   jax.ShapeDtypeStruct((nq // 128, t, 128), BF16)],
        scratch_shapes=[pltpu.VMEM((tm, d), BF16)],
        compiler_params=_cparams(("parallel", "arbitrary")),
        name="peer_query",
    )(x1, mod8, norm_g.reshape(1, d), w_q)


def _ret_kernel(lg_ref, q_ref, k_ref, v_ref, g_ref, gn_ref, *rest, n_chunks, has_init, has_dst, emit_state):
    rest = list(rest)
    s0_ref = rest.pop(0) if has_init else None
    if has_dst:
        rest.pop(0)
    o_ref = rest.pop(0)
    st_ref = rest.pop(0) if emit_state else None
    if n_chunks > 1:
        o_scr, sf_scr, sb_scr = rest
    c = RET_CHUNK
    scale = DK_RET ** -0.5
    h = pl.program_id(1)
    lgf = lg_ref[0, h]
    lgb = lg_ref[1, h]

    row = lax.broadcasted_iota(jnp.int32, (c, c), 0)
    col = lax.broadcasted_iota(jnp.int32, (c, c), 1)
    dlt = (row - col).astype(F32)
    dmat = (jnp.where(dlt >= 0, jnp.exp(lgf * jnp.maximum(dlt, 0.0)), 0.0)
            + jnp.where(dlt <= 0, jnp.exp(lgb * jnp.maximum(-dlt, 0.0)), 0.0)) * scale
    pos = lax.broadcasted_iota(jnp.int32, (c, DK_RET), 0).astype(F32)
    xi_f = jnp.exp(lgf * (pos + 1.0))
    xi_b = jnp.exp(lgb * (c - pos))
    zeta_f = jnp.exp(lgf * (c - 1.0 - pos)) * scale
    zeta_b = jnp.exp(lgb * pos) * scale
    dec_f = jnp.exp(jnp.full((1, DV_RET), lgf, F32) * float(c))
    dec_b = jnp.exp(jnp.full((1, DV_RET), lgb, F32) * float(c))

    def rows(i):
        return pl.ds(pl.multiple_of(i * c, c), c)

    def intra(r):
        q = q_ref[r, :]
        k = k_ref[r, :]
        v = v_ref[r, :]
        s = _nt_dot(q, k) * dmat
        return q.astype(F32), k.astype(F32), v, _dot(s.astype(BF16), v)

    def state_inc(kf, zeta, v):
        return _dot((kf * zeta).T.astype(BF16), v)

    def finish(o, r):
        mu = jnp.mean(o, axis=-1, keepdims=True)
        oc = o - mu
        var = jnp.mean(oc * oc, axis=-1, keepdims=True)
        on = oc * lax.rsqrt(var + EPS)
        o_ref[r, :] = (on * gn_ref[...] * _silu(g_ref[r, :].astype(F32))).astype(o_ref.dtype)

    if n_chunks == 1:
        r = pl.ds(0, c)
        qf, kf, v, o = intra(r)
        s_f = state_inc(kf, zeta_f, v)
        s_b = state_inc(kf, zeta_b, v)
        if has_init:
            s0f = s0_ref[0]
            s0b = s0_ref[1]
            o = o + _dot((qf * xi_f).astype(BF16), s0f.astype(BF16))
            o = o + _dot((qf * xi_b).astype(BF16), s0b.astype(BF16))
            s_f = s_f + dec_f * s0f
            s_b = s_b + dec_b * s0b
        finish(o, r)
        if emit_state:
            st_ref[0] = s_f
            st_ref[1] = s_b
        return

    if has_init:
        sf_scr[...] = s0_ref[0]
        sb_scr[...] = s0_ref[1]
    else:
        sf_scr[...] = jnp.zeros_like(sf_scr)
        sb_scr[...] = jnp.zeros_like(sb_scr)

    def fwd(i, carry):
        r = rows(i)
        qf, kf, v, o = intra(r)
        s = sf_scr[...]
        o_scr[r, :] = o + _dot((qf * xi_f).astype(BF16), s.astype(BF16))
        sf_scr[...] = dec_f * s + state_inc(kf, zeta_f, v)
        return carry

    lax.fori_loop(0, n_chunks, fwd, 0)

    def bwd(ii, carry):
        i = n_chunks - 1 - ii
        r = rows(i)
        qf = q_ref[r, :].astype(F32)
        kf = k_ref[r, :].astype(F32)
        v = v_ref[r, :]
        s = sb_scr[...]
        o = o_scr[r, :] + _dot((qf * xi_b).astype(BF16), s.astype(BF16))
        sb_scr[...] = dec_b * s + state_inc(kf, zeta_b, v)
        finish(o, r)
        return carry

    lax.fori_loop(0, n_chunks, bwd, 0)
    if emit_state:
        st_ref[0] = sf_scr[...]
        st_ref[1] = sb_scr[...]


def _retention(proj, lg, ret_gn, s0, dst, *, n_batch, seq, row0, emit_state):
    n_chunks = seq // RET_CHUNK
    rb0 = row0 // seq
    has_init = s0 is not None
    aliases = {}
    in_specs = [pl.BlockSpec(memory_space=pltpu.SMEM),
                pl.BlockSpec((seq, DK_RET), lambda b, h: (rb0 + b, RQ0 // DK_RET + h)),
                pl.BlockSpec((seq, DK_RET), lambda b, h: (rb0 + b, RK0 // DK_RET + h)),
                pl.BlockSpec((seq, DV_RET), lambda b, h: (rb0 + b, RV0 // DV_RET + h)),
                pl.BlockSpec((seq, DV_RET), lambda b, h: (rb0 + b, RG0 // DV_RET + h)),
                pl.BlockSpec((1, DV_RET), lambda b, h: (0, h))]
    args = [lg, proj, proj, proj, proj, ret_gn.reshape(1, RET_V)]
    if has_init:
        in_specs.append(pl.BlockSpec((None, 2, None, DK_RET, DV_RET), lambda b, h: (b, 0, h, 0, 0)))
        args.append(s0)
    if dst is not None:
        in_specs.append(pl.BlockSpec(memory_space=pl.ANY))
        aliases[len(args)] = 0
        args.append(dst)
    out_specs = [pl.BlockSpec((seq, DV_RET), lambda b, h: (rb0 + b, h))]
    out_shape = [jax.ShapeDtypeStruct((proj.shape[0], RET_V), BF16)]
    if emit_state:
        out_specs.append(pl.BlockSpec((None, 2, None, DK_RET, DV_RET), lambda b, h: (b, 0, h, 0, 0)))
        out_shape.append(jax.ShapeDtypeStruct((n_batch, 2, H_RET, DK_RET, DV_RET), F32))
    scratch = []
    if n_chunks > 1:
        scratch = [pltpu.VMEM((seq, DV_RET), F32), pltpu.VMEM((DK_RET, DV_RET), F32),
                   pltpu.VMEM((DK_RET, DV_RET), F32)]
    return pl.pallas_call(
        functools.partial(_ret_kernel, n_chunks=n_chunks, has_init=has_init, has_dst=dst is not None,
                          emit_state=emit_state),
        grid=(n_batch, H_RET),
        in_specs=in_specs,
        out_specs=out_specs,
        out_shape=out_shape,
        input_output_aliases=aliases,
        scratch_shapes=scratch,
        compiler_params=_cparams(("parallel", "arbitrary")),
        name="retention_ctx" if emit_state else "retention_lat",
    )(*args)


def _dn_kernel(par_ref, q_ref, k_ref, v_ref, z_ref, gt_ref, cq_ref, ck_ref, cv_ref, nrm_ref, *rest,
               seq, row_len, heads, has_init, has_dst, emit_state):
    rest = list(rest)
    s0_ref = rest.pop(0) if has_init else None
    if has_dst:
        rest.pop(0)
    o_ref = rest.pop(0)
    st_ref = rest.pop(0) if emit_state else None
    rows_scr, o_scr, s_scr, qkv_scr = rest
    cb = DN_CHUNK
    n_blocks = seq // cb
    hg = pl.program_id(1)

    lane = lax.broadcasted_iota(jnp.int32, (1, seq), 1) & (cb - 1)

    def splat(v):
        return jnp.full((1, seq), v, F32)

    def prefix(x):
        s = 1
        while s < cb:
            x = x + jnp.where(lane >= s, pltpu.roll(x, s, 1), 0.0)
            s *= 2
        return x

    def suffix(x):
        s = 1
        while s < cb:
            x = x + jnp.where(lane < cb - s, pltpu.roll(x, seq - s, 1), 0.0)
            s *= 2
        return x

    for g in range(heads):
        h = hg * heads + g
        r0 = 8 * g
        g_f = (-jnp.exp(splat(par_ref[0, h]))
               * jax.nn.softplus(gt_ref[r0 + 2:r0 + 3, :] + splat(par_ref[2, h])))
        g_b = (-jnp.exp(splat(par_ref[1, h]))
               * jax.nn.softplus(gt_ref[r0 + 3:r0 + 4, :] + splat(par_ref[3, h])))
        rows_scr[r0:r0 + 1, :] = jax.nn.sigmoid(gt_ref[r0:r0 + 1, :])
        rows_scr[r0 + 1:r0 + 2, :] = prefix(g_f)
        rows_scr[r0 + 2:r0 + 3, :] = suffix(g_f) - g_f
        rows_scr[r0 + 3:r0 + 4, :] = jax.nn.sigmoid(gt_ref[r0 + 1:r0 + 2, :])
        rows_scr[r0 + 4:r0 + 5, :] = suffix(g_b)
        rows_scr[r0 + 5:r0 + 6, :] = prefix(g_b) - g_b
        rows_scr[r0 + 6:r0 + 8, :] = jnp.zeros((2, seq), F32)
        for direction in range(2):
            if has_init:
                s_scr[g, direction] = s0_ref[direction, g]
            else:
                s_scr[g, direction] = jnp.zeros((DK_DN, DV_DN), F32)

    ri = lax.broadcasted_iota(jnp.int32, (cb, cb), 0)
    ci = lax.broadcasted_iota(jnp.int32, (cb, cb), 1)
    eye = (ri == ci).astype(F32)
    tpos = lax.broadcasted_iota(jnp.int32, (cb, DK_DN), 0) & (row_len - 1)

    def conv(x, w_ref, lanes):
        acc = x * w_ref[CONV_K // 2:CONV_K // 2 + 1, lanes]
        for i in range(CONV_K):
            d = i - CONV_K // 2
            if d == 0:
                continue
            sh = pltpu.roll(x, (-d) % cb, 0)
            ok = (tpos + d >= 0) & (tpos + d < row_len)
            acc = acc + jnp.where(ok, sh, 0.0) * w_ref[i:i + 1, lanes]
        return _silu(acc)

    def l2n(x):
        return x * lax.rsqrt(jnp.sum(x * x, axis=-1, keepdims=True) + EPS)

    def col_of(row):
        parts = []
        for t in range(cb // 128):
            parts.append(jnp.broadcast_to(row[:, t * 128:(t + 1) * 128], (128, 128)).T)
        return jnp.concatenate(parts, axis=0)

    def rows_of(bi):
        return pl.ds(pl.multiple_of(bi * cb, cb), cb)

    def prepare(bi, carry):
        r = rows_of(bi)
        for g in range(heads):
            lanes = slice(g * DK_DN, (g + 1) * DK_DN)
            qkv_scr[g, 0, r, :] = l2n(conv(q_ref[r, lanes].astype(F32), cq_ref, lanes)) * (DK_DN ** -0.5)
            qkv_scr[g, 1, r, :] = l2n(conv(k_ref[r, lanes].astype(F32), ck_ref, lanes))
            qkv_scr[g, 2, r, :] = conv(v_ref[r, lanes].astype(F32), cv_ref, lanes)
        return carry

    if n_blocks == 1:
        prepare(0, 0)
    else:
        lax.fori_loop(0, n_blocks, prepare, 0)

    def blocks(chains):
        cs = []
        for g, bi, direction in chains:
            r = rows_of(bi)
            base = 8 * g + (0 if direction == 0 else 3)
            cs.append(dict(g=g, d=direction, r=r, beta_row=rows_scr[base:base + 1, r],
                           g_row=rows_scr[base + 1:base + 2, r], e_row=rows_scr[base + 2:base + 3, r]))
        for c in cs:
            c["q"] = qkv_scr[c["g"], 0, c["r"], :]
            c["k"] = qkv_scr[c["g"], 1, c["r"], :]
            c["v"] = qkv_scr[c["g"], 2, c["r"], :]
            c["k16"] = c["k"].astype(BF16)
            c["kk"] = _nt_dot(c["k16"], c["k16"])
        for c in cs:
            c["beta_c"] = col_of(c["beta_row"])
            c["g_c"] = col_of(c["g_row"])
            c["incl"] = (ri >= ci) if c["d"] == 0 else (ri <= ci)
            strict = (ri > ci) if c["d"] == 0 else (ri < ci)
            g_cw = jnp.concatenate([c["g_c"]] * (cb // 128), axis=1)
            beta_cw = jnp.concatenate([c["beta_c"]] * (cb // 128), axis=1)
            c["lmat"] = jnp.exp(jnp.where(c["incl"], g_cw - c["g_row"], NEG_INF))
            c["m"] = jnp.where(strict, c["kk"] * beta_cw * c["lmat"], 0.0)
            c["x"] = eye - jnp.where((ri // 2) == (ci // 2), c["m"], 0.0)
        sz = 2
        while sz < cb:
            pair = ((ri // (2 * sz)) == (ci // (2 * sz))) & ((ri // sz) != (ci // sz))
            for c in cs:
                c["x16"] = c["x"].astype(BF16)
                c["t16"] = _dot(c["x16"], jnp.where(pair, c["m"], 0.0).astype(BF16)).astype(BF16)
            for c in cs:
                c["x"] = c["x"] - _dot(c["t16"], c["x16"])
            sz *= 2
        for c in cs:
            eg_c = jnp.exp(c["g_c"])
            rhs = jnp.concatenate([c["v"] * c["beta_c"], c["k"] * (c["beta_c"] * eg_c)], axis=1).astype(BF16)
            uw = _dot(c["x"].astype(BF16), rhs)
            c["u"] = uw[:, :DV_DN]
            c["w16"] = uw[:, DV_DN:].astype(BF16)
            c["qg16"] = (c["q"] * eg_c).astype(BF16)
            c["aqk"] = jnp.where(c["incl"], _nt_dot(c["q"].astype(BF16), c["k16"]) * c["lmat"], 0.0).astype(BF16)
            c["kdec_t"] = (c["k"] * jnp.exp(col_of(c["e_row"]))).T.astype(BF16)
        outs = []
        for c in cs:
            s = s_scr[c["g"], c["d"]]
            s16 = s.astype(BF16)
            vn16 = (c["u"] - _dot(c["w16"], s16)).astype(BF16)
            outs.append(_dot(c["qg16"], s16) + _dot(c["aqk"], vn16))
            last = cb - 1 if c["d"] == 0 else 0
            s_scr[c["g"], c["d"]] = s * jnp.exp(c["g_c"][last:last + 1, :]) + _dot(c["kdec_t"], vn16)
        return outs

    def finish(g, bi, d):
        r = rows_of(bi)
        lanes = slice(g * DV_DN, (g + 1) * DV_DN)
        d = d * lax.rsqrt(jnp.mean(d * d, axis=-1, keepdims=True) + EPS)
        o_ref[r, lanes] = (d * nrm_ref[...] * _silu(z_ref[r, lanes].astype(F32))).astype(o_ref.dtype)

    if n_blocks == 1:
        outs = blocks([(g, 0, d) for g in range(heads) for d in range(2)])
        for g in range(heads):
            finish(g, 0, outs[2 * g] + outs[2 * g + 1])
    else:
        per = 2 if n_blocks % 4 == 0 else 1

        def chains_of(it):
            return [(g, per * it + j if d == 0 else n_blocks - 1 - per * it - j, d)
                    for g in range(heads) for d in range(2) for j in range(per)]

        def first_half(it, carry):
            chains = chains_of(it)
            for (g, bi, _), out in zip(chains, blocks(chains)):
                o_scr[g, rows_of(bi), :] = out
            return carry

        def second_half(it, carry):
            chains = chains_of(it)
            for (g, bi, _), out in zip(chains, blocks(chains)):
                finish(g, bi, out + o_scr[g, rows_of(bi), :])
            return carry

        n_it = n_blocks // per
        lax.fori_loop(0, n_it // 2, first_half, 0)
        lax.fori_loop(n_it // 2, n_it, second_half, 0)
    if emit_state:
        for g in range(heads):
            st_ref[0, g] = s_scr[g, 0]
            st_ref[1, g] = s_scr[g, 1]


def _deltanet(proj, gates_t, dn_par, dn_conv, dn_norm, s0, dst, *, n_batch, seq, row0, row_len, heads,
              emit_state):
    rb0 = row0 // seq
    has_init = s0 is not None
    aliases = {}
    wd = heads * DK_DN
    in_specs = [pl.BlockSpec(memory_space=pltpu.SMEM),
                pl.BlockSpec((seq, wd), lambda b, h: (rb0 + b, DQ0 // wd + h)),
                pl.BlockSpec((seq, wd), lambda b, h: (rb0 + b, DK0 // wd + h)),
                pl.BlockSpec((seq, wd), lambda b, h: (rb0 + b, DV0 // wd + h)),
                pl.BlockSpec((seq, wd), lambda b, h: (rb0 + b, DZ0 // wd + h)),
                pl.BlockSpec((8 * heads, seq), lambda b, h: (h, rb0 + b)),
                pl.BlockSpec((CONV_K, wd), lambda b, h: (0, h)),
                pl.BlockSpec((CONV_K, wd), lambda b, h: (0, DN_QK // wd + h)),
                pl.BlockSpec((CONV_K, wd), lambda b, h: (0, 2 * DN_QK // wd + h)),
                pl.BlockSpec((1, DV_DN), lambda b, h: (0, 0))]
    args = [dn_par, proj, proj, proj, proj, gates_t, dn_conv, dn_conv, dn_conv, dn_norm.reshape(1, DV_DN)]
    if has_init:
        in_specs.append(pl.BlockSpec((None, 2, heads, DK_DN, DV_DN), lambda b, h: (b, 0, h, 0, 0)))
        args.append(s0)
    if dst is not None:
        in_specs.append(pl.BlockSpec(memory_space=pl.ANY))
        aliases[len(args)] = 0
        args.append(dst)
    out_specs = [pl.BlockSpec((seq, wd), lambda b, h: (rb0 + b, h))]
    out_shape = [jax.ShapeDtypeStruct((proj.shape[0], DN_V), BF16)]
    if emit_state:
        out_specs.append(pl.BlockSpec((None, 2, heads, DK_DN, DV_DN), lambda b, h: (b, 0, h, 0, 0)))
        out_shape.append(jax.ShapeDtypeStruct((n_batch, 2, H_DN, DK_DN, DV_DN), F32))
    scratch = [pltpu.VMEM((8 * heads, seq), F32), pltpu.VMEM((heads, seq, DV_DN), F32),
               pltpu.VMEM((heads, 2, DK_DN, DV_DN), F32), pltpu.VMEM((heads, 3, seq, DK_DN), F32)]
    n_blocks = seq // DN_CHUNK
    assert seq % DN_CHUNK == 0 and (n_blocks == 1 or n_blocks % 2 == 0)
    assert DN_CHUNK % row_len == 0 and row_len & (row_len - 1) == 0 and H_DN % heads == 0
    return pl.pallas_call(
        functools.partial(_dn_kernel, seq=seq, row_len=row_len, heads=heads, has_init=has_init,
                          has_dst=dst is not None, emit_state=emit_state),
        grid=(n_batch, H_DN // heads),
        in_specs=in_specs,
        out_specs=out_specs,
        out_shape=out_shape,
        input_output_aliases=aliases,
        scratch_shapes=scratch,
        compiler_params=_cparams(("parallel", "arbitrary")),
        name="deltanet_ctx" if emit_state else "deltanet_lat",
    )(*args)


def _cand_pairs():
    n = PEER_TOPK + 1
    return [(a, b) for a in range(n) for b in range(n) if (a + 1) * (b + 1) <= n]


def _peer_score_kernel(q_ref, sk_ref, pb_ref, pf_ref, s_scr, sv_scr, gp_scr):
    big_rank = float(N_KEYS)

    def per_head(h, carry):
        for p in range(2):
            s = _nt_dot(sk_ref[p], q_ref[2 * h + p])
            s_scr[p, h] = s
            cur = s
            rank = jnp.full(s.shape, big_rank, F32)
            for r in range(PEER_TOPK + 1):
                m = jnp.max(cur, axis=0, keepdims=True)
                sv_scr[p, r, pl.ds(h, 1), :] = m
                if p == 1:
                    rank = jnp.where(cur == m, float(r), rank)
                cur = jnp.where(cur == m, NEG_INF, cur)
            if p == 1:
                pb_ref[0, h] = rank.astype(BF16)
        return carry

    lax.fori_loop(0, PEER_HEADS, per_head, 0)
    cands = [sv_scr[0, a] + sv_scr[1, b] for a, b in _cand_pairs()]
    mx = cands[0]
    z = jnp.zeros_like(mx)
    for r in range(PEER_TOPK):
        m = functools.reduce(jnp.maximum, cands)
        z = z + jnp.exp(m - mx)
        cands = [jnp.where(c == m, NEG_INF, c) for c in cands]
    runner_up = functools.reduce(jnp.maximum, cands)
    gp_scr[0] = 0.5 * (m + runner_up)
    gp_scr[1] = 1.0 / z

    def finish_head(h, carry):
        row = pl.ds(h, 1)
        s1 = s_scr[0, h]
        pb_ref[1, h] = jnp.exp(s_scr[1, h] - sv_scr[1, 0, row, :]).astype(BF16)
        pf_ref[0, h] = jnp.exp(s1 - sv_scr[0, 0, row, :]) * gp_scr[1, row, :]
        need = gp_scr[0, row, :] - s1
        cnt = jnp.zeros_like(need)
        for r in range(PEER_TOPK + 1):
            cnt = cnt + jnp.where(sv_scr[1, r, row, :] >= need, 1.0, 0.0)
        pf_ref[1, h] = cnt
        return carry

    lax.fori_loop(0, PEER_HEADS, finish_head, 0)


def _peer_scores(q3, subkeys, tb):
    nhp, t, _ = q3.shape
    return pl.pallas_call(
        _peer_score_kernel,
        grid=(t // tb,),
        in_specs=[pl.BlockSpec((nhp, tb, 128), lambda i: (0, i, 0)),
                  pl.BlockSpec((2, N_KEYS, PEER_QDIM // 2), lambda i: (0, 0, 0))],
        out_specs=[pl.BlockSpec((2, PEER_HEADS, N_KEYS, tb), lambda i: (0, 0, 0, i)),
                   pl.BlockSpec((2, PEER_HEADS, N_KEYS, tb), lambda i: (0, 0, 0, i))],
        out_shape=[jax.ShapeDtypeStruct((2, PEER_HEADS, N_KEYS, t), BF16),
                   jax.ShapeDtypeStruct((2, PEER_HEADS, N_KEYS, t), F32)],
        scratch_shapes=[pltpu.VMEM((2, PEER_HEADS, N_KEYS, tb), F32),
                        pltpu.VMEM((2, PEER_TOPK + 1, PEER_HEADS, tb), F32),
                        pltpu.VMEM((2, PEER_HEADS, tb), F32)],
        compiler_params=_cparams(("parallel",)),
        name="peer_scores",
    )(q3, subkeys)


def _gelu_tanh(x):
    return 0.5 * x * (1.0 + jnp.tanh(0.7978845608028654 * (x + 0.044715 * (x * x * x))))


def _peer_dense_kernel(h_ref, pb_ref, pf_ref, u_ref, vt_ref, o_ref, w_scr, *, n_i1):
    e = pl.program_id(1)
    zero = jnp.zeros((), BF16)

    @pl.when(e == 0)
    def _():
        o_ref[...] = jnp.zeros_like(o_ref)

    for ii in range(n_i1):
        row = pl.ds(e * n_i1 + ii, 1)
        acc = None
        for h in range(PEER_HEADS):
            e1 = pf_ref[0, h, row, :].astype(BF16)
            cnt = pf_ref[1, h, row, :].astype(BF16)
            g = jnp.where(pb_ref[0, h] < cnt, pb_ref[1, h] * e1, zero)
            acc = g if acc is None else acc + g
        w_scr[ii * N_KEYS:(ii + 1) * N_KEYS, :] = acc
    act = _gelu_tanh(_nt_dot(u_ref[...], h_ref[...]))
    wa = w_scr[...] * act.astype(BF16)
    o_ref[...] += _dot(vt_ref[...], wa)


def _peer_dense(h2, pb, pf, u_tab, v_tab_t, tb, et):
    t, d = h2.shape
    ne = u_tab.shape[0]
    once = pl.Buffered(1)
    return pl.pallas_call(
        functools.partial(_peer_dense_kernel, n_i1=et // N_KEYS),
        grid=(t // tb, ne // et),
        in_specs=[pl.BlockSpec((tb, d), lambda i, e: (i, 0), pipeline_mode=once),
                  pl.BlockSpec((2, PEER_HEADS, N_KEYS, tb), lambda i, e: (0, 0, 0, i), pipeline_mode=once),
                  pl.BlockSpec((2, PEER_HEADS, N_KEYS, tb), lambda i, e: (0, 0, 0, i), pipeline_mode=once),
                  pl.BlockSpec((et, d), lambda i, e: (e, 0)),
                  pl.BlockSpec((d, et), lambda i, e: (0, e))],
        scratch_shapes=[pltpu.VMEM((et, tb), BF16)],
        out_specs=pl.BlockSpec((d, tb), lambda i, e: (0, i)),
        out_shape=jax.ShapeDtypeStruct((d, t), F32),
        compiler_params=_cparams(("parallel", "arbitrary")),
        name="peer_dense",
    )(h2, pb, pf, u_tab, v_tab_t)


def _final_kernel(x_ref, p_ref, mod_ref, g_ref, oc_ref, ol_ref, *, n_ctx_blocks):
    x = x_ref[...] + mod_ref[5:6, :] * p_ref[...].T
    ms = jnp.mean(x * x, axis=-1, keepdims=True)
    y = (x * lax.rsqrt(ms + EPS)) * g_ref[...]
    i = pl.program_id(0)

    @pl.when(i < n_ctx_blocks)
    def _():
        oc_ref[...] = y

    @pl.when(i >= n_ctx_blocks)
    def _():
        ol_ref[...] = y


def _final(x1, peer_t, mod8, seg_of_block, tm_seg, final_norm, tc):
    t, d = x1.shape
    tm = 256
    sub = tm_seg // tm
    nc = tc // tm
    return pl.pallas_call(
        functools.partial(_final_kernel, n_ctx_blocks=nc),
        grid=(t // tm,),
        in_specs=[pl.BlockSpec((tm, d), lambda i: (i, 0)),
                  pl.BlockSpec((d, tm), lambda i: (0, i)),
                  pl.BlockSpec((None, 8, d), lambda i: (seg_of_block(i // sub), 0, 0)),
                  pl.BlockSpec((1, d), lambda i: (0, 0))],
        out_specs=[pl.BlockSpec((tm, d), lambda i: (jnp.minimum(i, nc - 1), 0)),
                   pl.BlockSpec((tm, d), lambda i: (jnp.maximum(i - nc, 0), 0))],
        out_shape=[jax.ShapeDtypeStruct((tc, d), F32), jax.ShapeDtypeStruct((t - tc, d), F32)],
        compiler_params=_cparams(("arbitrary",)),
        name="final_norm",
    )(x1, peer_t, mod8, final_norm.reshape(1, d))


def kernel(x_prompt, x_sample, state_ret, state_dn, c, c_ctx, w_ada, b_ada, norm_mix, norm_ffn, w_in, ret_logit,
           ret_gn, dn_conv, dn_a_log, dn_dt_bias, dn_norm, w_br_a, w_br_b, w_out, peer_wq, peer_subkeys, peer_u,
           peer_v, final_norm):
    bc, lc, d = x_prompt.shape
    bl, ll, _ = x_sample.shape
    depth = w_ada.shape[0]
    tc = bc * lc
    t = tc + bl * ll
    assert bl + 1 <= 8 and tc % ll == 0

    tm = _pick_tile(math.gcd(tc, ll), (1024, 512, 256))
    n_ctx_blocks = tc // tm
    lat_blocks = ll // tm

    def seg_of_block(i):
        return jnp.where(i < n_ctx_blocks, 0, 1 + (i - n_ctx_blocks) // lat_blocks)

    x = jnp.concatenate([x_prompt.reshape(tc, d), x_sample.reshape(bl * ll, d)], axis=0)
    cvec = jnp.concatenate([c_ctx[None, :], c, jnp.zeros((8 - 1 - bl, d), F32)], axis=0)

    ret_states, dn_states = [], []
    for l in range(depth):
        mod = _adaln(cvec, w_ada[l], b_ada[l])
        mod8 = jnp.pad(mod.reshape(8, 6, d), ((0, 0), (0, 2), (0, 0)))

        w = w_in[l]
        w_main = jnp.concatenate([w[:, :DGL0], w[:, DGL0 + 4 * H_DN:]], axis=1).astype(BF16)
        wg = w[:, DGL0:DGL0 + 4 * H_DN].reshape(d, 4, H_DN).transpose(2, 1, 0)
        w_gate_t = jnp.pad(wg, ((0, 0), (0, 4), (0, 0))).reshape(8 * H_DN, d).astype(BF16)

        proj, gates_t = _inproj(x, mod8, seg_of_block, tm, norm_mix[l], w_main, w_gate_t)

        lg = jax.nn.log_sigmoid(ret_logit[l].astype(F32))
        ret_c, rs = _retention(proj, lg, ret_gn[l], None, None, n_batch=bc, seq=lc, row0=0, emit_state=True)
        (ret_out,) = _retention(proj, lg, ret_gn[l], state_ret[:, l], ret_c, n_batch=bl, seq=ll, row0=tc,
                                emit_state=False)
        dn_par = jnp.concatenate([dn_a_log[l], dn_dt_bias[l]], axis=0).astype(F32)
        dn_c, ds = _deltanet(proj, gates_t, dn_par, dn_conv[l], dn_norm[l], None, None, n_batch=bc, seq=lc,
                             row0=0, row_len=lc, heads=4, emit_state=True)
        (dn_out,) = _deltanet(proj, gates_t, dn_par, dn_conv[l], dn_norm[l], state_dn[:, l], dn_c, n_batch=bl,
                              seq=ll, row0=tc, row_len=GRID_W, heads=2, emit_state=False)
        ret_states.append(rs)
        dn_states.append(ds)

        merged = _merge(ret_out, dn_out, w_br_a[l].astype(BF16), w_br_b[l].astype(BF16), proj, d, tm)
        x1 = _outproj(merged, w_out[l].astype(BF16), x, mod8, seg_of_block, tm)

        h2, q3 = _peerq(x1, mod8, seg_of_block, tm, norm_ffn[l], peer_wq[l].astype(BF16))
        tb = min(512, tm)
        pb, pf = _peer_scores(q3, peer_subkeys[l].astype(BF16), tb)
        peer_t = _peer_dense(h2, pb, pf, peer_u[l].astype(BF16), peer_v[l].astype(BF16).T, tb, 1024)
        x = x1
        x_last_peer = peer_t
        assert depth == 1

    y_c, y_l = _final(x, x_last_peer, mod8, seg_of_block, tm, final_norm, tc)
    y_prompt = y_c.reshape(bc, lc, d)
    y_sample = y_l.reshape(bl, ll, d)
    new_state_ret = jnp.stack(ret_states, axis=1)
    new_state_dn = jnp.stack(dn_states, axis=1)
    return (y_prompt, y_sample, new_state_ret, new_state_dn)
```

```python
import functools
import math

import jax
import jax.numpy as jnp
from jax import lax
from jax.experimental import pallas as pl
from jax.experimental.pallas import tpu as pltpu

F32 = jnp.float32
BF16 = jnp.bfloat16

H_RET, DK_RET, DV_RET = 8, 128, 256
H_DN, DK_DN, DV_DN = 16, 128, 128
CONV_K = 5
GRID_W = 64
DN_CHUNK = 256
RET_CHUNK = 256
N_KEYS = 128
PEER_HEADS = 8
PEER_QDIM = 256
PEER_TOPK = 16
EPS = 1e-6

RET_QK = H_RET * DK_RET
RET_V = H_RET * DV_RET
DN_QK = H_DN * DK_DN
DN_V = H_DN * DV_DN
RQ0 = 0
RK0 = RQ0 + RET_QK
RV0 = RK0 + RET_QK
RG0 = RV0 + RET_V
DQ0 = RG0 + RET_V
DK0 = DQ0 + DN_QK
DV0 = DK0 + DN_QK
DZ0 = DV0 + DN_V
DGL0 = DZ0 + DN_V
BG0 = DZ0 + DN_V

VMEM_LIMIT = 56 * 1024 * 1024
NEG_INF = float("-inf")


def _cparams(sem):
    return pltpu.CompilerParams(dimension_semantics=sem, vmem_limit_bytes=VMEM_LIMIT)


def _nt_dot(a, b):
    return lax.dot_general(a, b, (((1,), (1,)), ((), ())), preferred_element_type=F32)


def _dot(a, b):
    return jnp.dot(a, b, preferred_element_type=F32)


def _silu(x):
    h = 0.5 * x
    return h + h * jnp.tanh(h)


def _pick_tile(n, cands):
    for c in cands:
        if n % c == 0:
            return c
    raise ValueError(f"no tile for {n}")


def _adaln_kernel(c_ref, w_ref, b_ref, o_ref):
    a = _silu(c_ref[...])
    o_ref[...] = jnp.dot(a, w_ref[...], precision=lax.Precision.HIGHEST,
                         preferred_element_type=F32) + b_ref[...]


def _adaln(cvec, w_ada, b_ada):
    d, n = w_ada.shape
    tn = _pick_tile(n, (1024, 512, 256, 128))
    return pl.pallas_call(
        _adaln_kernel,
        grid=(n // tn,),
        in_specs=[pl.BlockSpec((8, d), lambda j: (0, 0)),
                  pl.BlockSpec((d, tn), lambda j: (0, j)),
                  pl.BlockSpec((1, tn), lambda j: (0, j))],
        out_specs=pl.BlockSpec((8, tn), lambda j: (0, j)),
        out_shape=jax.ShapeDtypeStruct((8, n), F32),
        compiler_params=_cparams(("parallel",)),
        name="adaln",
    )(cvec, w_ada, b_ada.reshape(1, n))


def _modulated_norm(x, g, shift, scale):
    ms = jnp.mean(x * x, axis=-1, keepdims=True)
    return (x * lax.rsqrt(ms + EPS)) * g * (1.0 + scale) + shift


def _inproj_kernel(xc_ref, xl_ref, mod_ref, g_ref, wa_ref, wb_ref, wg_ref, o_ref, gt_ref, h_scr, *,
                   n_ctx_blocks, n_head_tiles):
    i = pl.program_id(0)
    j = pl.program_id(1)

    def prepare(x_ref):
        h = _modulated_norm(x_ref[...], g_ref[...], mod_ref[0:1, :], mod_ref[1:2, :]).astype(BF16)
        h_scr[...] = h
        gt_ref[...] = _nt_dot(wg_ref[...], h)

    pl.when((j == 0) & (i < n_ctx_blocks))(functools.partial(prepare, xc_ref))
    pl.when((j == 0) & (i >= n_ctx_blocks))(functools.partial(prepare, xl_ref))

    @pl.when(j < n_head_tiles)
    def _():
        o_ref[...] = _dot(h_scr[...], wa_ref[...]).astype(o_ref.dtype)

    @pl.when(j >= n_head_tiles)
    def _():
        o_ref[...] = _dot(h_scr[...], wb_ref[...]).astype(o_ref.dtype)


def _inproj(xc, xl, mod8, seg_of_block, tm, norm_g, w_all, w_tail, w_gate_t):
    d = xc.shape[1]
    t = xc.shape[0] + xl.shape[0]
    nm = BG0 + w_tail.shape[1]
    tn = _pick_tile(math.gcd(BG0, w_tail.shape[1]), (1024, 512, 256))
    nc = xc.shape[0] // tm
    nh = BG0 // tn
    once = pl.Buffered(1)
    return pl.pallas_call(
        functools.partial(_inproj_kernel, n_ctx_blocks=nc, n_head_tiles=nh),
        grid=(t // tm, nm // tn),
        in_specs=[pl.BlockSpec((tm, d), lambda i, j: (jnp.minimum(i, nc - 1), 0), pipeline_mode=once),
                  pl.BlockSpec((tm, d), lambda i, j: (jnp.maximum(i - nc, 0), 0), pipeline_mode=once),
                  pl.BlockSpec((None, 8, d), lambda i, j: (seg_of_block(i), 0, 0)),
                  pl.BlockSpec((1, d), lambda i, j: (0, 0)),
                  pl.BlockSpec((d, tn), lambda i, j: (0, jnp.minimum(j, nh - 1))),
                  pl.BlockSpec((d, tn), lambda i, j: (0, jnp.maximum(j - nh, 0))),
                  pl.BlockSpec((128, d), lambda i, j: (0, 0))],
        out_specs=[pl.BlockSpec((tm, tn), lambda i, j: (i, j)),
                   pl.BlockSpec((128, tm), lambda i, j: (0, i))],
        out_shape=[jax.ShapeDtypeStruct((t, nm), BF16),
                   jax.ShapeDtypeStruct((128, t), F32)],
        scratch_shapes=[pltpu.VMEM((tm, d), BF16)],
        compiler_params=_cparams(("parallel", "arbitrary")),
        name="inproj",
    )(xc, xl, mod8, norm_g.reshape(1, d), w_all, w_tail, w_gate_t)


def _merge_kernel(a_ref, b_ref, wa_ref, wb_ref, ga_ref, gb_ref, o_ref):
    ya = _dot(a_ref[...], wa_ref[...])
    yb = _dot(b_ref[...], wb_ref[...])
    ga = jax.nn.sigmoid(ga_ref[...].astype(F32))
    gb = jax.nn.sigmoid(gb_ref[...].astype(F32))
    o_ref[...] = (ga * ya + gb * yb).astype(o_ref.dtype)


def _merge(ret_out, dn_out, w_a, w_b, proj, d, tm):
    t = ret_out.shape[0]
    tn = _pick_tile(d, (512, 256, 128))
    ga0 = BG0 // tn
    gb0 = (BG0 + d) // tn
    return pl.pallas_call(
        _merge_kernel,
        grid=(t // tm, d // tn),
        in_specs=[pl.BlockSpec((tm, RET_V), lambda i, j: (i, 0)),
                  pl.BlockSpec((tm, DN_V), lambda i, j: (i, 0)),
                  pl.BlockSpec((RET_V, tn), lambda i, j: (0, j)),
                  pl.BlockSpec((DN_V, tn), lambda i, j: (0, j)),
                  pl.BlockSpec((tm, tn), lambda i, j: (i, ga0 + j)),
                  pl.BlockSpec((tm, tn), lambda i, j: (i, gb0 + j))],
        out_specs=pl.BlockSpec((tm, tn), lambda i, j: (i, j)),
        out_shape=jax.ShapeDtypeStruct((t, d), BF16),
        compiler_params=_cparams(("parallel", "arbitrary")),
        name="merge",
    )(ret_out, dn_out, w_a, w_b, proj, proj)


def _outproj_kernel(m_ref, w_ref, xc_ref, xl_ref, mod_ref, o_ref, *, n_ctx_blocks):
    y = _dot(m_ref[...], w_ref[...])
    x = jnp.where(pl.program_id(0) < n_ctx_blocks, xc_ref[...], xl_ref[...])
    o_ref[...] = x + mod_ref[2:3, :] * y


def _outproj(merged, w_out, xc, xl, mod8, seg_of_block, tm):
    t, d = merged.shape
    tn = _pick_tile(d, (512, 256, 128))
    nc = xc.shape[0] // tm
    return pl.pallas_call(
        functools.partial(_outproj_kernel, n_ctx_blocks=nc),
        grid=(t // tm, d // tn),
        in_specs=[pl.BlockSpec((tm, d), lambda i, j: (i, 0)),
                  pl.BlockSpec((d, tn), lambda i, j: (0, j)),
                  pl.BlockSpec((tm, tn), lambda i, j: (jnp.minimum(i, nc - 1), jnp.where(i < nc, j, 0))),
                  pl.BlockSpec((tm, tn), lambda i, j: (jnp.maximum(i - nc, 0), jnp.where(i < nc, 0, j))),
                  pl.BlockSpec((None, 8, tn), lambda i, j: (seg_of_block(i), 0, j))],
        out_specs=pl.BlockSpec((tm, tn), lambda i, j: (i, j)),
        out_shape=jax.ShapeDtypeStruct((t, d), F32),
        compiler_params=_cparams(("parallel", "arbitrary")),
        name="outproj",
    )(merged, w_out, xc, xl, mod8)


def _peerq_kernel(x_ref, mod_ref, g_ref, w_ref, h_ref, q_ref, h_scr):
    @pl.when(pl.program_id(1) == 0)
    def _():
        h = _modulated_norm(x_ref[...], g_ref[...], mod_ref[3:4, :], mod_ref[4:5, :]).astype(BF16)
        h_scr[...] = h
        h_ref[...] = h

    q = _dot(h_scr[...], w_ref[...]).astype(BF16)
    for i in range(q_ref.shape[0]):
        q_ref[i] = q[:, i * 128:(i + 1) * 128]


def _peerq(x1, mod8, seg_of_block, tm, norm_g, w_q):
    t, d = x1.shape
    nq = w_q.shape[1]
    tn = _pick_tile(nq, (512, 256, 128))
    return pl.pallas_call(
        _peerq_kernel,
        grid=(t // tm, nq // tn),
        in_specs=[pl.BlockSpec((tm, d), lambda i, j: (i, 0)),
                  pl.BlockSpec((None, 8, d), lambda i, j: (seg_of_block(i), 0, 0)),
                  pl.BlockSpec((1, d), lambda i, j: (0, 0)),
                  pl.BlockSpec((d, tn), lambda i, j: (0, j))],
        out_specs=[pl.BlockSpec((tm, d), lambda i, j: (i, 0)),
                   pl.BlockSpec((tn // 128, tm, 128), lambda i, j: (j, i, 0))],
        out_shape=[jax.ShapeDtypeStruct((t, d), BF16),
                   jax.ShapeDtypeStruct((nq // 128, t, 128), BF16)],
        scratch_shapes=[pltpu.VMEM((tm, d), BF16)],
        compiler_params=_cparams(("parallel", "arbitrary")),
        name="peer_query",
    )(x1, mod8, norm_g.reshape(1, d), w_q)


def _ret_kernel(lg_ref, q_ref, k_ref, v_ref, g_ref, gn_ref, *rest, n_chunks, has_init, has_dst, emit_state):
    rest = list(rest)
    s0_ref = rest.pop(0) if has_init else None
    if has_dst:
        rest.pop(0)
    o_ref = rest.pop(0)
    st_ref = rest.pop(0) if emit_state else None
    if n_chunks > 1:
        o_scr, sf_scr, sb_scr = rest
    c = RET_CHUNK
    scale = DK_RET ** -0.5
    h = pl.program_id(1)
    lgf = lg_ref[0, h]
    lgb = lg_ref[1, h]

    row = lax.broadcasted_iota(jnp.int32, (c, c), 0)
    col = lax.broadcasted_iota(jnp.int32, (c, c), 1)
    dlt = (row - col).astype(F32)
    dmat = (jnp.where(dlt >= 0, jnp.exp(lgf * jnp.maximum(dlt, 0.0)), 0.0)
            + jnp.where(dlt <= 0, jnp.exp(lgb * jnp.maximum(-dlt, 0.0)), 0.0)) * scale
    pos = lax.broadcasted_iota(jnp.int32, (c, DK_RET), 0).astype(F32)
    xi_f = jnp.exp(lgf * (pos + 1.0))
    xi_b = jnp.exp(lgb * (c - pos))
    zeta_f = jnp.exp(lgf * (c - 1.0 - pos)) * scale
    zeta_b = jnp.exp(lgb * pos) * scale
    dec_f = jnp.exp(jnp.full((1, DV_RET), lgf, F32) * float(c))
    dec_b = jnp.exp(jnp.full((1, DV_RET), lgb, F32) * float(c))

    def rows(i):
        return pl.ds(pl.multiple_of(i * c, c), c)

    def intra(r):
        q = q_ref[r, :]
        k = k_ref[r, :]
        v = v_ref[r, :]
        s = _nt_dot(q, k) * dmat
        return q.astype(F32), k.astype(F32), v, _dot(s.astype(BF16), v)

    def state_inc(kf, zeta, v):
        return _dot((kf * zeta).T.astype(BF16), v)

    def finish(o, r):
        mu = jnp.mean(o, axis=-1, keepdims=True)
        oc = o - mu
        var = jnp.mean(oc * oc, axis=-1, keepdims=True)
        on = oc * lax.rsqrt(var + EPS)
        o_ref[r, :] = (on * gn_ref[...] * _silu(g_ref[r, :].astype(F32))).astype(o_ref.dtype)

    if n_chunks == 1:
        r = pl.ds(0, c)
        qf, kf, v, o = intra(r)
        s_f = state_inc(kf, zeta_f, v)
        s_b = state_inc(kf, zeta_b, v)
        if has_init:
            s0f = s0_ref[0]
            s0b = s0_ref[1]
            o = o + _dot((qf * xi_f).astype(BF16), s0f.astype(BF16))
            o = o + _dot((qf * xi_b).astype(BF16), s0b.astype(BF16))
            s_f = s_f + dec_f * s0f
            s_b = s_b + dec_b * s0b
        finish(o, r)
        if emit_state:
            st_ref[0] = s_f
            st_ref[1] = s_b
        return

    if has_init:
        sf_scr[...] = s0_ref[0]
        sb_scr[...] = s0_ref[1]
    else:
        sf_scr[...] = jnp.zeros_like(sf_scr)
        sb_scr[...] = jnp.zeros_like(sb_scr)

    def fwd(i, carry):
        r = rows(i)
        qf, kf, v, o = intra(r)
        s = sf_scr[...]
        o_scr[r, :] = o + _dot((qf * xi_f).astype(BF16), s.astype(BF16))
        sf_scr[...] = dec_f * s + state_inc(kf, zeta_f, v)
        return carry

    lax.fori_loop(0, n_chunks, fwd, 0)

    def bwd(ii, carry):
        i = n_chunks - 1 - ii
        r = rows(i)
        qf = q_ref[r, :].astype(F32)
        kf = k_ref[r, :].astype(F32)
        v = v_ref[r, :]
        s = sb_scr[...]
        o = o_scr[r, :] + _dot((qf * xi_b).astype(BF16), s.astype(BF16))
        sb_scr[...] = dec_b * s + state_inc(kf, zeta_b, v)
        finish(o, r)
        return carry

    lax.fori_loop(0, n_chunks, bwd, 0)
    if emit_state:
        st_ref[0] = sf_scr[...]
        st_ref[1] = sb_scr[...]


def _retention(proj, lg, ret_gn, s0, dst, *, n_batch, seq, row0, emit_state):
    n_chunks = seq // RET_CHUNK
    rb0 = row0 // seq
    has_init = s0 is not None
    aliases = {}
    in_specs = [pl.BlockSpec(memory_space=pltpu.SMEM),
                pl.BlockSpec((seq, DK_RET), lambda b, h: (rb0 + b, RQ0 // DK_RET + h)),
                pl.BlockSpec((seq, DK_RET), lambda b, h: (rb0 + b, RK0 // DK_RET + h)),
                pl.BlockSpec((seq, DV_RET), lambda b, h: (rb0 + b, RV0 // DV_RET + h)),
                pl.BlockSpec((seq, DV_RET), lambda b, h: (rb0 + b, RG0 // DV_RET + h)),
                pl.BlockSpec((1, DV_RET), lambda b, h: (0, h))]
    args = [lg, proj, proj, proj, proj, ret_gn.reshape(1, RET_V)]
    if has_init:
        in_specs.append(pl.BlockSpec((None, 2, None, DK_RET, DV_RET), lambda b, h: (b, 0, h, 0, 0)))
        args.append(s0)
    if dst is not None:
        in_specs.append(pl.BlockSpec(memory_space=pl.ANY))
        aliases[len(args)] = 0
        args.append(dst)
    out_specs = [pl.BlockSpec((seq, DV_RET), lambda b, h: (rb0 + b, h))]
    out_shape = [jax.ShapeDtypeStruct((proj.shape[0], RET_V), BF16)]
    if emit_state:
        out_specs.append(pl.BlockSpec((None, 2, None, DK_RET, DV_RET), lambda b, h: (b, 0, h, 0, 0)))
        out_shape.append(jax.ShapeDtypeStruct((n_batch, 2, H_RET, DK_RET, DV_RET), F32))
    scratch = []
    if n_chunks > 1:
        scratch = [pltpu.VMEM((seq, DV_RET), F32), pltpu.VMEM((DK_RET, DV_RET), F32),
                   pltpu.VMEM((DK_RET, DV_RET), F32)]
    return pl.pallas_call(
        functools.partial(_ret_kernel, n_chunks=n_chunks, has_init=has_init, has_dst=dst is not None,
                          emit_state=emit_state),
        grid=(n_batch, H_RET),
        in_specs=in_specs,
        out_specs=out_specs,
        out_shape=out_shape,
        input_output_aliases=aliases,
        scratch_shapes=scratch,
        compiler_params=_cparams(("parallel", "arbitrary")),
        name="retention_ctx" if emit_state else "retention_lat",
    )(*args)


def _dn_kernel(par_ref, q_ref, k_ref, v_ref, z_ref, gt_ref, cq_ref, ck_ref, cv_ref, nrm_ref, *rest,
               seq, row_len, heads, has_init, has_dst, emit_state):
    rest = list(rest)
    s0_ref = rest.pop(0) if has_init else None
    if has_dst:
        rest.pop(0)
    o_ref = rest.pop(0)
    st_ref = rest.pop(0) if emit_state else None
    rows_scr, o_scr, s_scr, qkv_scr = rest
    cb = DN_CHUNK
    n_blocks = seq // cb
    hg = pl.program_id(1)

    lane = lax.broadcasted_iota(jnp.int32, (1, seq), 1) & (cb - 1)

    def splat(v):
        return jnp.full((1, seq), v, F32)

    def prefix(x):
        s = 1
        while s < cb:
            x = x + jnp.where(lane >= s, pltpu.roll(x, s, 1), 0.0)
            s *= 2
        return x

    def suffix(x):
        s = 1
        while s < cb:
            x = x + jnp.where(lane < cb - s, pltpu.roll(x, seq - s, 1), 0.0)
            s *= 2
        return x

    for g in range(heads):
        h = hg * heads + g
        r0 = 8 * g
        g_f = (-jnp.exp(splat(par_ref[0, h]))
               * jax.nn.softplus(gt_ref[r0 + 2:r0 + 3, :] + splat(par_ref[2, h])))
        g_b = (-jnp.exp(splat(par_ref[1, h]))
               * jax.nn.softplus(gt_ref[r0 + 3:r0 + 4, :] + splat(par_ref[3, h])))
        rows_scr[r0:r0 + 1, :] = jax.nn.sigmoid(gt_ref[r0:r0 + 1, :])
        rows_scr[r0 + 1:r0 + 2, :] = prefix(g_f)
        rows_scr[r0 + 2:r0 + 3, :] = suffix(g_f) - g_f
        rows_scr[r0 + 3:r0 + 4, :] = jax.nn.sigmoid(gt_ref[r0 + 1:r0 + 2, :])
        rows_scr[r0 + 4:r0 + 5, :] = suffix(g_b)
        rows_scr[r0 + 5:r0 + 6, :] = prefix(g_b) - g_b
        rows_scr[r0 + 6:r0 + 8, :] = jnp.zeros((2, seq), F32)
        for direction in range(2):
            if has_init:
                s_scr[g, direction] = s0_ref[direction, g]
            else:
                s_scr[g, direction] = jnp.zeros((DK_DN, DV_DN), F32)

    ri = lax.broadcasted_iota(jnp.int32, (cb, cb), 0)
    ci = lax.broadcasted_iota(jnp.int32, (cb, cb), 1)
    eye = (ri == ci).astype(F32)
    pair_masks = []
    sz = 2
    while sz < cb:
        pair = ((ri // (2 * sz)) == (ci // (2 * sz))) & ((ri // sz) != (ci // sz))
        pair_masks.append(jnp.where(pair, 1.0, 0.0).astype(BF16))
        sz *= 2
    tpos =lax.broadcasted_iota(jnp.int32, (cb, DK_DN), 0) & (row_len - 1)

    def conv(x, w_ref, lanes):
        acc = x * w_ref[CONV_K // 2:CONV_K // 2 + 1, lanes]
        for i in range(CONV_K):
            d = i - CONV_K // 2
            if d == 0:
                continue
            sh = pltpu.roll(x, (-d) % cb, 0)
            ok = (tpos + d >= 0) & (tpos + d < row_len)
            acc = acc + jnp.where(ok, sh, 0.0) * w_ref[i:i + 1, lanes]
        return _silu(acc)

    def l2n(x):
        return x * lax.rsqrt(jnp.sum(x * x, axis=-1, keepdims=True) + EPS)

    def col_of(row):
        parts = []
        for t in range(cb // 128):
            parts.append(jnp.broadcast_to(row[:, t * 128:(t + 1) * 128], (128, 128)).T)
        return jnp.concatenate(parts, axis=0)

    def rows_of(bi):
        return pl.ds(pl.multiple_of(bi * cb, cb), cb)

    def prepare(bi, carry):
        r = rows_of(bi)
        for g in range(heads):
            lanes = slice(g * DK_DN, (g + 1) * DK_DN)
            qkv_scr[g, 0, r, :] = l2n(conv(q_ref[r, lanes].astype(F32), cq_ref, lanes)) * (DK_DN ** -0.5)
            qkv_scr[g, 1, r, :] = l2n(conv(k_ref[r, lanes].astype(F32), ck_ref, lanes))
            qkv_scr[g, 2, r, :] = conv(v_ref[r, lanes].astype(F32), cv_ref, lanes)
        return carry

    if n_blocks == 1:
        prepare(0, 0)
    else:
        lax.fori_loop(0, n_blocks, prepare, 0)

    def blocks(chains):
        cs = []
        for g, bi, direction in chains:
            r = rows_of(bi)
            base = 8 * g + (0 if direction == 0 else 3)
            cs.append(dict(g=g, d=direction, r=r, beta_row=rows_scr[base:base + 1, r],
                           g_row=rows_scr[base + 1:base + 2, r], e_row=rows_scr[base + 2:base + 3, r]))
        for c in cs:
            c["q"] = qkv_scr[c["g"], 0, c["r"], :]
            c["k"] = qkv_scr[c["g"], 1, c["r"], :]
            c["v"] = qkv_scr[c["g"], 2, c["r"], :]
            c["k16"] = c["k"].astype(BF16)
            c["kk"] = _nt_dot(c["k16"], c["k16"])
        for c in cs:
            c["beta_c"] = col_of(c["beta_row"])
            c["g_c"] = col_of(c["g_row"])
            c["incl"] = (ri >= ci) if c["d"] == 0 else (ri <= ci)
            strict = (ri > ci) if c["d"] == 0 else (ri < ci)
            g_cw = jnp.concatenate([c["g_c"]] * (cb // 128), axis=1)
            beta_cw = jnp.concatenate([c["beta_c"]] * (cb // 128), axis=1)
            c["lmat"] = jnp.exp(jnp.where(c["incl"], g_cw - c["g_row"], NEG_INF))
            c["m"] = jnp.where(strict, c["kk"] * beta_cw * c["lmat"], 0.0)
            c["x"] = eye - jnp.where((ri // 2) == (ci // 2), c["m"], 0.0)
        for c in cs:
            c["m16"] = c["m"].astype(BF16)
        for pair16 in pair_masks:
            for c in cs:
                c["x16"] = c["x"].astype(BF16)
                c["t16"] = _dot(c["x16"], c["m16"] * pair16).astype(BF16)
            for c in cs:
                c["x"] = c["x"] - _dot(c["t16"], c["x16"])
        for c in cs:
            eg_c = jnp.exp(c["g_c"])
            rhs = jnp.concatenate([c["v"] * c["beta_c"], c["k"] * (c["beta_c"] * eg_c)], axis=1).astype(BF16)
            uw = _dot(c["x"].astype(BF16), rhs)
            c["u"] = uw[:, :DV_DN]
            c["w16"] = uw[:, DV_DN:].astype(BF16)
            c["qg16"] = (c["q"] * eg_c).astype(BF16)
            c["aqk"] = (_nt_dot(c["q"].astype(BF16), c["k16"]) * c["lmat"]).astype(BF16)
            c["kdec_t"] = (c["k"] * jnp.exp(col_of(c["e_row"]))).T.astype(BF16)
        outs = []
        for c in cs:
            s = s_scr[c["g"], c["d"]]
            s16 = s.astype(BF16)
            vn16 = (c["u"] - _dot(c["w16"], s16)).astype(BF16)
            outs.append(_dot(c["qg16"], s16) + _dot(c["aqk"], vn16))
            last = cb - 1 if c["d"] == 0 else 0
            s_scr[c["g"], c["d"]] = s * jnp.exp(c["g_c"][last:last + 1, :]) + _dot(c["kdec_t"], vn16)
        return outs

    def finish(g, bi, d):
        r = rows_of(bi)
        lanes = slice(g * DV_DN, (g + 1) * DV_DN)
        d = d * lax.rsqrt(jnp.mean(d * d, axis=-1, keepdims=True) + EPS)
        o_ref[r, lanes] = (d * nrm_ref[...] * _silu(z_ref[r, lanes].astype(F32))).astype(o_ref.dtype)

    if n_blocks == 1:
        outs = blocks([(g, 0, d) for g in range(heads) for d in range(2)])
        for g in range(heads):
            finish(g, 0, outs[2 * g] + outs[2 * g + 1])
    else:
        per = 2 if n_blocks % 4 == 0 else 1

        def chains_of(it):
            return [(g, per * it + j if d == 0 else n_blocks - 1 - per * it - j, d)
                    for g in range(heads) for d in range(2) for j in range(per)]

        def first_half(it, carry):
            chains = chains_of(it)
            for (g, bi, _), out in zip(chains, blocks(chains)):
                o_scr[g, rows_of(bi), :] = out
            return carry

        def second_half(it, carry):
            chains = chains_of(it)
            for (g, bi, _), out in zip(chains, blocks(chains)):
                finish(g, bi, out + o_scr[g, rows_of(bi), :])
            return carry

        n_it = n_blocks // per
        lax.fori_loop(0, n_it // 2, first_half, 0)
        lax.fori_loop(n_it // 2, n_it, second_half, 0)
    if emit_state:
        for g in range(heads):
            st_ref[0, g] = s_scr[g, 0]
            st_ref[1, g] = s_scr[g, 1]


def _deltanet(proj, gates_t, dn_par, dn_conv, dn_norm, s0, dst, *, n_batch, seq, row0, row_len, heads,
              emit_state):
    rb0 = row0 // seq
    has_init = s0 is not None
    aliases = {}
    wd = heads * DK_DN
    in_specs = [pl.BlockSpec(memory_space=pltpu.SMEM),
                pl.BlockSpec((seq, wd), lambda b, h: (rb0 + b, DQ0 // wd + h)),
                pl.BlockSpec((seq, wd), lambda b, h: (rb0 + b, DK0 // wd + h)),
                pl.BlockSpec((seq, wd), lambda b, h: (rb0 + b, DV0 // wd + h)),
                pl.BlockSpec((seq, wd), lambda b, h: (rb0 + b, DZ0 // wd + h)),
                pl.BlockSpec((8 * heads, seq), lambda b, h: (h, rb0 + b)),
                pl.BlockSpec((CONV_K, wd), lambda b, h: (0, h)),
                pl.BlockSpec((CONV_K, wd), lambda b, h: (0, DN_QK // wd + h)),
                pl.BlockSpec((CONV_K, wd), lambda b, h: (0, 2 * DN_QK // wd + h)),
                pl.BlockSpec((1, DV_DN), lambda b, h: (0, 0))]
    args = [dn_par, proj, proj, proj, proj, gates_t, dn_conv, dn_conv, dn_conv, dn_norm.reshape(1, DV_DN)]
    if has_init:
        in_specs.append(pl.BlockSpec((None, 2, heads, DK_DN, DV_DN), lambda b, h: (b, 0, h, 0, 0)))
        args.append(s0)
    if dst is not None:
        in_specs.append(pl.BlockSpec(memory_space=pl.ANY))
        aliases[len(args)] = 0
        args.append(dst)
    out_specs = [pl.BlockSpec((seq, wd), lambda b, h: (rb0 + b, h))]
    out_shape = [jax.ShapeDtypeStruct((proj.shape[0], DN_V), BF16)]
    if emit_state:
        out_specs.append(pl.BlockSpec((None, 2, heads, DK_DN, DV_DN), lambda b, h: (b, 0, h, 0, 0)))
        out_shape.append(jax.ShapeDtypeStruct((n_batch, 2, H_DN, DK_DN, DV_DN), F32))
    scratch = [pltpu.VMEM((8 * heads, seq), F32), pltpu.VMEM((heads, seq, DV_DN), F32),
               pltpu.VMEM((heads, 2, DK_DN, DV_DN), F32), pltpu.VMEM((heads, 3, seq, DK_DN), F32)]
    n_blocks = seq // DN_CHUNK
    assert seq % DN_CHUNK == 0 and (n_blocks == 1 or n_blocks % 2 == 0)
    assert DN_CHUNK % row_len == 0 and row_len & (row_len - 1) == 0 and H_DN % heads == 0
    return pl.pallas_call(
        functools.partial(_dn_kernel, seq=seq, row_len=row_len, heads=heads, has_init=has_init,
                          has_dst=dst is not None, emit_state=emit_state),
        grid=(n_batch, H_DN // heads),
        in_specs=in_specs,
        out_specs=out_specs,
        out_shape=out_shape,
        input_output_aliases=aliases,
        scratch_shapes=scratch,
        compiler_params=_cparams(("parallel", "arbitrary")),
        name="deltanet_ctx" if emit_state else "deltanet_lat",
    )(*args)


def _cand_pairs():
    n = PEER_TOPK + 1
    return [(a, b) for a in range(n) for b in range(n) if (a + 1) * (b + 1) <= n]


def _peer_score_kernel(q_ref, sk_ref, pb_ref, pf_ref, s_scr, sv_scr, gp_scr):
    big_rank = float(N_KEYS)

    def per_head(h, carry):
        for p in range(2):
            s = _nt_dot(sk_ref[p], q_ref[2 * h + p])
            s_scr[p, h] = s
            cur = s
            rank = jnp.full(s.shape, big_rank, F32)
            for r in range(PEER_TOPK + 1):
                m = jnp.max(cur, axis=0, keepdims=True)
                sv_scr[p, r, pl.ds(h, 1), :] = m
                if p == 1:
                    rank = jnp.where(cur == m, float(r), rank)
                cur = jnp.where(cur == m, NEG_INF, cur)
            if p == 1:
                pb_ref[0, h] = rank.astype(BF16)
        return carry

    lax.fori_loop(0, PEER_HEADS, per_head, 0)
    cands = [sv_scr[0, a] + sv_scr[1, b] for a, b in _cand_pairs()]
    mx = cands[0]
    z = jnp.zeros_like(mx)
    for r in range(PEER_TOPK):
        m = functools.reduce(jnp.maximum, cands)
        z = z + jnp.exp(m - mx)
        cands = [jnp.where(c == m, NEG_INF, c) for c in cands]
    runner_up = functools.reduce(jnp.maximum, cands)
    gp_scr[0] = 0.5 * (m + runner_up)
    gp_scr[1] = 1.0 / z

    def finish_head(h, carry):
        row = pl.ds(h, 1)
        s1 = s_scr[0, h]
        pb_ref[1, h] = jnp.exp(s_scr[1, h] - sv_scr[1, 0, row, :]).astype(BF16)
        pf_ref[0, h] = jnp.exp(s1 - sv_scr[0, 0, row, :]) * gp_scr[1, row, :]
        need = gp_scr[0, row, :] - s1
        cnt = jnp.zeros_like(need)
        for r in range(PEER_TOPK + 1):
            cnt = cnt + jnp.where(sv_scr[1, r, row, :] >= need, 1.0, 0.0)
        pf_ref[1, h] = cnt
        return carry

    lax.fori_loop(0, PEER_HEADS, finish_head, 0)


def _peer_scores(q3, subkeys, tb):
    nhp, t, _ = q3.shape
    return pl.pallas_call(
        _peer_score_kernel,
        grid=(t // tb,),
        in_specs=[pl.BlockSpec((nhp, tb, 128), lambda i: (0, i, 0)),
                  pl.BlockSpec((2, N_KEYS, PEER_QDIM // 2), lambda i: (0, 0, 0))],
        out_specs=[pl.BlockSpec((2, PEER_HEADS, N_KEYS, tb), lambda i: (0, 0, 0, i)),
                   pl.BlockSpec((2, PEER_HEADS, N_KEYS, tb), lambda i: (0, 0, 0, i))],
        out_shape=[jax.ShapeDtypeStruct((2, PEER_HEADS, N_KEYS, t), BF16),
                   jax.ShapeDtypeStruct((2, PEER_HEADS, N_KEYS, t), F32)],
        scratch_shapes=[pltpu.VMEM((2, PEER_HEADS, N_KEYS, tb), F32),
                        pltpu.VMEM((2, PEER_TOPK + 1, PEER_HEADS, tb), F32),
                        pltpu.VMEM((2, PEER_HEADS, tb), F32)],
        compiler_params=_cparams(("parallel",)),
        name="peer_scores",
    )(q3, subkeys)


def _gelu_tanh(x):
    return 0.5 * x * (1.0 + jnp.tanh(0.7978845608028654 * (x + 0.044715 * (x * x * x))))


def _peer_dense_kernel(h_ref, pb_ref, pf_ref, u_ref, vt_ref, o_ref, *, n_i1):
    e = pl.program_id(1)
    zero = jnp.zeros((), BF16)
    tb, d = h_ref.shape
    kt = 256
    per_m = kt // N_KEYS

    @pl.when(e == 0)
    def _():
        o_ref[...] = jnp.zeros_like(o_ref)

    def gate_head(ii, h):
        row = pl.ds(e * n_i1 + ii, 1)

        def bcast(x):
            x = jnp.broadcast_to(x, (16, x.shape[1])).astype(BF16)
            return jnp.concatenate([x] * (N_KEYS // 16), axis=0)

        e1 = bcast(pf_ref[0, h, row, :])
        cnt = bcast(pf_ref[1, h, row, :])
        return jnp.where(pb_ref[0, h] < cnt, pb_ref[1, h] * e1, zero)

    wa_parts = []
    for m in range(n_i1 // per_m):
        todo = [(per_m * m + j, h) for j in range(per_m) for h in range(PEER_HEADS)]
        gate = {}

        def one_gate():
            ii, h = todo.pop(0)
            g = gate_head(ii, h)
            gate[ii] = g if h == 0 else gate[ii] + g

        accs = [None] * (tb // kt)
        for k in range(d // kt):
            for n in range(tb // kt):
                part = _nt_dot(u_ref[m * kt:(m + 1) * kt, k * kt:(k + 1) * kt],
                               h_ref[n * kt:(n + 1) * kt, k * kt:(k + 1) * kt])
                accs[n] = part if accs[n] is None else accs[n] + part
                if todo:
                    one_gate()
        while todo:
            one_gate()
        act = jnp.concatenate([_gelu_tanh(a).astype(BF16) for a in accs], axis=1)
        w = jnp.concatenate([gate[per_m * m + j] for j in range(per_m)], axis=0)
        wa_parts.append(w * act)
    o_ref[...] += _dot(vt_ref[...], jnp.concatenate(wa_parts, axis=0))


def _peer_dense(h2, pb, pf, u_tab, v_tab_t, tb, et):
    t, d = h2.shape
    ne = u_tab.shape[0]
    once = pl.Buffered(1)
    return pl.pallas_call(
        functools.partial(_peer_dense_kernel, n_i1=et // N_KEYS),
        grid=(t // tb, ne // et),
        in_specs=[pl.BlockSpec((tb, d), lambda i, e: (i, 0), pipeline_mode=once),
                  pl.BlockSpec((2, PEER_HEADS, N_KEYS, tb), lambda i, e: (0, 0, 0, i), pipeline_mode=once),
                  pl.BlockSpec((2, PEER_HEADS, N_KEYS, tb), lambda i, e: (0, 0, 0, i), pipeline_mode=once),
                  pl.BlockSpec((et, d), lambda i, e: (e, 0)),
                  pl.BlockSpec((d, et), lambda i, e: (0, e))],
        out_specs=pl.BlockSpec((d, tb), lambda i, e: (0, i)),
        out_shape=jax.ShapeDtypeStruct((d, t), F32),
        compiler_params=_cparams(("parallel", "arbitrary")),
        name="peer_dense",
    )(h2, pb, pf, u_tab, v_tab_t)


def _final_kernel(x_ref, p_ref, mod_ref, g_ref, oc_ref, ol_ref, *, n_ctx_blocks):
    x = x_ref[...] + mod_ref[5:6, :] * p_ref[...].T
    ms = jnp.mean(x * x, axis=-1, keepdims=True)
    y = (x * lax.rsqrt(ms + EPS)) * g_ref[...]
    i = pl.program_id(0)

    @pl.when(i < n_ctx_blocks)
    def _():
        oc_ref[...] = y

    @pl.when(i >= n_ctx_blocks)
    def _():
        ol_ref[...] = y


def _final(x1, peer_t, mod8, seg_of_block, tm_seg, final_norm, tc):
    t, d = x1.shape
    tm = 256
    sub = tm_seg // tm
    nc = tc // tm
    return pl.pallas_call(
        functools.partial(_final_kernel, n_ctx_blocks=nc),
        grid=(t // tm,),
        in_specs=[pl.BlockSpec((tm, d), lambda i: (i, 0)),
                  pl.BlockSpec((d, tm), lambda i: (0, i)),
                  pl.BlockSpec((None, 8, d), lambda i: (seg_of_block(i // sub), 0, 0)),
                  pl.BlockSpec((1, d), lambda i: (0, 0))],
        out_specs=[pl.BlockSpec((tm, d), lambda i: (jnp.minimum(i, nc - 1), 0)),
                   pl.BlockSpec((tm, d), lambda i: (jnp.maximum(i - nc, 0), 0))],
        out_shape=[jax.ShapeDtypeStruct((tc, d), F32), jax.ShapeDtypeStruct((t - tc, d), F32)],
        compiler_params=_cparams(("arbitrary",)),
        name="final_norm",
    )(x1, peer_t, mod8, final_norm.reshape(1, d))


def kernel(x_prompt, x_sample, state_ret, state_dn, c, c_ctx, w_ada, b_ada, norm_mix, norm_ffn, w_in, ret_logit,
           ret_gn, dn_conv, dn_a_log, dn_dt_bias, dn_norm, w_br_a, w_br_b, w_out, peer_wq, peer_subkeys, peer_u,
           peer_v, final_norm):
    bc, lc, d = x_prompt.shape
    bl, ll, _ = x_sample.shape
    depth = w_ada.shape[0]
    tc = bc * lc
    t = tc + bl * ll
    assert bl + 1 <= 8 and tc % ll == 0

    tm = _pick_tile(math.gcd(tc, ll), (1024, 512, 256))
    n_ctx_blocks = tc // tm
    lat_blocks = ll // tm

    def seg_of_block(i):
        return jnp.where(i < n_ctx_blocks, 0, 1 + (i - n_ctx_blocks) // lat_blocks)

    xc = x_prompt.reshape(tc, d)
    xl = x_sample.reshape(bl * ll, d)
    cvec = jnp.concatenate([c_ctx[None, :], c, jnp.zeros((8 - 1 - bl, d), F32)], axis=0)

    ret_states, dn_states = [], []
    for l in range(depth):
        mod = _adaln(cvec, w_ada[l], b_ada[l])
        mod8 = jnp.pad(mod.reshape(8, 6, d), ((0, 0), (0, 2), (0, 0)))

        w = w_in[l].astype(BF16)
        wg = w[:, DGL0:DGL0 + 4 * H_DN].reshape(d, 4, H_DN).transpose(2, 1, 0)
        w_gate_t = jnp.pad(wg, ((0, 0), (0, 4), (0, 0))).reshape(8 * H_DN, d)

        proj, gates_t = _inproj(xc, xl, mod8, seg_of_block, tm, norm_mix[l], w, w[:, DGL0 + 4 * H_DN:], w_gate_t)

        lg = jax.nn.log_sigmoid(ret_logit[l].astype(F32))
        ret_c, rs = _retention(proj, lg, ret_gn[l], None, None, n_batch=bc, seq=lc, row0=0, emit_state=True)
        (ret_out,) = _retention(proj, lg, ret_gn[l], state_ret[:, l], ret_c, n_batch=bl, seq=ll, row0=tc,
                                emit_state=False)
        dn_par = jnp.concatenate([dn_a_log[l], dn_dt_bias[l]], axis=0).astype(F32)
        dn_c, ds = _deltanet(proj, gates_t, dn_par, dn_conv[l], dn_norm[l], None, None, n_batch=bc, seq=lc,
                             row0=0, row_len=lc, heads=4, emit_state=True)
        (dn_out,) = _deltanet(proj, gates_t, dn_par, dn_conv[l], dn_norm[l], state_dn[:, l], dn_c, n_batch=bl,
                              seq=ll, row0=tc, row_len=GRID_W, heads=2, emit_state=False)
        ret_states.append(rs)
        dn_states.append(ds)

        merged = _merge(ret_out, dn_out, w_br_a[l].astype(BF16), w_br_b[l].astype(BF16), proj, d, tm)
        x1 = _outproj(merged, w_out[l].astype(BF16), xc, xl, mod8, seg_of_block, tm)

        h2, q3 = _peerq(x1, mod8, seg_of_block, tm, norm_ffn[l], peer_wq[l].astype(BF16))
        tb = min(512, tm)
        pb, pf = _peer_scores(q3, peer_subkeys[l].astype(BF16), tb)
        peer_t = _peer_dense(h2, pb, pf, peer_u[l].astype(BF16), peer_v[l].astype(BF16).T, tb, 1024)
        assert depth == 1

    y_c, y_l = _final(x1, peer_t, mod8, seg_of_block, tm, final_norm, tc)
    y_prompt = y_c.reshape(bc, lc, d)
    y_sample = y_l.reshape(bl, ll, d)
    new_state_ret = jnp.stack(ret_states, axis=1)
    new_state_dn = jnp.stack(dn_states, axis=1)
    return (y_prompt, y_sample, new_state_ret, new_state_dn)
```

```python
import functools
import math

import jax
import jax.numpy as jnp
from jax import lax
from jax.experimental import pallas as pl
from jax.experimental.pallas import tpu as pltpu

F32 = jnp.float32
BF16 = jnp.bfloat16

H_RET, DK_RET, DV_RET = 8, 128, 256
H_DN, DK_DN, DV_DN = 16, 128, 128
CONV_K = 5
GRID_W = 64
DN_CHUNK = 256
DN_SLAB = 64
RET_CHUNK = 256
N_KEYS = 128
PEER_HEADS = 8
PEER_QDIM = 256
PEER_TOPK = 16
EPS = 1e-6

RET_QK = H_RET * DK_RET
RET_V = H_RET * DV_RET
DN_QK = H_DN * DK_DN
DN_V = H_DN * DV_DN
RQ0 = 0
RK0 = RQ0 + RET_QK
RV0 = RK0 + RET_QK
RG0 = RV0 + RET_V
DQ0 = RG0 + RET_V
DK0 = DQ0 + DN_QK
DV0 = DK0 + DN_QK
DZ0 = DV0 + DN_V
DGL0 = DZ0 + DN_V
BG0 = DZ0 + DN_V

V7X_VMEM_BYTES = 64 * 1024 * 1024
VMEM_LIMIT = V7X_VMEM_BYTES - 8 * 1024 * 1024
V7X_MXU_DIM = 256
LANES = 128
PEER_TOKEN_TILE = 512
PEER_EXPERT_TILE = 1024
FINAL_TILE = 256
NEG_INF = float("-inf")


def _cparams(sem):
    return pltpu.CompilerParams(dimension_semantics=sem, vmem_limit_bytes=VMEM_LIMIT)


def _nt_dot(a, b):
    return lax.dot_general(a, b, (((1,), (1,)), ((), ())), preferred_element_type=F32)


def _dot(a, b):
    return jnp.dot(a, b, preferred_element_type=F32)


def _silu(x):
    h = 0.5 * x
    return h + h * jnp.tanh(h)


def _pick_tile(n, cands):
    for c in cands:
        if n % c == 0:
            return c
    raise ValueError(f"no tile for {n}")


def _adaln_kernel(c_ref, w_ref, b_ref, o_ref):
    a = _silu(c_ref[...])
    o_ref[...] = jnp.dot(a, w_ref[...], precision=lax.Precision.HIGHEST,
                         preferred_element_type=F32) + b_ref[...]


def _adaln(cvec, w_ada, b_ada):
    d, n = w_ada.shape
    tn = _pick_tile(n, (1024, 512, 256, 128))
    return pl.pallas_call(
        _adaln_kernel,
        grid=(n // tn,),
        in_specs=[pl.BlockSpec((8, d), lambda j: (0, 0)),
                  pl.BlockSpec((d, tn), lambda j: (0, j)),
                  pl.BlockSpec((1, tn), lambda j: (0, j))],
        out_specs=pl.BlockSpec((8, tn), lambda j: (0, j)),
        out_shape=jax.ShapeDtypeStruct((8, n), F32),
        compiler_params=_cparams(("parallel",)),
        name="adaln",
    )(cvec, w_ada, b_ada.reshape(1, n))


def _modulated_norm(x, g, shift, scale):
    ms = jnp.mean(x * x, axis=-1, keepdims=True)
    return (x * lax.rsqrt(ms + EPS)) * g * (1.0 + scale) + shift


def _inproj_kernel(xc_ref, xl_ref, mod_ref, g_ref, wa_ref, wb_ref, wg_ref, o_ref, gt_ref, h_scr, *,
                   n_ctx_blocks, n_head_tiles):
    i = pl.program_id(0)
    j = pl.program_id(1)

    def prepare(x_ref):
        h = _modulated_norm(x_ref[...], g_ref[...], mod_ref[0:1, :], mod_ref[1:2, :]).astype(BF16)
        h_scr[...] = h
        gt_ref[...] = _nt_dot(wg_ref[...], h)

    pl.when((j == 0) & (i < n_ctx_blocks))(functools.partial(prepare, xc_ref))
    pl.when((j == 0) & (i >= n_ctx_blocks))(functools.partial(prepare, xl_ref))

    @pl.when(j < n_head_tiles)
    def _():
        o_ref[...] = _dot(h_scr[...], wa_ref[...]).astype(o_ref.dtype)

    @pl.when(j >= n_head_tiles)
    def _():
        o_ref[...] = _dot(h_scr[...], wb_ref[...]).astype(o_ref.dtype)


def _inproj(xc, xl, mod8, seg_of_block, tm, norm_g, w_all, w_tail, w_gate_t):
    d = xc.shape[1]
    t = xc.shape[0] + xl.shape[0]
    nm = BG0 + w_tail.shape[1]
    tn = _pick_tile(math.gcd(BG0, w_tail.shape[1]), (1024, 512, 256))
    nc = xc.shape[0] // tm
    nh = BG0 // tn
    once = pl.Buffered(1)
    return pl.pallas_call(
        functools.partial(_inproj_kernel, n_ctx_blocks=nc, n_head_tiles=nh),
        grid=(t // tm, nm // tn),
        in_specs=[pl.BlockSpec((tm, d), lambda i, j: (jnp.minimum(i, nc - 1), 0), pipeline_mode=once),
                  pl.BlockSpec((tm, d), lambda i, j: (jnp.maximum(i - nc, 0), 0), pipeline_mode=once),
                  pl.BlockSpec((None, 8, d), lambda i, j: (seg_of_block(i), 0, 0)),
                  pl.BlockSpec((1, d), lambda i, j: (0, 0)),
                  pl.BlockSpec((d, tn), lambda i, j: (0, jnp.minimum(j, nh - 1))),
                  pl.BlockSpec((d, tn), lambda i, j: (0, jnp.maximum(j - nh, 0))),
                  pl.BlockSpec((8 * H_DN, d), lambda i, j: (0, 0))],
        out_specs=[pl.BlockSpec((tm, tn), lambda i, j: (i, j)),
                   pl.BlockSpec((8 * H_DN, tm), lambda i, j: (0, i))],
        out_shape=[jax.ShapeDtypeStruct((t, nm), BF16),
                   jax.ShapeDtypeStruct((8 * H_DN, t), F32)],
        scratch_shapes=[pltpu.VMEM((tm, d), BF16)],
        compiler_params=_cparams(("parallel", "arbitrary")),
        name="inproj",
    )(xc, xl, mod8, norm_g.reshape(1, d), w_all, w_tail, w_gate_t)


def _merge_kernel(a_ref, b_ref, wa_ref, wb_ref, ga_ref, gb_ref, o_ref):
    ya = _dot(a_ref[...], wa_ref[...])
    yb = _dot(b_ref[...], wb_ref[...])
    ga = jax.nn.sigmoid(ga_ref[...].astype(F32))
    gb = jax.nn.sigmoid(gb_ref[...].astype(F32))
    o_ref[...] = (ga * ya + gb * yb).astype(o_ref.dtype)


def _merge(ret_out, dn_out, w_a, w_b, proj, d, tm):
    t = ret_out.shape[0]
    tn = _pick_tile(d, (512, 256, 128))
    ga0 = BG0 // tn
    gb0 = (BG0 + d) // tn
    return pl.pallas_call(
        _merge_kernel,
        grid=(t // tm, d // tn),
        in_specs=[pl.BlockSpec((tm, RET_V), lambda i, j: (i, 0)),
                  pl.BlockSpec((tm, DN_V), lambda i, j: (i, 0)),
                  pl.BlockSpec((RET_V, tn), lambda i, j: (0, j)),
                  pl.BlockSpec((DN_V, tn), lambda i, j: (0, j)),
                  pl.BlockSpec((tm, tn), lambda i, j: (i, ga0 + j)),
                  pl.BlockSpec((tm, tn), lambda i, j: (i, gb0 + j))],
        out_specs=pl.BlockSpec((tm, tn), lambda i, j: (i, j)),
        out_shape=jax.ShapeDtypeStruct((t, d), BF16),
        compiler_params=_cparams(("parallel", "arbitrary")),
        name="merge",
    )(ret_out, dn_out, w_a, w_b, proj, proj)


def _outproj_kernel(m_ref, w_ref, xc_ref, xl_ref, mod_ref, o_ref, *, n_ctx_blocks):
    y = _dot(m_ref[...], w_ref[...])
    x = jnp.where(pl.program_id(0) < n_ctx_blocks, xc_ref[...], xl_ref[...])
    o_ref[...] = x + mod_ref[2:3, :] * y


def _outproj(merged, w_out, xc, xl, mod8, seg_of_block, tm):
    t, d = merged.shape
    tn = _pick_tile(d, (512, 256, 128))
    nc = xc.shape[0] // tm
    return pl.pallas_call(
        functools.partial(_outproj_kernel, n_ctx_blocks=nc),
        grid=(t // tm, d // tn),
        in_specs=[pl.BlockSpec((tm, d), lambda i, j: (i, 0)),
                  pl.BlockSpec((d, tn), lambda i, j: (0, j)),
                  pl.BlockSpec((tm, tn), lambda i, j: (jnp.minimum(i, nc - 1), jnp.where(i < nc, j, 0))),
                  pl.BlockSpec((tm, tn), lambda i, j: (jnp.maximum(i - nc, 0), jnp.where(i < nc, 0, j))),
                  pl.BlockSpec((None, 8, tn), lambda i, j: (seg_of_block(i), 0, j))],
        out_specs=pl.BlockSpec((tm, tn), lambda i, j: (i, j)),
        out_shape=jax.ShapeDtypeStruct((t, d), F32),
        compiler_params=_cparams(("parallel", "arbitrary")),
        name="outproj",
    )(merged, w_out, xc, xl, mod8)


def _peerq_kernel(x_ref, mod_ref, g_ref, w_ref, h_ref, q_ref, h_scr):
    @pl.when(pl.program_id(1) == 0)
    def _():
        h = _modulated_norm(x_ref[...], g_ref[...], mod_ref[3:4, :], mod_ref[4:5, :]).astype(BF16)
        h_scr[...] = h
        h_ref[...] = h

    q = _dot(h_scr[...], w_ref[...]).astype(BF16)
    hq = q_ref.shape[2]
    for i in range(q_ref.shape[0]):
        q_ref[i] = q[:, i * hq:(i + 1) * hq]


def _peerq(x1, mod8, seg_of_block, tm, norm_g, w_q):
    t, d = x1.shape
    nq = w_q.shape[1]
    hq = PEER_QDIM // 2
    tn = _pick_tile(nq, (512, 256, 128))
    return pl.pallas_call(
        _peerq_kernel,
        grid=(t // tm, nq // tn),
        in_specs=[pl.BlockSpec((tm, d), lambda i, j: (i, 0)),
                  pl.BlockSpec((None, 8, d), lambda i, j: (seg_of_block(i), 0, 0)),
                  pl.BlockSpec((1, d), lambda i, j: (0, 0)),
                  pl.BlockSpec((d, tn), lambda i, j: (0, j))],
        out_specs=[pl.BlockSpec((tm, d), lambda i, j: (i, 0)),
                   pl.BlockSpec((tn // hq, tm, hq), lambda i, j: (j, i, 0))],
        out_shape=[jax.ShapeDtypeStruct((t, d), BF16),
                   jax.ShapeDtypeStruct((nq // hq, t, hq), BF16)],
        scratch_shapes=[pltpu.VMEM((tm, d), BF16)],
        compiler_params=_cparams(("parallel", "arbitrary")),
        name="peer_query",
    )(x1, mod8, norm_g.reshape(1, d), w_q)


def _ret_kernel(lg_ref, q_ref, k_ref, v_ref, g_ref, gn_ref, *rest, n_chunks, has_init, has_dst, emit_state):
    rest = list(rest)
    s0_ref = rest.pop(0) if has_init else None
    if has_dst:
        rest.pop(0)
    o_ref = rest.pop(0)
    st_ref = rest.pop(0) if emit_state else None
    dmat_scr, vec_scr = rest[:2]
    if n_chunks > 1:
        o_scr, sf_scr, sb_scr = rest[2:]
    c = RET_CHUNK
    scale = DK_RET ** -0.5
    h = pl.program_id(0)
    lgf = lg_ref[0, h]
    lgb = lg_ref[1, h]

    @pl.when(pl.program_id(1) == 0)
    def _():
        row = lax.broadcasted_iota(jnp.int32, (c, c), 0)
        col = lax.broadcasted_iota(jnp.int32, (c, c), 1)
        dlt = (row - col).astype(F32)
        dmat_scr[...] = (jnp.where(dlt >= 0, jnp.exp(lgf * jnp.maximum(dlt, 0.0)), 0.0)
                         + jnp.where(dlt <= 0, jnp.exp(lgb * jnp.maximum(-dlt, 0.0)), 0.0)) * scale
        pos = lax.broadcasted_iota(jnp.int32, (c, DK_RET), 0).astype(F32)
        vec_scr[0] = jnp.exp(lgf * (pos + 1.0))
        vec_scr[1] = jnp.exp(lgb * (c - pos))
        vec_scr[2] = jnp.exp(lgf * (c - 1.0 - pos)) * scale
        vec_scr[3] = jnp.exp(lgb * pos) * scale

    dmat = dmat_scr[...]
    xi_f = vec_scr[0]
    xi_b = vec_scr[1]
    zeta_f = vec_scr[2]
    zeta_b = vec_scr[3]
    dec_f = jnp.exp(jnp.full((1, DV_RET), lgf, F32) * float(c))
    dec_b = jnp.exp(jnp.full((1, DV_RET), lgb, F32) * float(c))

    def rows(i):
        return pl.ds(pl.multiple_of(i * c, c), c)

    def intra(r):
        q = q_ref[r, :]
        k = k_ref[r, :]
        v = v_ref[r, :]
        s = _nt_dot(q, k) * dmat
        return q.astype(F32), k.astype(F32), v, _dot(s.astype(BF16), v)

    def state_inc(kf, zeta, v):
        return _dot((kf * zeta).T.astype(BF16), v)

    def finish(o, r):
        mu = jnp.mean(o, axis=-1, keepdims=True)
        oc = o - mu
        var = jnp.mean(oc * oc, axis=-1, keepdims=True)
        on = oc * lax.rsqrt(var + EPS)
        o_ref[r, :] = (on * gn_ref[...] * _silu(g_ref[r, :].astype(F32))).astype(o_ref.dtype)

    if n_chunks == 1:
        r = pl.ds(0, c)
        qf, kf, v, o = intra(r)
        s_f = state_inc(kf, zeta_f, v)
        s_b = state_inc(kf, zeta_b, v)
        if has_init:
            s0f = s0_ref[0]
            s0b = s0_ref[1]
            o = o + _dot((qf * xi_f).astype(BF16), s0f.astype(BF16))
            o = o + _dot((qf * xi_b).astype(BF16), s0b.astype(BF16))
            s_f = s_f + dec_f * s0f
            s_b = s_b + dec_b * s0b
        finish(o, r)
        if emit_state:
            st_ref[0] = s_f
            st_ref[1] = s_b
        return

    if has_init:
        sf_scr[...] = s0_ref[0]
        sb_scr[...] = s0_ref[1]
    else:
        sf_scr[...] = jnp.zeros_like(sf_scr)
        sb_scr[...] = jnp.zeros_like(sb_scr)

    def fwd(i, carry):
        r = rows(i)
        qf, kf, v, o = intra(r)
        s = sf_scr[...]
        o_scr[r, :] = o + _dot((qf * xi_f).astype(BF16), s.astype(BF16))
        sf_scr[...] = dec_f * s + state_inc(kf, zeta_f, v)
        return carry

    lax.fori_loop(0, n_chunks, fwd, 0)

    def bwd(ii, carry):
        i = n_chunks - 1 - ii
        r = rows(i)
        qf = q_ref[r, :].astype(F32)
        kf = k_ref[r, :].astype(F32)
        v = v_ref[r, :]
        s = sb_scr[...]
        o = o_scr[r, :] + _dot((qf * xi_b).astype(BF16), s.astype(BF16))
        sb_scr[...] = dec_b * s + state_inc(kf, zeta_b, v)
        finish(o, r)
        return carry

    lax.fori_loop(0, n_chunks, bwd, 0)
    if emit_state:
        st_ref[0] = sf_scr[...]
        st_ref[1] = sb_scr[...]


def _retention(proj, lg, ret_gn, s0, dst, *, n_batch, seq, row0, emit_state):
    n_chunks = seq // RET_CHUNK
    rb0 = row0 // seq
    has_init = s0 is not None
    aliases = {}
    in_specs = [pl.BlockSpec(memory_space=pltpu.SMEM),
                pl.BlockSpec((seq, DK_RET), lambda h, b: (rb0 + b, RQ0 // DK_RET + h)),
                pl.BlockSpec((seq, DK_RET), lambda h, b: (rb0 + b, RK0 // DK_RET + h)),
                pl.BlockSpec((seq, DV_RET), lambda h, b: (rb0 + b, RV0 // DV_RET + h)),
                pl.BlockSpec((seq, DV_RET), lambda h, b: (rb0 + b, RG0 // DV_RET + h)),
                pl.BlockSpec((1, DV_RET), lambda h, b: (0, h))]
    args = [lg, proj, proj, proj, proj, ret_gn.reshape(1, RET_V)]
    if has_init:
        in_specs.append(pl.BlockSpec((None, 2, None, DK_RET, DV_RET), lambda h, b: (b, 0, h, 0, 0)))
        args.append(s0)
    if dst is not None:
        in_specs.append(pl.BlockSpec(memory_space=pl.ANY))
        aliases[len(args)] = 0
        args.append(dst)
    out_specs = [pl.BlockSpec((seq, DV_RET), lambda h, b: (rb0 + b, h))]
    out_shape = [jax.ShapeDtypeStruct((proj.shape[0], RET_V), BF16)]
    if emit_state:
        out_specs.append(pl.BlockSpec((None, 2, None, DK_RET, DV_RET), lambda h, b: (b, 0, h, 0, 0)))
        out_shape.append(jax.ShapeDtypeStruct((n_batch, 2, H_RET, DK_RET, DV_RET), F32))
    scratch = [pltpu.VMEM((RET_CHUNK, RET_CHUNK), F32), pltpu.VMEM((4, RET_CHUNK, DK_RET), F32)]
    if n_chunks > 1:
        scratch += [pltpu.VMEM((seq, DV_RET), F32), pltpu.VMEM((DK_RET, DV_RET), F32),
                    pltpu.VMEM((DK_RET, DV_RET), F32)]
    return pl.pallas_call(
        functools.partial(_ret_kernel, n_chunks=n_chunks, has_init=has_init, has_dst=dst is not None,
                          emit_state=emit_state),
        grid=(H_RET, n_batch),
        in_specs=in_specs,
        out_specs=out_specs,
        out_shape=out_shape,
        input_output_aliases=aliases,
        scratch_shapes=scratch,
        compiler_params=_cparams(("parallel", "arbitrary")),
        name="retention_ctx" if emit_state else "retention_lat",
    )(*args)


def _dn_kernel(par_ref, q_ref, k_ref, v_ref, z_ref, gt_ref, cq_ref, ck_ref, cv_ref, nrm_ref, *rest,
               seq, row_len, heads, has_init, has_dst, emit_state):
    rest = list(rest)
    s0_ref = rest.pop(0) if has_init else None
    if has_dst:
        rest.pop(0)
    o_ref = rest.pop(0)
    st_ref = rest.pop(0) if emit_state else None
    rows_scr, o_scr, s_scr, qkv_scr = rest
    cb = DN_CHUNK
    n_blocks = seq // cb
    hg = pl.program_id(1)

    lane = lax.broadcasted_iota(jnp.int32, (1, seq), 1) & (cb - 1)

    def splat(v):
        return jnp.full((1, seq), v, F32)

    def prefix(x):
        s = 1
        while s < cb:
            x = x + jnp.where(lane >= s, pltpu.roll(x, s, 1), 0.0)
            s *= 2
        return x

    def suffix(x):
        s = 1
        while s < cb:
            x = x + jnp.where(lane < cb - s, pltpu.roll(x, seq - s, 1), 0.0)
            s *= 2
        return x

    for g in range(heads):
        h = hg * heads + g
        r0 = 8 * g
        g_f = (-jnp.exp(splat(par_ref[0, h]))
               * jax.nn.softplus(gt_ref[r0 + 2:r0 + 3, :] + splat(par_ref[2, h])))
        g_b = (-jnp.exp(splat(par_ref[1, h]))
               * jax.nn.softplus(gt_ref[r0 + 3:r0 + 4, :] + splat(par_ref[3, h])))
        rows_scr[r0:r0 + 1, :] = jax.nn.sigmoid(gt_ref[r0:r0 + 1, :])
        rows_scr[r0 + 1:r0 + 2, :] = prefix(g_f)
        rows_scr[r0 + 2:r0 + 3, :] = suffix(g_f) - g_f
        rows_scr[r0 + 3:r0 + 4, :] = jax.nn.sigmoid(gt_ref[r0 + 1:r0 + 2, :])
        rows_scr[r0 + 4:r0 + 5, :] = suffix(g_b)
        rows_scr[r0 + 5:r0 + 6, :] = prefix(g_b) - g_b
        rows_scr[r0 + 6:r0 + 8, :] = jnp.zeros((2, seq), F32)
        for direction in range(2):
            if has_init:
                s_scr[g, direction] = s0_ref[direction, g]
            else:
                s_scr[g, direction] = jnp.zeros((DK_DN, DV_DN), F32)

    ri = lax.broadcasted_iota(jnp.int32, (cb, cb), 0)
    ci = lax.broadcasted_iota(jnp.int32, (cb, cb), 1)
    eye = (ri == ci).astype(F32)
    pair_masks = []
    sz = 2
    while sz < cb:
        pair = ((ri // (2 * sz)) == (ci // (2 * sz))) & ((ri // sz) != (ci // sz))
        pair_masks.append(jnp.where(pair, 1.0, 0.0).astype(BF16))
        sz *= 2
    n_small = (DN_SLAB // 2).bit_length() - 1
    same_slab = (ri // DN_SLAB) == (ci // DN_SLAB)
    slab_mask16 = jnp.where(same_slab, 1.0, 0.0).astype(BF16)
    tpos = lax.broadcasted_iota(jnp.int32, (cb, DK_DN), 0) & (row_len - 1)

    def conv(x, w_ref, lanes):
        acc = x * w_ref[CONV_K // 2:CONV_K // 2 + 1, lanes]
        for i in range(CONV_K):
            d = i - CONV_K // 2
            if d == 0:
                continue
            sh = pltpu.roll(x, (-d) % cb, 0)
            ok = (tpos + d >= 0) & (tpos + d < row_len)
            acc = acc + jnp.where(ok, sh, 0.0) * w_ref[i:i + 1, lanes]
        return _silu(acc)

    def l2n(x):
        return x * lax.rsqrt(jnp.sum(x * x, axis=-1, keepdims=True) + EPS)

    def col_of(row):
        parts = []
        for t in range(cb // LANES):
            parts.append(jnp.broadcast_to(row[:, t * LANES:(t + 1) * LANES], (LANES, LANES)).T)
        return jnp.concatenate(parts, axis=0)

    def rows_of(bi):
        return pl.ds(pl.multiple_of(bi * cb, cb), cb)

    def prepare(bi, carry):
        r = rows_of(bi)
        for g in range(heads):
            lanes = slice(g * DK_DN, (g + 1) * DK_DN)
            qkv_scr[g, 0, r, :] = l2n(conv(q_ref[r, lanes].astype(F32), cq_ref, lanes)) * (DK_DN ** -0.5)
            qkv_scr[g, 1, r, :] = l2n(conv(k_ref[r, lanes].astype(F32), ck_ref, lanes))
            qkv_scr[g, 2, r, :] = conv(v_ref[r, lanes].astype(F32), cv_ref, lanes)
        return carry

    if n_blocks == 1:
        prepare(0, 0)
    else:
        lax.fori_loop(0, n_blocks, prepare, 0)

    def blocks(chains):
        cs = []
        for g, bi, direction in chains:
            r = rows_of(bi)
            base = 8 * g + (0 if direction == 0 else 3)
            cs.append(dict(g=g, d=direction, r=r, beta_row=rows_scr[base:base + 1, r],
                           g_row=rows_scr[base + 1:base + 2, r], e_row=rows_scr[base + 2:base + 3, r]))
        for c in cs:
            c["q"] = qkv_scr[c["g"], 0, c["r"], :]
            c["k"] = qkv_scr[c["g"], 1, c["r"], :]
            c["v"] = qkv_scr[c["g"], 2, c["r"], :]
            c["k16"] = c["k"].astype(BF16)
            c["kk"] = _nt_dot(c["k16"], c["k16"])
        for c in cs:
            c["beta_c"] = col_of(c["beta_row"])
            c["g_c"] = col_of(c["g_row"])
            c["incl"] = (ri >= ci) if c["d"] == 0 else (ri <= ci)
            strict = (ri > ci) if c["d"] == 0 else (ri < ci)
            g_cw = jnp.concatenate([c["g_c"]] * (cb // LANES), axis=1)
            beta_cw = jnp.concatenate([c["beta_c"]] * (cb // LANES), axis=1)
            c["lmat"] = jnp.exp(jnp.where(c["incl"], g_cw - c["g_row"], NEG_INF))
            c["m"] = jnp.where(strict, c["kk"] * beta_cw * c["lmat"], 0.0)
            c["x"] = eye - jnp.where((ri // 2) == (ci // 2), c["m"], 0.0)
        n_slab = cb // DN_SLAB
        for c in cs:
            c["m16"] = c["m"].astype(BF16)
            c["xc"] = functools.reduce(
                lambda a, b: a + b, [c["x"][p * DN_SLAB:(p + 1) * DN_SLAB, :] for p in range(n_slab)])
        for pair16 in pair_masks[:n_small]:
            for c in cs:
                xc16 = c["xc"].astype(BF16)
                c["xbd16"] = jnp.concatenate([xc16] * n_slab, axis=0) * slab_mask16
                c["t16"] = _dot(xc16, c["m16"] * pair16).astype(BF16)
            for c in cs:
                c["xc"] = c["xc"] - _dot(c["t16"], c["xbd16"])
        for c in cs:
            c["x"] = jnp.where(same_slab, jnp.concatenate([c["xc"]] * n_slab, axis=0), 0.0)
        for pair16 in pair_masks[n_small:]:
            for c in cs:
                c["x16"] = c["x"].astype(BF16)
                c["t16"] = _dot(c["x16"], c["m16"] * pair16).astype(BF16)
            for c in cs:
                c["x"] = c["x"] - _dot(c["t16"], c["x16"])
        for c in cs:
            eg_c = jnp.exp(c["g_c"])
            rhs = jnp.concatenate([c["v"] * c["beta_c"], c["k"] * (c["beta_c"] * eg_c)], axis=1).astype(BF16)
            uw = _dot(c["x"].astype(BF16), rhs)
            c["u"] = uw[:, :DV_DN]
            c["w16"] = uw[:, DV_DN:].astype(BF16)
            c["qg16"] = (c["q"] * eg_c).astype(BF16)
            c["aqk"] = (_nt_dot(c["q"].astype(BF16), c["k16"]) * c["lmat"]).astype(BF16)
            c["kdec_t"] = (c["k"] * jnp.exp(col_of(c["e_row"]))).T.astype(BF16)
        outs = []
        for c in cs:
            s = s_scr[c["g"], c["d"]]
            s16 = s.astype(BF16)
            vn16 = (c["u"] - _dot(c["w16"], s16)).astype(BF16)
            outs.append(_dot(c["qg16"], s16) + _dot(c["aqk"], vn16))
            last = cb - 1 if c["d"] == 0 else 0
            s_scr[c["g"], c["d"]] = s * jnp.exp(c["g_c"][last:last + 1, :]) + _dot(c["kdec_t"], vn16)
        return outs

    def finish(g, bi, d):
        r = rows_of(bi)
        lanes = slice(g * DV_DN, (g + 1) * DV_DN)
        d = d * lax.rsqrt(jnp.mean(d * d, axis=-1, keepdims=True) + EPS)
        o_ref[r, lanes] = (d * nrm_ref[...] * _silu(z_ref[r, lanes].astype(F32))).astype(o_ref.dtype)

    if n_blocks == 1:
        outs = blocks([(g, 0, d) for g in range(heads) for d in range(2)])
        for g in range(heads):
            finish(g, 0, outs[2 * g] + outs[2 * g + 1])
    else:
        per = 2 if n_blocks % 4 == 0 else 1

        def chains_of(it):
            return [(g, per * it + j if d == 0 else n_blocks - 1 - per * it - j, d)
                    for g in range(heads) for d in range(2) for j in range(per)]

        def first_half(it, carry):
            chains = chains_of(it)
            for (g, bi, _), out in zip(chains, blocks(chains)):
                o_scr[g, rows_of(bi), :] = out
            return carry

        def second_half(it, carry):
            chains = chains_of(it)
            for (g, bi, _), out in zip(chains, blocks(chains)):
                finish(g, bi, out + o_scr[g, rows_of(bi), :])
            return carry

        n_it = n_blocks // per
        lax.fori_loop(0, n_it // 2, first_half, 0)
        lax.fori_loop(n_it // 2, n_it, second_half, 0)
    if emit_state:
        for g in range(heads):
            st_ref[0, g] = s_scr[g, 0]
            st_ref[1, g] = s_scr[g, 1]


def _deltanet(proj, gates_t, dn_par, dn_conv, dn_norm, s0, dst, *, n_batch, seq, row0, row_len, heads,
              emit_state):
    rb0 = row0 // seq
    has_init = s0 is not None
    aliases = {}
    wd = heads * DK_DN
    in_specs = [pl.BlockSpec(memory_space=pltpu.SMEM),
                pl.BlockSpec((seq, wd), lambda b, h: (rb0 + b, DQ0 // wd + h)),
                pl.BlockSpec((seq, wd), lambda b, h: (rb0 + b, DK0 // wd + h)),
                pl.BlockSpec((seq, wd), lambda b, h: (rb0 + b, DV0 // wd + h)),
                pl.BlockSpec((seq, wd), lambda b, h: (rb0 + b, DZ0 // wd + h)),
                pl.BlockSpec((8 * heads, seq), lambda b, h: (h, rb0 + b)),
                pl.BlockSpec((CONV_K, wd), lambda b, h: (0, h)),
                pl.BlockSpec((CONV_K, wd), lambda b, h: (0, DN_QK // wd + h)),
                pl.BlockSpec((CONV_K, wd), lambda b, h: (0, 2 * DN_QK // wd + h)),
                pl.BlockSpec((1, DV_DN), lambda b, h: (0, 0))]
    args = [dn_par, proj, proj, proj, proj, gates_t, dn_conv, dn_conv, dn_conv, dn_norm.reshape(1, DV_DN)]
    if has_init:
        in_specs.append(pl.BlockSpec((None, 2, heads, DK_DN, DV_DN), lambda b, h: (b, 0, h, 0, 0)))
        args.append(s0)
    if dst is not None:
        in_specs.append(pl.BlockSpec(memory_space=pl.ANY))
        aliases[len(args)] = 0
        args.append(dst)
    out_specs = [pl.BlockSpec((seq, wd), lambda b, h: (rb0 + b, h))]
    out_shape = [jax.ShapeDtypeStruct((proj.shape[0], DN_V), BF16)]
    if emit_state:
        out_specs.append(pl.BlockSpec((None, 2, heads, DK_DN, DV_DN), lambda b, h: (b, 0, h, 0, 0)))
        out_shape.append(jax.ShapeDtypeStruct((n_batch, 2, H_DN, DK_DN, DV_DN), F32))
    scratch = [pltpu.VMEM((8 * heads, seq), F32), pltpu.VMEM((heads, seq, DV_DN), F32),
               pltpu.VMEM((heads, 2, DK_DN, DV_DN), F32), pltpu.VMEM((heads, 3, seq, DK_DN), F32)]
    n_blocks = seq // DN_CHUNK
    assert seq % DN_CHUNK == 0 and (n_blocks == 1 or n_blocks % 2 == 0)
    assert DN_CHUNK % row_len == 0 and row_len & (row_len - 1) == 0 and H_DN % heads == 0
    return pl.pallas_call(
        functools.partial(_dn_kernel, seq=seq, row_len=row_len, heads=heads, has_init=has_init,
                          has_dst=dst is not None, emit_state=emit_state),
        grid=(n_batch, H_DN // heads),
        in_specs=in_specs,
        out_specs=out_specs,
        out_shape=out_shape,
        input_output_aliases=aliases,
        scratch_shapes=scratch,
        compiler_params=_cparams(("parallel", "arbitrary")),
        name="deltanet_ctx" if emit_state else "deltanet_lat",
    )(*args)


def _cand_pairs():
    n = PEER_TOPK + 1
    return [(a, b) for a in range(n) for b in range(n) if (a + 1) * (b + 1) <= n]


def _peer_score_kernel(q_ref, sk_ref, pb_ref, pf_ref, s_scr, sv_scr, gp_scr):
    big_rank = float(N_KEYS)

    def per_head(h, carry):
        for p in range(2):
            s = _nt_dot(sk_ref[p], q_ref[2 * h + p])
            s_scr[p, h] = s
            cur = s
            rank = jnp.full(s.shape, big_rank, F32)
            for r in range(PEER_TOPK + 1):
                m = jnp.max(cur, axis=0, keepdims=True)
                sv_scr[p, r, pl.ds(h, 1), :] = m
                if p == 1:
                    rank = jnp.where(cur == m, float(r), rank)
                cur = jnp.where(cur == m, NEG_INF, cur)
            if p == 1:
                pb_ref[0, h] = rank.astype(BF16)
        return carry

    lax.fori_loop(0, PEER_HEADS, per_head, 0)
    cands = [sv_scr[0, a] + sv_scr[1, b] for a, b in _cand_pairs()]
    mx = cands[0]
    z = jnp.zeros_like(mx)
    for r in range(PEER_TOPK):
        m = functools.reduce(jnp.maximum, cands)
        z = z + jnp.exp(m - mx)
        cands = [jnp.where(c == m, NEG_INF, c) for c in cands]
    runner_up = functools.reduce(jnp.maximum, cands)
    gp_scr[0] = 0.5 * (m + runner_up)
    gp_scr[1] = 1.0 / z

    def finish_head(h, carry):
        row = pl.ds(h, 1)
        s1 = s_scr[0, h]
        pb_ref[1, h] = jnp.exp(s_scr[1, h] - sv_scr[1, 0, row, :]).astype(BF16)
        pf_ref[0, h] = jnp.exp(s1 - sv_scr[0, 0, row, :]) * gp_scr[1, row, :]
        need = gp_scr[0, row, :] - s1
        cnt = jnp.zeros_like(need)
        for r in range(PEER_TOPK + 1):
            cnt = cnt + jnp.where(sv_scr[1, r, row, :] >= need, 1.0, 0.0)
        pf_ref[1, h] = cnt
        return carry

    lax.fori_loop(0, PEER_HEADS, finish_head, 0)


def _peer_scores(q3, subkeys, tb):
    nhp, t, _ = q3.shape
    return pl.pallas_call(
        _peer_score_kernel,
        grid=(t // tb,),
        in_specs=[pl.BlockSpec((nhp, tb, PEER_QDIM // 2), lambda i: (0, i, 0)),
                  pl.BlockSpec((2, N_KEYS, PEER_QDIM // 2), lambda i: (0, 0, 0))],
        out_specs=[pl.BlockSpec((2, PEER_HEADS, N_KEYS, tb), lambda i: (0, 0, 0, i)),
                   pl.BlockSpec((2, PEER_HEADS, N_KEYS, tb), lambda i: (0, 0, 0, i))],
        out_shape=[jax.ShapeDtypeStruct((2, PEER_HEADS, N_KEYS, t), BF16),
                   jax.ShapeDtypeStruct((2, PEER_HEADS, N_KEYS, t), F32)],
        scratch_shapes=[pltpu.VMEM((2, PEER_HEADS, N_KEYS, tb), F32),
                        pltpu.VMEM((2, PEER_TOPK + 1, PEER_HEADS, tb), F32),
                        pltpu.VMEM((2, PEER_HEADS, tb), F32)],
        compiler_params=_cparams(("parallel",)),
        name="peer_scores",
    )(q3, subkeys)


def _gelu_tanh(x):
    return 0.5 * x * (1.0 + jnp.tanh(0.7978845608028654 * (x + 0.044715 * (x * x * x))))


def _peer_dense_kernel(h_ref, pb_ref, pf_ref, u_ref, vt_ref, o_ref, *, n_i1):
    e = pl.program_id(1)
    zero = jnp.zeros((), BF16)
    tb, d = h_ref.shape
    kt = V7X_MXU_DIM
    per_m = kt // N_KEYS

    @pl.when(e == 0)
    def _():
        o_ref[...] = jnp.zeros_like(o_ref)

    def gate_head(ii, h):
        row = pl.ds(e * n_i1 + ii, 1)

        def bcast(x):
            x = jnp.broadcast_to(x, (16, x.shape[1])).astype(BF16)
            return jnp.concatenate([x] * (N_KEYS // 16), axis=0)

        e1 = bcast(pf_ref[0, h, row, :])
        cnt = bcast(pf_ref[1, h, row, :])
        return jnp.where(pb_ref[0, h] < cnt, pb_ref[1, h] * e1, zero)

    wa_parts = []
    for m in range(n_i1 // per_m):
        todo = [(per_m * m + j, h) for j in range(per_m) for h in range(PEER_HEADS)]
        gate = {}

        def one_gate():
            ii, h = todo.pop(0)
            g = gate_head(ii, h)
            gate[ii] = g if h == 0 else gate[ii] + g

        accs = [None] * (tb // kt)
        for k in range(d // kt):
            for n in range(tb // kt):
                part = _nt_dot(u_ref[m * kt:(m + 1) * kt, k * kt:(k + 1) * kt],
                               h_ref[n * kt:(n + 1) * kt, k * kt:(k + 1) * kt])
                accs[n] = part if accs[n] is None else accs[n] + part
                if todo:
                    one_gate()
        while todo:
            one_gate()
        act = jnp.concatenate([_gelu_tanh(a).astype(BF16) for a in accs], axis=1)
        w = jnp.concatenate([gate[per_m * m + j] for j in range(per_m)], axis=0)
        wa_parts.append(w * act)
    o_ref[...] += _dot(vt_ref[...], jnp.concatenate(wa_parts, axis=0))


def _peer_dense(h2, pb, pf, u_tab, v_tab_t, tb, et):
    t, d = h2.shape
    ne = u_tab.shape[0]
    once = pl.Buffered(1)
    return pl.pallas_call(
        functools.partial(_peer_dense_kernel, n_i1=et // N_KEYS),
        grid=(t // tb, ne // et),
        in_specs=[pl.BlockSpec((tb, d), lambda i, e: (i, 0), pipeline_mode=once),
                  pl.BlockSpec((2, PEER_HEADS, N_KEYS, tb), lambda i, e: (0, 0, 0, i), pipeline_mode=once),
                  pl.BlockSpec((2, PEER_HEADS, N_KEYS, tb), lambda i, e: (0, 0, 0, i), pipeline_mode=once),
                  pl.BlockSpec((et, d), lambda i, e: (e, 0)),
                  pl.BlockSpec((d, et), lambda i, e: (0, e))],
        out_specs=pl.BlockSpec((d, tb), lambda i, e: (0, i)),
        out_shape=jax.ShapeDtypeStruct((d, t), F32),
        compiler_params=_cparams(("parallel", "arbitrary")),
        name="peer_dense",
    )(h2, pb, pf, u_tab, v_tab_t)


def _final_kernel(x_ref, p_ref, mod_ref, g_ref, oc_ref, ol_ref, *, n_ctx_blocks):
    x = x_ref[...] + mod_ref[5:6, :] * p_ref[...].T
    ms = jnp.mean(x * x, axis=-1, keepdims=True)
    y = (x * lax.rsqrt(ms + EPS)) * g_ref[...]
    i = pl.program_id(0)

    @pl.when(i < n_ctx_blocks)
    def _():
        oc_ref[...] = y

    @pl.when(i >= n_ctx_blocks)
    def _():
        ol_ref[...] = y


def _final(x1, peer_t, mod8, seg_of_block, tm_seg, final_norm, tc):
    t, d = x1.shape
    tm = FINAL_TILE
    sub = tm_seg // tm
    nc = tc // tm
    return pl.pallas_call(
        functools.partial(_final_kernel, n_ctx_blocks=nc),
        grid=(t // tm,),
        in_specs=[pl.BlockSpec((tm, d), lambda i: (i, 0)),
                  pl.BlockSpec((d, tm), lambda i: (0, i)),
                  pl.BlockSpec((None, 8, d), lambda i: (seg_of_block(i // sub), 0, 0)),
                  pl.BlockSpec((1, d), lambda i: (0, 0))],
        out_specs=[pl.BlockSpec((tm, d), lambda i: (jnp.minimum(i, nc - 1), 0)),
                   pl.BlockSpec((tm, d), lambda i: (jnp.maximum(i - nc, 0), 0))],
        out_shape=[jax.ShapeDtypeStruct((tc, d), F32), jax.ShapeDtypeStruct((t - tc, d), F32)],
        compiler_params=_cparams(("arbitrary",)),
        name="final_norm",
    )(x1, peer_t, mod8, final_norm.reshape(1, d))


def kernel(x_prompt, x_sample, state_ret, state_dn, c, c_ctx, w_ada, b_ada, norm_mix, norm_ffn, w_in, ret_logit,
           ret_gn, dn_conv, dn_a_log, dn_dt_bias, dn_norm, w_br_a, w_br_b, w_out, peer_wq, peer_subkeys, peer_u,
           peer_v, final_norm):
    bc, lc, d = x_prompt.shape
    bl, ll, _ = x_sample.shape
    depth = w_ada.shape[0]
    tc = bc * lc
    t = tc + bl * ll
    assert bl + 1 <= 8 and tc % ll == 0

    tm = _pick_tile(math.gcd(tc, ll), (1024, 512, 256))
    n_ctx_blocks = tc // tm
    lat_blocks = ll // tm

    def seg_of_block(i):
        return jnp.where(i < n_ctx_blocks, 0, 1 + (i - n_ctx_blocks) // lat_blocks)

    xc = x_prompt.reshape(tc, d)
    xl = x_sample.reshape(bl * ll, d)
    cvec = jnp.concatenate([c_ctx[None, :], c, jnp.zeros((8 - 1 - bl, d), F32)], axis=0)

    ret_states, dn_states = [], []
    for l in range(depth):
        mod = _adaln(cvec, w_ada[l], b_ada[l])
        mod8 = jnp.pad(mod.reshape(8, 6, d), ((0, 0), (0, 2), (0, 0)))

        w = w_in[l].astype(BF16)
        wg = w[:, DGL0:DGL0 + 4 * H_DN].reshape(d, 4, H_DN).transpose(2, 1, 0)
        w_gate_t = jnp.pad(wg, ((0, 0), (0, 4), (0, 0))).reshape(8 * H_DN, d)

        proj, gates_t = _inproj(xc, xl, mod8, seg_of_block, tm, norm_mix[l], w, w[:, DGL0 + 4 * H_DN:], w_gate_t)

        lg = jax.nn.log_sigmoid(ret_logit[l].astype(F32))
        ret_c, rs = _retention(proj, lg, ret_gn[l], None, None, n_batch=bc, seq=lc, row0=0, emit_state=True)
        (ret_out,) = _retention(proj, lg, ret_gn[l], state_ret[:, l], ret_c, n_batch=bl, seq=ll, row0=tc,
                                emit_state=False)
        dn_par = jnp.concatenate([dn_a_log[l], dn_dt_bias[l]], axis=0).astype(F32)
        dn_c, ds = _deltanet(proj, gates_t, dn_par, dn_conv[l], dn_norm[l], None, None, n_batch=bc, seq=lc,
                             row0=0, row_len=lc, heads=4, emit_state=True)
        (dn_out,) = _deltanet(proj, gates_t, dn_par, dn_conv[l], dn_norm[l], state_dn[:, l], dn_c, n_batch=bl,
                              seq=ll, row0=tc, row_len=GRID_W, heads=2, emit_state=False)
        ret_states.append(rs)
        dn_states.append(ds)

        merged = _merge(ret_out, dn_out, w_br_a[l].astype(BF16), w_br_b[l].astype(BF16), proj, d, tm)
        x1 = _outproj(merged, w_out[l].astype(BF16), xc, xl, mod8, seg_of_block, tm)

        h2, q3 = _peerq(x1, mod8, seg_of_block, tm, norm_ffn[l], peer_wq[l].astype(BF16))
        tb = min(PEER_TOKEN_TILE, tm)
        pb, pf = _peer_scores(q3, peer_subkeys[l].astype(BF16), tb)
        peer_t = _peer_dense(h2, pb, pf, peer_u[l].astype(BF16), peer_v[l].astype(BF16).T, tb,
                             PEER_EXPERT_TILE)
        assert depth == 1

    y_c, y_l = _final(x1, peer_t, mod8, seg_of_block, tm, final_norm, tc)
    y_prompt = y_c.reshape(bc, lc, d)
    y_sample = y_l.reshape(bl, ll, d)
    new_state_ret = jnp.stack(ret_states, axis=1)
    new_state_dn = jnp.stack(dn_states, axis=1)
    return (y_prompt, y_sample, new_state_ret, new_state_dn)
```

```python
import functools
import math

import jax
import jax.numpy as jnp
from jax import lax
from jax.experimental import pallas as pl
from jax.experimental.pallas import tpu as pltpu

F32 = jnp.float32
BF16 = jnp.bfloat16

H_RET, DK_RET, DV_RET = 8, 128, 256
H_DN, DK_DN, DV_DN = 16, 128, 128
CONV_K = 5
GRID_W = 64
DN_CHUNK = 256
DN_SLAB = 64
RET_CHUNK = 256
N_KEYS = 128
PEER_HEADS = 8
PEER_QDIM = 256
PEER_TOPK = 16
EPS = 1e-6

RET_QK = H_RET * DK_RET
RET_V = H_RET * DV_RET
DN_QK = H_DN * DK_DN
DN_V = H_DN * DV_DN
RQ0 = 0
RK0 = RQ0 + RET_QK
RV0 = RK0 + RET_QK
RG0 = RV0 + RET_V
DQ0 = RG0 + RET_V
DK0 = DQ0 + DN_QK
DV0 = DK0 + DN_QK
DZ0 = DV0 + DN_V
DGL0 = DZ0 + DN_V
BG0 = DZ0 + DN_V

V7X_VMEM_BYTES = 64 * 1024 * 1024
VMEM_LIMIT = V7X_VMEM_BYTES - 8 * 1024 * 1024
V7X_MXU_DIM = 256
LANES = 128
PEER_TOKEN_TILE = 512
PEER_EXPERT_TILE = 1024
FINAL_TILE = 256
NEG_INF = float("-inf")


def _cparams(sem):
    return pltpu.CompilerParams(dimension_semantics=sem, vmem_limit_bytes=VMEM_LIMIT)


def _nt_dot(a, b):
    return lax.dot_general(a, b, (((1,), (1,)), ((), ())), preferred_element_type=F32)


def _dot(a, b):
    return jnp.dot(a, b, preferred_element_type=F32)


def _silu(x):
    h = 0.5 * x
    return h + h * jnp.tanh(h)


def _pick_tile(n, cands):
    for c in cands:
        if n % c == 0:
            return c
    raise ValueError(f"no tile for {n}")


def _adaln_kernel(c_ref, w_ref, b_ref, o_ref):
    a = _silu(c_ref[...])
    o_ref[...] = jnp.dot(a, w_ref[...], precision=lax.Precision.HIGHEST,
                         preferred_element_type=F32) + b_ref[...]


def _adaln(cvec, w_ada, b_ada):
    d, n = w_ada.shape
    tn = _pick_tile(n, (1024, 512, 256, 128))
    return pl.pallas_call(
        _adaln_kernel,
        grid=(n // tn,),
        in_specs=[pl.BlockSpec((8, d), lambda j: (0, 0)),
                  pl.BlockSpec((d, tn), lambda j: (0, j)),
                  pl.BlockSpec((1, tn), lambda j: (0, j))],
        out_specs=pl.BlockSpec((8, tn), lambda j: (0, j)),
        out_shape=jax.ShapeDtypeStruct((8, n), F32),
        compiler_params=_cparams(("parallel",)),
        name="adaln",
    )(cvec, w_ada, b_ada.reshape(1, n))


def _modulated_norm(x, g, shift, scale):
    ms = jnp.mean(x * x, axis=-1, keepdims=True)
    return (x * lax.rsqrt(ms + EPS)) * g * (1.0 + scale) + shift


def _inproj_kernel(xc_ref, xl_ref, mod_ref, g_ref, wa_ref, wb_ref, wg_ref, o_ref, gt_ref, h_scr, *,
                   n_ctx_blocks, n_head_tiles):
    i = pl.program_id(0)
    j = pl.program_id(1)

    def prepare(x_ref):
        h = _modulated_norm(x_ref[...], g_ref[...], mod_ref[0:1, :], mod_ref[1:2, :]).astype(BF16)
        h_scr[...] = h
        gt_ref[...] = _nt_dot(wg_ref[...], h)

    pl.when((j == 0) & (i < n_ctx_blocks))(functools.partial(prepare, xc_ref))
    pl.when((j == 0) & (i >= n_ctx_blocks))(functools.partial(prepare, xl_ref))

    @pl.when(j < n_head_tiles)
    def _():
        o_ref[...] = _dot(h_scr[...], wa_ref[...]).astype(o_ref.dtype)

    @pl.when(j >= n_head_tiles)
    def _():
        o_ref[...] = _dot(h_scr[...], wb_ref[...]).astype(o_ref.dtype)


def _inproj(xc, xl, mod8, seg_of_block, tm, norm_g, w_all, w_tail, w_gate_t):
    d = xc.shape[1]
    t = xc.shape[0] + xl.shape[0]
    nm = BG0 + w_tail.shape[1]
    tn = _pick_tile(math.gcd(BG0, w_tail.shape[1]), (1024, 512, 256))
    nc = xc.shape[0] // tm
    nh = BG0 // tn
    once = pl.Buffered(1)
    return pl.pallas_call(
        functools.partial(_inproj_kernel, n_ctx_blocks=nc, n_head_tiles=nh),
        grid=(t // tm, nm // tn),
        in_specs=[pl.BlockSpec((tm, d), lambda i, j: (jnp.minimum(i, nc - 1), 0), pipeline_mode=once),
                  pl.BlockSpec((tm, d), lambda i, j: (jnp.maximum(i - nc, 0), 0), pipeline_mode=once),
                  pl.BlockSpec((None, 8, d), lambda i, j: (seg_of_block(i), 0, 0)),
                  pl.BlockSpec((1, d), lambda i, j: (0, 0)),
                  pl.BlockSpec((d, tn), lambda i, j: (0, jnp.minimum(j, nh - 1))),
                  pl.BlockSpec((d, tn), lambda i, j: (0, jnp.maximum(j - nh, 0))),
                  pl.BlockSpec((8 * H_DN, d), lambda i, j: (0, 0))],
        out_specs=[pl.BlockSpec((tm, tn), lambda i, j: (i, j)),
                   pl.BlockSpec((8 * H_DN, tm), lambda i, j: (0, i))],
        out_shape=[jax.ShapeDtypeStruct((t, nm), BF16),
                   jax.ShapeDtypeStruct((8 * H_DN, t), F32)],
        scratch_shapes=[pltpu.VMEM((tm, d), BF16)],
        compiler_params=_cparams(("parallel", "arbitrary")),
        name="inproj",
    )(xc, xl, mod8, norm_g.reshape(1, d), w_all, w_tail, w_gate_t)


def _merge_kernel(a_ref, b_ref, wa_ref, wb_ref, ga_ref, gb_ref, o_ref):
    ya = _dot(a_ref[...], wa_ref[...])
    yb = _dot(b_ref[...], wb_ref[...])
    ga = jax.nn.sigmoid(ga_ref[...].astype(F32))
    gb = jax.nn.sigmoid(gb_ref[...].astype(F32))
    o_ref[...] = (ga * ya + gb * yb).astype(o_ref.dtype)


def _merge(ret_out, dn_out, w_a, w_b, proj, d, tm):
    t = ret_out.shape[0]
    tn = _pick_tile(d, (512, 256, 128))
    ga0 = BG0 // tn
    gb0 = (BG0 + d) // tn
    return pl.pallas_call(
        _merge_kernel,
        grid=(t // tm, d // tn),
        in_specs=[pl.BlockSpec((tm, RET_V), lambda i, j: (i, 0)),
                  pl.BlockSpec((tm, DN_V), lambda i, j: (i, 0)),
                  pl.BlockSpec((RET_V, tn), lambda i, j: (0, j)),
                  pl.BlockSpec((DN_V, tn), lambda i, j: (0, j)),
                  pl.BlockSpec((tm, tn), lambda i, j: (i, ga0 + j)),
                  pl.BlockSpec((tm, tn), lambda i, j: (i, gb0 + j))],
        out_specs=pl.BlockSpec((tm, tn), lambda i, j: (i, j)),
        out_shape=jax.ShapeDtypeStruct((t, d), BF16),
        compiler_params=_cparams(("parallel", "arbitrary")),
        name="merge",
    )(ret_out, dn_out, w_a, w_b, proj, proj)


def _outproj_kernel(m_ref, w_ref, xc_ref, xl_ref, mod_ref, o_ref, *, n_ctx_blocks):
    y = _dot(m_ref[...], w_ref[...])
    x = jnp.where(pl.program_id(0) < n_ctx_blocks, xc_ref[...], xl_ref[...])
    o_ref[...] = x + mod_ref[2:3, :] * y


def _outproj(merged, w_out, xc, xl, mod8, seg_of_block, tm):
    t, d = merged.shape
    tn = _pick_tile(d, (512, 256, 128))
    nc = xc.shape[0] // tm
    return pl.pallas_call(
        functools.partial(_outproj_kernel, n_ctx_blocks=nc),
        grid=(t // tm, d // tn),
        in_specs=[pl.BlockSpec((tm, d), lambda i, j: (i, 0)),
                  pl.BlockSpec((d, tn), lambda i, j: (0, j)),
                  pl.BlockSpec((tm, tn), lambda i, j: (jnp.minimum(i, nc - 1), jnp.where(i < nc, j, 0))),
                  pl.BlockSpec((tm, tn), lambda i, j: (jnp.maximum(i - nc, 0), jnp.where(i < nc, 0, j))),
                  pl.BlockSpec((None, 8, tn), lambda i, j: (seg_of_block(i), 0, j))],
        out_specs=pl.BlockSpec((tm, tn), lambda i, j: (i, j)),
        out_shape=jax.ShapeDtypeStruct((t, d), F32),
        compiler_params=_cparams(("parallel", "arbitrary")),
        name="outproj",
    )(merged, w_out, xc, xl, mod8)


def _peerq_kernel(x_ref, mod_ref, g_ref, w_ref, h_ref, q_ref, h_scr):
    @pl.when(pl.program_id(1) == 0)
    def _():
        h = _modulated_norm(x_ref[...], g_ref[...], mod_ref[3:4, :], mod_ref[4:5, :]).astype(BF16)
        h_scr[...] = h
        h_ref[...] = h

    q = _dot(h_scr[...], w_ref[...]).astype(BF16)
    hq = q_ref.shape[2]
    for i in range(q_ref.shape[0]):
        q_ref[i] = q[:, i * hq:(i + 1) * hq]


def _peerq(x1, mod8, seg_of_block, tm, norm_g, w_q):
    t, d = x1.shape
    nq = w_q.shape[1]
    hq = PEER_QDIM // 2
    tn = _pick_tile(nq, (512, 256, 128))
    return pl.pallas_call(
        _peerq_kernel,
        grid=(t // tm, nq // tn),
        in_specs=[pl.BlockSpec((tm, d), lambda i, j: (i, 0)),
                  pl.BlockSpec((None, 8, d), lambda i, j: (seg_of_block(i), 0, 0)),
                  pl.BlockSpec((1, d), lambda i, j: (0, 0)),
                  pl.BlockSpec((d, tn), lambda i, j: (0, j))],
        out_specs=[pl.BlockSpec((tm, d), lambda i, j: (i, 0)),
                   pl.BlockSpec((tn // hq, tm, hq), lambda i, j: (j, i, 0))],
        out_shape=[jax.ShapeDtypeStruct((t, d), BF16),
                   jax.ShapeDtypeStruct((nq // hq, t, hq), BF16)],
        scratch_shapes=[pltpu.VMEM((tm, d), BF16)],
        compiler_params=_cparams(("parallel", "arbitrary")),
        name="peer_query",
    )(x1, mod8, norm_g.reshape(1, d), w_q)


def _ret_kernel(lg_ref, q_ref, k_ref, v_ref, g_ref, gn_ref, *rest, n_chunks, has_init, has_dst, emit_state):
    rest = list(rest)
    s0_ref = rest.pop(0) if has_init else None
    if has_dst:
        rest.pop(0)
    o_ref = rest.pop(0)
    st_ref = rest.pop(0) if emit_state else None
    dmat_scr, vec_scr = rest[:2]
    if n_chunks > 1:
        o_scr, sf_scr, sb_scr = rest[2:]
    c = RET_CHUNK
    scale = DK_RET ** -0.5
    h = pl.program_id(0)
    lgf = lg_ref[0, h]
    lgb = lg_ref[1, h]

    @pl.when(pl.program_id(1) == 0)
    def _():
        row = lax.broadcasted_iota(jnp.int32, (c, c), 0)
        col = lax.broadcasted_iota(jnp.int32, (c, c), 1)
        dlt = (row - col).astype(F32)
        dmat_scr[...] = (jnp.where(dlt >= 0, jnp.exp(lgf * jnp.maximum(dlt, 0.0)), 0.0)
                         + jnp.where(dlt <= 0, jnp.exp(lgb * jnp.maximum(-dlt, 0.0)), 0.0)) * scale
        pos = lax.broadcasted_iota(jnp.int32, (c, DK_RET), 0).astype(F32)
        vec_scr[0] = jnp.exp(lgf * (pos + 1.0))
        vec_scr[1] = jnp.exp(lgb * (c - pos))
        vec_scr[2] = jnp.exp(lgf * (c - 1.0 - pos)) * scale
        vec_scr[3] = jnp.exp(lgb * pos) * scale

    dmat = dmat_scr[...]
    xi_f = vec_scr[0]
    xi_b = vec_scr[1]
    zeta_f = vec_scr[2]
    zeta_b = vec_scr[3]
    dec_f = jnp.exp(jnp.full((1, DV_RET), lgf, F32) * float(c))
    dec_b = jnp.exp(jnp.full((1, DV_RET), lgb, F32) * float(c))

    def rows(i):
        return pl.ds(pl.multiple_of(i * c, c), c)

    def intra(r):
        q = q_ref[r, :]
        k = k_ref[r, :]
        v = v_ref[r, :]
        s = _nt_dot(q, k) * dmat
        return q.astype(F32), k.astype(F32), v, _dot(s.astype(BF16), v)

    def state_inc(kf, zeta, v):
        return _dot((kf * zeta).T.astype(BF16), v)

    def finish(o, r):
        mu = jnp.mean(o, axis=-1, keepdims=True)
        oc = o - mu
        var = jnp.mean(oc * oc, axis=-1, keepdims=True)
        on = oc * lax.rsqrt(var + EPS)
        o_ref[r, :] = (on * gn_ref[...] * _silu(g_ref[r, :].astype(F32))).astype(o_ref.dtype)

    if n_chunks == 1:
        r = pl.ds(0, c)
        qf, kf, v, o = intra(r)
        s_f = state_inc(kf, zeta_f, v)
        s_b = state_inc(kf, zeta_b, v)
        if has_init:
            s0f = s0_ref[0]
            s0b = s0_ref[1]
            o = o + _dot((qf * xi_f).astype(BF16), s0f.astype(BF16))
            o = o + _dot((qf * xi_b).astype(BF16), s0b.astype(BF16))
            s_f = s_f + dec_f * s0f
            s_b = s_b + dec_b * s0b
        finish(o, r)
        if emit_state:
            st_ref[0] = s_f
            st_ref[1] = s_b
        return

    if has_init:
        sf_scr[...] = s0_ref[0]
        sb_scr[...] = s0_ref[1]
    else:
        sf_scr[...] = jnp.zeros_like(sf_scr)
        sb_scr[...] = jnp.zeros_like(sb_scr)

    def fwd(i, carry):
        r = rows(i)
        qf, kf, v, o = intra(r)
        s = sf_scr[...]
        o_scr[r, :] = o + _dot((qf * xi_f).astype(BF16), s.astype(BF16))
        sf_scr[...] = dec_f * s + state_inc(kf, zeta_f, v)
        return carry

    lax.fori_loop(0, n_chunks, fwd, 0)

    def bwd(ii, carry):
        i = n_chunks - 1 - ii
        r = rows(i)
        qf = q_ref[r, :].astype(F32)
        kf = k_ref[r, :].astype(F32)
        v = v_ref[r, :]
        s = sb_scr[...]
        o = o_scr[r, :] + _dot((qf * xi_b).astype(BF16), s.astype(BF16))
        sb_scr[...] = dec_b * s + state_inc(kf, zeta_b, v)
        finish(o, r)
        return carry

    lax.fori_loop(0, n_chunks, bwd, 0)
    if emit_state:
        st_ref[0] = sf_scr[...]
        st_ref[1] = sb_scr[...]


def _retention(proj, lg, ret_gn, s0, dst, *, n_batch, seq, row0, emit_state):
    n_chunks = seq // RET_CHUNK
    rb0 = row0 // seq
    has_init = s0 is not None
    aliases = {}
    in_specs = [pl.BlockSpec(memory_space=pltpu.SMEM),
                pl.BlockSpec((seq, DK_RET), lambda h, b: (rb0 + b, RQ0 // DK_RET + h)),
                pl.BlockSpec((seq, DK_RET), lambda h, b: (rb0 + b, RK0 // DK_RET + h)),
                pl.BlockSpec((seq, DV_RET), lambda h, b: (rb0 + b, RV0 // DV_RET + h)),
                pl.BlockSpec((seq, DV_RET), lambda h, b: (rb0 + b, RG0 // DV_RET + h)),
                pl.BlockSpec((1, DV_RET), lambda h, b: (0, h))]
    args = [lg, proj, proj, proj, proj, ret_gn.reshape(1, RET_V)]
    if has_init:
        in_specs.append(pl.BlockSpec((None, 2, None, DK_RET, DV_RET), lambda h, b: (b, 0, h, 0, 0)))
        args.append(s0)
    if dst is not None:
        in_specs.append(pl.BlockSpec(memory_space=pl.ANY))
        aliases[len(args)] = 0
        args.append(dst)
    out_specs = [pl.BlockSpec((seq, DV_RET), lambda h, b: (rb0 + b, h))]
    out_shape = [jax.ShapeDtypeStruct((proj.shape[0], RET_V), BF16)]
    if emit_state:
        out_specs.append(pl.BlockSpec((None, 2, None, DK_RET, DV_RET), lambda h, b: (b, 0, h, 0, 0)))
        out_shape.append(jax.ShapeDtypeStruct((n_batch, 2, H_RET, DK_RET, DV_RET), F32))
    scratch = [pltpu.VMEM((RET_CHUNK, RET_CHUNK), F32), pltpu.VMEM((4, RET_CHUNK, DK_RET), F32)]
    if n_chunks > 1:
        scratch += [pltpu.VMEM((seq, DV_RET), F32), pltpu.VMEM((DK_RET, DV_RET), F32),
                    pltpu.VMEM((DK_RET, DV_RET), F32)]
    return pl.pallas_call(
        functools.partial(_ret_kernel, n_chunks=n_chunks, has_init=has_init, has_dst=dst is not None,
                          emit_state=emit_state),
        grid=(H_RET, n_batch),
        in_specs=in_specs,
        out_specs=out_specs,
        out_shape=out_shape,
        input_output_aliases=aliases,
        scratch_shapes=scratch,
        compiler_params=_cparams(("parallel", "arbitrary")),
        name="retention_ctx" if emit_state else "retention_lat",
    )(*args)


def _dn_kernel(par_ref, q_ref, k_ref, v_ref, z_ref, gt_ref, cq_ref, ck_ref, cv_ref, nrm_ref, *rest,
               seq, row_len, heads, has_init, has_dst, emit_state):
    rest = list(rest)
    s0_ref = rest.pop(0) if has_init else None
    if has_dst:
        rest.pop(0)
    o_ref = rest.pop(0)
    st_ref = rest.pop(0) if emit_state else None
    rows_scr, o_scr, s_scr, qkv_scr = rest
    cb = DN_CHUNK
    n_blocks = seq // cb
    hg = pl.program_id(1)

    lane = lax.broadcasted_iota(jnp.int32, (1, seq), 1) & (cb - 1)

    def splat(v):
        return jnp.full((1, seq), v, F32)

    def prefix(x):
        s = 1
        while s < cb:
            x = x + jnp.where(lane >= s, pltpu.roll(x, s, 1), 0.0)
            s *= 2
        return x

    def suffix(x):
        s = 1
        while s < cb:
            x = x + jnp.where(lane < cb - s, pltpu.roll(x, seq - s, 1), 0.0)
            s *= 2
        return x

    for g in range(heads):
        h = hg * heads + g
        r0 = 8 * g
        g_f = (-jnp.exp(splat(par_ref[0, h]))
               * jax.nn.softplus(gt_ref[r0 + 2:r0 + 3, :] + splat(par_ref[2, h])))
        g_b = (-jnp.exp(splat(par_ref[1, h]))
               * jax.nn.softplus(gt_ref[r0 + 3:r0 + 4, :] + splat(par_ref[3, h])))
        rows_scr[r0:r0 + 1, :] = jax.nn.sigmoid(gt_ref[r0:r0 + 1, :])
        rows_scr[r0 + 1:r0 + 2, :] = prefix(g_f)
        rows_scr[r0 + 2:r0 + 3, :] = suffix(g_f) - g_f
        rows_scr[r0 + 3:r0 + 4, :] = jax.nn.sigmoid(gt_ref[r0 + 1:r0 + 2, :])
        rows_scr[r0 + 4:r0 + 5, :] = suffix(g_b)
        rows_scr[r0 + 5:r0 + 6, :] = prefix(g_b) - g_b
        rows_scr[r0 + 6:r0 + 8, :] = jnp.zeros((2, seq), F32)
        for direction in range(2):
            if has_init:
                s_scr[g, direction] = s0_ref[direction, g]
            else:
                s_scr[g, direction] = jnp.zeros((DK_DN, DV_DN), F32)

    ri = lax.broadcasted_iota(jnp.int32, (cb, cb), 0)
    ci = lax.broadcasted_iota(jnp.int32, (cb, cb), 1)
    eye = (ri == ci).astype(F32)
    pair_masks = []
    sz = 2
    while sz < cb:
        pair = ((ri // (2 * sz)) == (ci // (2 * sz))) & ((ri // sz) != (ci // sz))
        pair_masks.append(jnp.where(pair, 1.0, 0.0).astype(BF16))
        sz *= 2
    n_small = (DN_SLAB // 2).bit_length() - 1
    same_slab = (ri // DN_SLAB) == (ci // DN_SLAB)
    slab_mask16 = jnp.where(same_slab, 1.0, 0.0).astype(BF16)
    tpos = lax.broadcasted_iota(jnp.int32, (cb, DK_DN), 0) & (row_len - 1)

    def conv(x, w_ref, lanes):
        acc = x * w_ref[CONV_K // 2:CONV_K // 2 + 1, lanes]
        for i in range(CONV_K):
            d = i - CONV_K // 2
            if d == 0:
                continue
            sh = pltpu.roll(x, (-d) % cb, 0)
            ok = (tpos + d >= 0) & (tpos + d < row_len)
            acc = acc + jnp.where(ok, sh, 0.0) * w_ref[i:i + 1, lanes]
        return _silu(acc)

    def l2n(x):
        return x * lax.rsqrt(jnp.sum(x * x, axis=-1, keepdims=True) + EPS)

    def col_of(row):
        parts = []
        for t in range(cb // LANES):
            parts.append(jnp.broadcast_to(row[:, t * LANES:(t + 1) * LANES], (LANES, LANES)).T)
        return jnp.concatenate(parts, axis=0)

    def rows_of(bi):
        return pl.ds(pl.multiple_of(bi * cb, cb), cb)

    def prepare(bi, carry):
        r = rows_of(bi)
        for g in range(heads):
            lanes = slice(g * DK_DN, (g + 1) * DK_DN)
            qkv_scr[g, 0, r, :] = l2n(conv(q_ref[r, lanes].astype(F32), cq_ref, lanes)) * (DK_DN ** -0.5)
            qkv_scr[g, 1, r, :] = l2n(conv(k_ref[r, lanes].astype(F32), ck_ref, lanes))
            qkv_scr[g, 2, r, :] = conv(v_ref[r, lanes].astype(F32), cv_ref, lanes)
        return carry

    if n_blocks == 1:
        prepare(0, 0)
    else:
        lax.fori_loop(0, n_blocks, prepare, 0)

    def blocks(chains):
        cs = []
        for g, bi, direction in chains:
            r = rows_of(bi)
            base = 8 * g + (0 if direction == 0 else 3)
            cs.append(dict(g=g, d=direction, r=r, beta_row=rows_scr[base:base + 1, r],
                           g_row=rows_scr[base + 1:base + 2, r], e_row=rows_scr[base + 2:base + 3, r]))
        for c in cs:
            c["q"] = qkv_scr[c["g"], 0, c["r"], :]
            c["k"] = qkv_scr[c["g"], 1, c["r"], :]
            c["v"] = qkv_scr[c["g"], 2, c["r"], :]
            c["k16"] = c["k"].astype(BF16)
            c["kk"] = _nt_dot(c["k16"], c["k16"])
        for c in cs:
            c["beta_c"] = col_of(c["beta_row"])
            c["g_c"] = col_of(c["g_row"])
            c["incl"] = (ri >= ci) if c["d"] == 0 else (ri <= ci)
            strict = (ri > ci) if c["d"] == 0 else (ri < ci)
            g_cw = jnp.concatenate([c["g_c"]] * (cb // LANES), axis=1)
            beta_cw = jnp.concatenate([c["beta_c"]] * (cb // LANES), axis=1)
            c["lmat"] = jnp.exp(jnp.where(c["incl"], g_cw - c["g_row"], NEG_INF))
            c["m"] = jnp.where(strict, c["kk"] * beta_cw * c["lmat"], 0.0)
            c["x"] = eye - jnp.where((ri // 2) == (ci // 2), c["m"], 0.0)
        n_slab = cb // DN_SLAB
        for c in cs:
            c["m16"] = c["m"].astype(BF16)
            c["xc"] = functools.reduce(
                lambda a, b: a + b, [c["x"][p * DN_SLAB:(p + 1) * DN_SLAB, :] for p in range(n_slab)])
        for pair16 in pair_masks[:n_small]:
            for c in cs:
                xc16 = c["xc"].astype(BF16)
                c["xbd16"] = jnp.concatenate([xc16] * n_slab, axis=0) * slab_mask16
                c["t16"] = _dot(xc16, c["m16"] * pair16).astype(BF16)
            for c in cs:
                c["xc"] = c["xc"] - _dot(c["t16"], c["xbd16"])
        for c in cs:
            c["x"] = jnp.where(same_slab, jnp.concatenate([c["xc"]] * n_slab, axis=0), 0.0)
        for pair16 in pair_masks[n_small:]:
            for c in cs:
                c["x16"] = c["x"].astype(BF16)
                c["t16"] = _dot(c["x16"], c["m16"] * pair16).astype(BF16)
            for c in cs:
                c["x"] = c["x"] - _dot(c["t16"], c["x16"])
        for c in cs:
            eg_c = jnp.exp(c["g_c"])
            rhs = jnp.concatenate([c["v"] * c["beta_c"], c["k"] * (c["beta_c"] * eg_c)], axis=1).astype(BF16)
            uw = _dot(c["x"].astype(BF16), rhs)
            c["u"] = uw[:, :DV_DN]
            c["w16"] = uw[:, DV_DN:].astype(BF16)
            c["qg16"] = (c["q"] * eg_c).astype(BF16)
            c["aqk"] = (_nt_dot(c["q"].astype(BF16), c["k16"]) * c["lmat"]).astype(BF16)
            c["kdec_t"] = (c["k"] * jnp.exp(col_of(c["e_row"]))).T.astype(BF16)
        outs = []
        for c in cs:
            s = s_scr[c["g"], c["d"]]
            s16 = s.astype(BF16)
            vn16 = (c["u"] - _dot(c["w16"], s16)).astype(BF16)
            outs.append(_dot(c["qg16"], s16) + _dot(c["aqk"], vn16))
            last = cb - 1 if c["d"] == 0 else 0
            s_scr[c["g"], c["d"]] = s * jnp.exp(c["g_c"][last:last + 1, :]) + _dot(c["kdec_t"], vn16)
        return outs

    def finish(g, bi, d):
        r = rows_of(bi)
        lanes = slice(g * DV_DN, (g + 1) * DV_DN)
        d = d * lax.rsqrt(jnp.mean(d * d, axis=-1, keepdims=True) + EPS)
        o_ref[r, lanes] = (d * nrm_ref[...] * _silu(z_ref[r, lanes].astype(F32))).astype(o_ref.dtype)

    if n_blocks == 1:
        outs = blocks([(g, 0, d) for g in range(heads) for d in range(2)])
        for g in range(heads):
            finish(g, 0, outs[2 * g] + outs[2 * g + 1])
    else:
        per = 2 if n_blocks % 4 == 0 else 1

        def chains_of(it):
            return [(g, per * it + j if d == 0 else n_blocks - 1 - per * it - j, d)
                    for g in range(heads) for d in range(2) for j in range(per)]

        def first_half(it, carry):
            chains = chains_of(it)
            for (g, bi, _), out in zip(chains, blocks(chains)):
                o_scr[g, rows_of(bi), :] = out
            return carry

        def second_half(it, carry):
            chains = chains_of(it)
            for (g, bi, _), out in zip(chains, blocks(chains)):
                finish(g, bi, out + o_scr[g, rows_of(bi), :])
            return carry

        n_it = n_blocks // per
        lax.fori_loop(0, n_it // 2, first_half, 0)
        lax.fori_loop(n_it // 2, n_it, second_half, 0)
    if emit_state:
        for g in range(heads):
            st_ref[0, g] = s_scr[g, 0]
            st_ref[1, g] = s_scr[g, 1]


def _deltanet(proj, gates_t, dn_par, dn_conv, dn_norm, s0, dst, *, n_batch, seq, row0, row_len, heads,
              emit_state):
    rb0 = row0 // seq
    has_init = s0 is not None
    aliases = {}
    wd = heads * DK_DN
    in_specs = [pl.BlockSpec(memory_space=pltpu.SMEM),
                pl.BlockSpec((seq, wd), lambda b, h: (rb0 + b, DQ0 // wd + h)),
                pl.BlockSpec((seq, wd), lambda b, h: (rb0 + b, DK0 // wd + h)),
                pl.BlockSpec((seq, wd), lambda b, h: (rb0 + b, DV0 // wd + h)),
                pl.BlockSpec((seq, wd), lambda b, h: (rb0 + b, DZ0 // wd + h)),
                pl.BlockSpec((8 * heads, seq), lambda b, h: (h, rb0 + b)),
                pl.BlockSpec((CONV_K, wd), lambda b, h: (0, h)),
                pl.BlockSpec((CONV_K, wd), lambda b, h: (0, DN_QK // wd + h)),
                pl.BlockSpec((CONV_K, wd), lambda b, h: (0, 2 * DN_QK // wd + h)),
                pl.BlockSpec((1, DV_DN), lambda b, h: (0, 0))]
    args = [dn_par, proj, proj, proj, proj, gates_t, dn_conv, dn_conv, dn_conv, dn_norm.reshape(1, DV_DN)]
    if has_init:
        in_specs.append(pl.BlockSpec((None, 2, heads, DK_DN, DV_DN), lambda b, h: (b, 0, h, 0, 0)))
        args.append(s0)
    if dst is not None:
        in_specs.append(pl.BlockSpec(memory_space=pl.ANY))
        aliases[len(args)] = 0
        args.append(dst)
    out_specs = [pl.BlockSpec((seq, wd), lambda b, h: (rb0 + b, h))]
    out_shape = [jax.ShapeDtypeStruct((proj.shape[0], DN_V), BF16)]
    if emit_state:
        out_specs.append(pl.BlockSpec((None, 2, heads, DK_DN, DV_DN), lambda b, h: (b, 0, h, 0, 0)))
        out_shape.append(jax.ShapeDtypeStruct((n_batch, 2, H_DN, DK_DN, DV_DN), F32))
    scratch = [pltpu.VMEM((8 * heads, seq), F32), pltpu.VMEM((heads, seq, DV_DN), F32),
               pltpu.VMEM((heads, 2, DK_DN, DV_DN), F32), pltpu.VMEM((heads, 3, seq, DK_DN), F32)]
    n_blocks = seq // DN_CHUNK
    assert seq % DN_CHUNK == 0 and (n_blocks == 1 or n_blocks % 2 == 0)
    assert DN_CHUNK % row_len == 0 and row_len & (row_len - 1) == 0 and H_DN % heads == 0
    return pl.pallas_call(
        functools.partial(_dn_kernel, seq=seq, row_len=row_len, heads=heads, has_init=has_init,
                          has_dst=dst is not None, emit_state=emit_state),
        grid=(n_batch, H_DN // heads),
        in_specs=in_specs,
        out_specs=out_specs,
        out_shape=out_shape,
        input_output_aliases=aliases,
        scratch_shapes=scratch,
        compiler_params=_cparams(("parallel", "arbitrary")),
        name="deltanet_ctx" if emit_state else "deltanet_lat",
    )(*args)


def _sort_network(n):
    pairs = []
    k = 2
    while k <= n:
        j = k // 2
        while j >= 1:
            for i in range(n):
                l = i ^ j
                if l > i:
                    pairs.append((i, l) if (i & k) == 0 else (l, i))
            j //= 2
        k *= 2
    return pairs


def _cand_pairs():
    n = PEER_TOPK + 1
    return [(a, b) for a in range(n) for b in range(n) if (a + 1) * (b + 1) <= n]


def _peer_score_kernel(q_ref, sk_ref, pb_ref, pf_ref, s_scr, sv_scr, gp_scr):
    def top_values_sorted(s, p, h):
        n_rows = 8
        v = [s[j * n_rows:(j + 1) * n_rows, :] for j in range(N_KEYS // n_rows)]
        for a, b in _sort_network(len(v)):
            v[a], v[b] = jnp.maximum(v[a], v[b]), jnp.minimum(v[a], v[b])
        for r in range(PEER_TOPK + 1):
            m = jnp.max(v[0], axis=0, keepdims=True)
            sv_scr[p, r, pl.ds(h, 1), :] = m
            hit = v[0] == m
            live = min(len(v), PEER_TOPK + 1 - r)
            for j in range(live - 1):
                v[j] = jnp.where(hit, v[j + 1], v[j])
            if live == len(v):
                v[live - 1] = jnp.where(hit, NEG_INF, v[live - 1])

    def per_head(h, carry):
        for p in range(2):
            s = _nt_dot(sk_ref[p], q_ref[2 * h + p])
            s_scr[p, h] = s
            top_values_sorted(s, p, h)
        rank = jnp.zeros(s.shape, F32)
        for r in range(PEER_TOPK):
            rank = rank + jnp.where(sv_scr[1, r, pl.ds(h, 1), :] > s, 1.0, 0.0)
        pb_ref[0, h] = rank.astype(BF16)
        return carry

    lax.fori_loop(0, PEER_HEADS, per_head, 0)
    cands = [sv_scr[0, a] + sv_scr[1, b] for a, b in _cand_pairs()]
    mx = cands[0]
    z = jnp.zeros_like(mx)
    for r in range(PEER_TOPK):
        m = functools.reduce(jnp.maximum, cands)
        z = z + jnp.exp(m - mx)
        cands = [jnp.where(c == m, NEG_INF, c) for c in cands]
    runner_up = functools.reduce(jnp.maximum, cands)
    gp_scr[0] = 0.5 * (m + runner_up)
    gp_scr[1] = 1.0 / z

    def finish_head(h, carry):
        row = pl.ds(h, 1)
        s1 = s_scr[0, h]
        pb_ref[1, h] = jnp.exp(s_scr[1, h] - sv_scr[1, 0, row, :]).astype(BF16)
        pf_ref[0, h] = jnp.exp(s1 - sv_scr[0, 0, row, :]) * gp_scr[1, row, :]
        need = gp_scr[0, row, :] - s1
        cnt = jnp.zeros_like(need)
        for r in range(PEER_TOPK):
            cnt = cnt + jnp.where(sv_scr[1, r, row, :] >= need, 1.0, 0.0)
        pf_ref[1, h] = cnt
        return carry

    lax.fori_loop(0, PEER_HEADS, finish_head, 0)


def _peer_scores(q3, subkeys, tb):
    nhp, t, _ = q3.shape
    return pl.pallas_call(
        _peer_score_kernel,
        grid=(t // tb,),
        in_specs=[pl.BlockSpec((nhp, tb, PEER_QDIM // 2), lambda i: (0, i, 0)),
                  pl.BlockSpec((2, N_KEYS, PEER_QDIM // 2), lambda i: (0, 0, 0))],
        out_specs=[pl.BlockSpec((2, PEER_HEADS, N_KEYS, tb), lambda i: (0, 0, 0, i)),
                   pl.BlockSpec((2, PEER_HEADS, N_KEYS, tb), lambda i: (0, 0, 0, i))],
        out_shape=[jax.ShapeDtypeStruct((2, PEER_HEADS, N_KEYS, t), BF16),
                   jax.ShapeDtypeStruct((2, PEER_HEADS, N_KEYS, t), F32)],
        scratch_shapes=[pltpu.VMEM((2, PEER_HEADS, N_KEYS, tb), F32),
                        pltpu.VMEM((2, PEER_TOPK + 1, PEER_HEADS, tb), F32),
                        pltpu.VMEM((2, PEER_HEADS, tb), F32)],
        compiler_params=_cparams(("parallel",)),
        name="peer_scores",
    )(q3, subkeys)


def _gelu_tanh(x):
    return 0.5 * x * (1.0 + jnp.tanh(0.7978845608028654 * (x + 0.044715 * (x * x * x))))


def _peer_dense_kernel(h_ref, pb_ref, pf_ref, u_ref, vt_ref, o_ref, *, n_i1):
    e = pl.program_id(1)
    zero = jnp.zeros((), BF16)
    tb, d = h_ref.shape
    kt = V7X_MXU_DIM
    per_m = kt // N_KEYS

    @pl.when(e == 0)
    def _():
        o_ref[...] = jnp.zeros_like(o_ref)

    def gate_head(ii, h):
        row = pl.ds(e * n_i1 + ii, 1)

        def bcast(x):
            x = jnp.broadcast_to(x, (16, x.shape[1])).astype(BF16)
            return jnp.concatenate([x] * (N_KEYS // 16), axis=0)

        e1 = bcast(pf_ref[0, h, row, :])
        cnt = bcast(pf_ref[1, h, row, :])
        return jnp.where(pb_ref[0, h] < cnt, pb_ref[1, h] * e1, zero)

    wa_parts = []
    for m in range(n_i1 // per_m):
        todo = [(per_m * m + j, h) for j in range(per_m) for h in range(PEER_HEADS)]
        gate = {}

        def one_gate():
            ii, h = todo.pop(0)
            g = gate_head(ii, h)
            gate[ii] = g if h == 0 else gate[ii] + g

        accs = [None] * (tb // kt)
        for k in range(d // kt):
            for n in range(tb // kt):
                part = _nt_dot(u_ref[m * kt:(m + 1) * kt, k * kt:(k + 1) * kt],
                               h_ref[n * kt:(n + 1) * kt, k * kt:(k + 1) * kt])
                accs[n] = part if accs[n] is None else accs[n] + part
                if todo:
                    one_gate()
        while todo:
            one_gate()
        act = jnp.concatenate([_gelu_tanh(a).astype(BF16) for a in accs], axis=1)
        w = jnp.concatenate([gate[per_m * m + j] for j in range(per_m)], axis=0)
        wa_parts.append(w * act)
    o_ref[...] += _dot(vt_ref[...], jnp.concatenate(wa_parts, axis=0))


def _peer_dense(h2, pb, pf, u_tab, v_tab_t, tb, et):
    t, d = h2.shape
    ne = u_tab.shape[0]
    once = pl.Buffered(1)
    return pl.pallas_call(
        functools.partial(_peer_dense_kernel, n_i1=et // N_KEYS),
        grid=(t // tb, ne // et),
        in_specs=[pl.BlockSpec((tb, d), lambda i, e: (i, 0), pipeline_mode=once),
                  pl.BlockSpec((2, PEER_HEADS, N_KEYS, tb), lambda i, e: (0, 0, 0, i), pipeline_mode=once),
                  pl.BlockSpec((2, PEER_HEADS, N_KEYS, tb), lambda i, e: (0, 0, 0, i), pipeline_mode=once),
                  pl.BlockSpec((et, d), lambda i, e: (e, 0)),
                  pl.BlockSpec((d, et), lambda i, e: (0, e))],
        out_specs=pl.BlockSpec((d, tb), lambda i, e: (0, i)),
        out_shape=jax.ShapeDtypeStruct((d, t), F32),
        compiler_params=_cparams(("parallel", "arbitrary")),
        name="peer_dense",
    )(h2, pb, pf, u_tab, v_tab_t)


def _final_kernel(x_ref, p_ref, mod_ref, g_ref, oc_ref, ol_ref, *, n_ctx_blocks):
    x = x_ref[...] + mod_ref[5:6, :] * p_ref[...].T
    ms = jnp.mean(x * x, axis=-1, keepdims=True)
    y = (x * lax.rsqrt(ms + EPS)) * g_ref[...]
    i = pl.program_id(0)

    @pl.when(i < n_ctx_blocks)
    def _():
        oc_ref[...] = y

    @pl.when(i >= n_ctx_blocks)
    def _():
        ol_ref[...] = y


def _final(x1, peer_t, mod8, seg_of_block, tm_seg, final_norm, tc):
    t, d = x1.shape
    tm = FINAL_TILE
    sub = tm_seg // tm
    nc = tc // tm
    return pl.pallas_call(
        functools.partial(_final_kernel, n_ctx_blocks=nc),
        grid=(t // tm,),
        in_specs=[pl.BlockSpec((tm, d), lambda i: (i, 0)),
                  pl.BlockSpec((d, tm), lambda i: (0, i)),
                  pl.BlockSpec((None, 8, d), lambda i: (seg_of_block(i // sub), 0, 0)),
                  pl.BlockSpec((1, d), lambda i: (0, 0))],
        out_specs=[pl.BlockSpec((tm, d), lambda i: (jnp.minimum(i, nc - 1), 0)),
                   pl.BlockSpec((tm, d), lambda i: (jnp.maximum(i - nc, 0), 0))],
        out_shape=[jax.ShapeDtypeStruct((tc, d), F32), jax.ShapeDtypeStruct((t - tc, d), F32)],
        compiler_params=_cparams(("arbitrary",)),
        name="final_norm",
    )(x1, peer_t, mod8, final_norm.reshape(1, d))


def kernel(x_prompt, x_sample, state_ret, state_dn, c, c_ctx, w_ada, b_ada, norm_mix, norm_ffn, w_in, ret_logit,
           ret_gn, dn_conv, dn_a_log, dn_dt_bias, dn_norm, w_br_a, w_br_b, w_out, peer_wq, peer_subkeys, peer_u,
           peer_v, final_norm):
    bc, lc, d = x_prompt.shape
    bl, ll, _ = x_sample.shape
    depth = w_ada.shape[0]
    tc = bc * lc
    t = tc + bl * ll
    assert bl + 1 <= 8 and tc % ll == 0

    tm = _pick_tile(math.gcd(tc, ll), (1024, 512, 256))
    n_ctx_blocks = tc // tm
    lat_blocks = ll // tm

    def seg_of_block(i):
        return jnp.where(i < n_ctx_blocks, 0, 1 + (i - n_ctx_blocks) // lat_blocks)

    xc = x_prompt.reshape(tc, d)
    xl = x_sample.reshape(bl * ll, d)
    cvec = jnp.concatenate([c_ctx[None, :], c, jnp.zeros((8 - 1 - bl, d), F32)], axis=0)

    ret_states, dn_states = [], []
    for l in range(depth):
        mod = _adaln(cvec, w_ada[l], b_ada[l])
        mod8 = jnp.pad(mod.reshape(8, 6, d), ((0, 0), (0, 2), (0, 0)))

        w = w_in[l].astype(BF16)
        wg = w[:, DGL0:DGL0 + 4 * H_DN].reshape(d, 4, H_DN).transpose(2, 1, 0)
        w_gate_t = jnp.pad(wg, ((0, 0), (0, 4), (0, 0))).reshape(8 * H_DN, d)

        proj, gates_t = _inproj(xc, xl, mod8, seg_of_block, tm, norm_mix[l], w, w[:, DGL0 + 4 * H_DN:], w_gate_t)

        lg = jax.nn.log_sigmoid(ret_logit[l].astype(F32))
        ret_c, rs = _retention(proj, lg, ret_gn[l], None, None, n_batch=bc, seq=lc, row0=0, emit_state=True)
        (ret_out,) = _retention(proj, lg, ret_gn[l], state_ret[:, l], ret_c, n_batch=bl, seq=ll, row0=tc,
                                emit_state=False)
        dn_par = jnp.concatenate([dn_a_log[l], dn_dt_bias[l]], axis=0).astype(F32)
        dn_c, ds = _deltanet(proj, gates_t, dn_par, dn_conv[l], dn_norm[l], None, None, n_batch=bc, seq=lc,
                             row0=0, row_len=lc, heads=4, emit_state=True)
        (dn_out,) = _deltanet(proj, gates_t, dn_par, dn_conv[l], dn_norm[l], state_dn[:, l], dn_c, n_batch=bl,
                              seq=ll, row0=tc, row_len=GRID_W, heads=2, emit_state=False)
        ret_states.append(rs)
        dn_states.append(ds)

        merged = _merge(ret_out, dn_out, w_br_a[l].astype(BF16), w_br_b[l].astype(BF16), proj, d, tm)
        x1 = _outproj(merged, w_out[l].astype(BF16), xc, xl, mod8, seg_of_block, tm)

        h2, q3 = _peerq(x1, mod8, seg_of_block, tm, norm_ffn[l], peer_wq[l].astype(BF16))
        tb = min(PEER_TOKEN_TILE, tm)
        pb, pf = _peer_scores(q3, peer_subkeys[l].astype(BF16), tb)
        peer_t = _peer_dense(h2, pb, pf, peer_u[l].astype(BF16), peer_v[l].astype(BF16).T, tb,
                             PEER_EXPERT_TILE)
        assert depth == 1

    y_c, y_l = _final(x1, peer_t, mod8, seg_of_block, tm, final_norm, tc)
    y_prompt = y_c.reshape(bc, lc, d)
    y_sample = y_l.reshape(bl, ll, d)
    new_state_ret = jnp.stack(ret_states, axis=1)
    new_state_dn = jnp.stack(dn_states, axis=1)
    return (y_prompt, y_sample, new_state_ret, new_state_dn)
```

```python
import functools
import math

import jax
import jax.numpy as jnp
from jax import lax
from jax.experimental import pallas as pl
from jax.experimental.pallas import tpu as pltpu

F32 = jnp.float32
BF16 = jnp.bfloat16

H_RET, DK_RET, DV_RET = 8, 128, 256
H_DN, DK_DN, DV_DN = 16, 128, 128
CONV_K = 5
GRID_W = 64
DN_CHUNK = 256
DN_SLAB = 64
RET_CHUNK = 256
N_KEYS = 128
PEER_HEADS = 8
PEER_QDIM = 256
PEER_TOPK = 16
EPS = 1e-6

RET_QK = H_RET * DK_RET
RET_V = H_RET * DV_RET
DN_QK = H_DN * DK_DN
DN_V = H_DN * DV_DN
RQ0 = 0
RK0 = RQ0 + RET_QK
RV0 = RK0 + RET_QK
RG0 = RV0 + RET_V
DQ0 = RG0 + RET_V
DK0 = DQ0 + DN_QK
DV0 = DK0 + DN_QK
DZ0 = DV0 + DN_V
DGL0 = DZ0 + DN_V
BG0 = DZ0 + DN_V

V7X_VMEM_BYTES = 64 * 1024 * 1024
VMEM_LIMIT = V7X_VMEM_BYTES - 8 * 1024 * 1024
V7X_MXU_DIM = 256
LANES = 128
PEER_TOKEN_TILE = 512
PEER_EXPERT_TILE = 1024
FINAL_TILE = 256
NEG_INF = float("-inf")


def _cparams(sem):
    return pltpu.CompilerParams(dimension_semantics=sem, vmem_limit_bytes=VMEM_LIMIT)


def _nt_dot(a, b):
    return lax.dot_general(a, b, (((1,), (1,)), ((), ())), preferred_element_type=F32)


def _dot(a, b):
    return jnp.dot(a, b, preferred_element_type=F32)


def _silu(x):
    h = 0.5 * x
    return h + h * jnp.tanh(h)


def _pick_tile(n, cands):
    for c in cands:
        if n % c == 0:
            return c
    raise ValueError(f"no tile for {n}")


def _adaln_kernel(c_ref, w_ref, b_ref, o_ref):
    a = _silu(c_ref[...])
    o_ref[...] = jnp.dot(a, w_ref[...], precision=lax.Precision.HIGHEST,
                         preferred_element_type=F32) + b_ref[...]


def _adaln(cvec, w_ada, b_ada):
    d, n = w_ada.shape
    tn = _pick_tile(n, (1024, 512, 256, 128))
    return pl.pallas_call(
        _adaln_kernel,
        grid=(n // tn,),
        in_specs=[pl.BlockSpec((8, d), lambda j: (0, 0)),
                  pl.BlockSpec((d, tn), lambda j: (0, j)),
                  pl.BlockSpec((1, tn), lambda j: (0, j))],
        out_specs=pl.BlockSpec((8, tn), lambda j: (0, j)),
        out_shape=jax.ShapeDtypeStruct((8, n), F32),
        compiler_params=_cparams(("parallel",)),
        name="adaln",
    )(cvec, w_ada, b_ada.reshape(1, n))


def _modulated_norm(x, g, shift, scale):
    ms = jnp.mean(x * x, axis=-1, keepdims=True)
    return (x * lax.rsqrt(ms + EPS)) * g * (1.0 + scale) + shift


def _inproj_kernel(xc_ref, xl_ref, mod_ref, g_ref, wa_ref, wb_ref, wg_ref, o_ref, gt_ref, h_scr, *,
                   n_ctx_blocks, n_head_tiles):
    i = pl.program_id(0)
    j = pl.program_id(1)

    def prepare(x_ref):
        h = _modulated_norm(x_ref[...], g_ref[...], mod_ref[0:1, :], mod_ref[1:2, :]).astype(BF16)
        h_scr[...] = h
        gt_ref[...] = _nt_dot(wg_ref[...], h)

    pl.when((j == 0) & (i < n_ctx_blocks))(functools.partial(prepare, xc_ref))
    pl.when((j == 0) & (i >= n_ctx_blocks))(functools.partial(prepare, xl_ref))

    @pl.when(j < n_head_tiles)
    def _():
        o_ref[...] = _dot(h_scr[...], wa_ref[...]).astype(o_ref.dtype)

    @pl.when(j >= n_head_tiles)
    def _():
        o_ref[...] = _dot(h_scr[...], wb_ref[...]).astype(o_ref.dtype)


def _inproj(xc, xl, mod8, seg_of_block, tm, norm_g, w_all, w_tail, w_gate_t):
    d = xc.shape[1]
    t = xc.shape[0] + xl.shape[0]
    nm = BG0 + w_tail.shape[1]
    tn = _pick_tile(math.gcd(BG0, w_tail.shape[1]), (1024, 512, 256))
    nc = xc.shape[0] // tm
    nh = BG0 // tn
    once = pl.Buffered(1)
    return pl.pallas_call(
        functools.partial(_inproj_kernel, n_ctx_blocks=nc, n_head_tiles=nh),
        grid=(t // tm, nm // tn),
        in_specs=[pl.BlockSpec((tm, d), lambda i, j: (jnp.minimum(i, nc - 1), 0), pipeline_mode=once),
                  pl.BlockSpec((tm, d), lambda i, j: (jnp.maximum(i - nc, 0), 0), pipeline_mode=once),
                  pl.BlockSpec((None, 8, d), lambda i, j: (seg_of_block(i), 0, 0)),
                  pl.BlockSpec((1, d), lambda i, j: (0, 0)),
                  pl.BlockSpec((d, tn), lambda i, j: (0, jnp.minimum(j, nh - 1))),
                  pl.BlockSpec((d, tn), lambda i, j: (0, jnp.maximum(j - nh, 0))),
                  pl.BlockSpec((8 * H_DN, d), lambda i, j: (0, 0))],
        out_specs=[pl.BlockSpec((tm, tn), lambda i, j: (i, j)),
                   pl.BlockSpec((8 * H_DN, tm), lambda i, j: (0, i))],
        out_shape=[jax.ShapeDtypeStruct((t, nm), BF16),
                   jax.ShapeDtypeStruct((8 * H_DN, t), F32)],
        scratch_shapes=[pltpu.VMEM((tm, d), BF16)],
        compiler_params=_cparams(("parallel", "arbitrary")),
        name="inproj",
    )(xc, xl, mod8, norm_g.reshape(1, d), w_all, w_tail, w_gate_t)


def _merge_kernel(a_ref, b_ref, wa_ref, wb_ref, ga_ref, gb_ref, o_ref):
    ya = _dot(a_ref[...], wa_ref[...])
    yb = _dot(b_ref[...], wb_ref[...])
    ga = jax.nn.sigmoid(ga_ref[...].astype(F32))
    gb = jax.nn.sigmoid(gb_ref[...].astype(F32))
    o_ref[...] = (ga * ya + gb * yb).astype(o_ref.dtype)


def _merge(ret_out, dn_out, w_a, w_b, proj, d, tm):
    t = ret_out.shape[0]
    tn = _pick_tile(d, (512, 256, 128))
    ga0 = BG0 // tn
    gb0 = (BG0 + d) // tn
    return pl.pallas_call(
        _merge_kernel,
        grid=(t // tm, d // tn),
        in_specs=[pl.BlockSpec((tm, RET_V), lambda i, j: (i, 0)),
                  pl.BlockSpec((tm, DN_V), lambda i, j: (i, 0)),
                  pl.BlockSpec((RET_V, tn), lambda i, j: (0, j)),
                  pl.BlockSpec((DN_V, tn), lambda i, j: (0, j)),
                  pl.BlockSpec((tm, tn), lambda i, j: (i, ga0 + j)),
                  pl.BlockSpec((tm, tn), lambda i, j: (i, gb0 + j))],
        out_specs=pl.BlockSpec((tm, tn), lambda i, j: (i, j)),
        out_shape=jax.ShapeDtypeStruct((t, d), BF16),
        compiler_params=_cparams(("parallel", "arbitrary")),
        name="merge",
    )(ret_out, dn_out, w_a, w_b, proj, proj)


def _outproj_kernel(m_ref, w_ref, xc_ref, xl_ref, mod_ref, o_ref, *, n_ctx_blocks):
    y = _dot(m_ref[...], w_ref[...])
    x = jnp.where(pl.program_id(0) < n_ctx_blocks, xc_ref[...], xl_ref[...])
    o_ref[...] = x + mod_ref[2:3, :] * y


def _outproj(merged, w_out, xc, xl, mod8, seg_of_block, tm):
    t, d = merged.shape
    tn = _pick_tile(d, (512, 256, 128))
    nc = xc.shape[0] // tm
    return pl.pallas_call(
        functools.partial(_outproj_kernel, n_ctx_blocks=nc),
        grid=(t // tm, d // tn),
        in_specs=[pl.BlockSpec((tm, d), lambda i, j: (i, 0)),
                  pl.BlockSpec((d, tn), lambda i, j: (0, j)),
                  pl.BlockSpec((tm, tn), lambda i, j: (jnp.minimum(i, nc - 1), jnp.where(i < nc, j, 0))),
                  pl.BlockSpec((tm, tn), lambda i, j: (jnp.maximum(i - nc, 0), jnp.where(i < nc, 0, j))),
                  pl.BlockSpec((None, 8, tn), lambda i, j: (seg_of_block(i), 0, j))],
        out_specs=pl.BlockSpec((tm, tn), lambda i, j: (i, j)),
        out_shape=jax.ShapeDtypeStruct((t, d), F32),
        compiler_params=_cparams(("parallel", "arbitrary")),
        name="outproj",
    )(merged, w_out, xc, xl, mod8)


def _peerq_kernel(x_ref, mod_ref, g_ref, w_ref, h_ref, q_ref, h_scr):
    @pl.when(pl.program_id(1) == 0)
    def _():
        h = _modulated_norm(x_ref[...], g_ref[...], mod_ref[3:4, :], mod_ref[4:5, :]).astype(BF16)
        h_scr[...] = h
        h_ref[...] = h

    q = _dot(h_scr[...], w_ref[...]).astype(BF16)
    hq = q_ref.shape[2]
    for i in range(q_ref.shape[0]):
        q_ref[i] = q[:, i * hq:(i + 1) * hq]


def _peerq(x1, mod8, seg_of_block, tm, norm_g, w_q):
    t, d = x1.shape
    nq = w_q.shape[1]
    hq = PEER_QDIM // 2
    tn = _pick_tile(nq, (512, 256, 128))
    return pl.pallas_call(
        _peerq_kernel,
        grid=(t // tm, nq // tn),
        in_specs=[pl.BlockSpec((tm, d), lambda i, j: (i, 0)),
                  pl.BlockSpec((None, 8, d), lambda i, j: (seg_of_block(i), 0, 0)),
                  pl.BlockSpec((1, d), lambda i, j: (0, 0)),
                  pl.BlockSpec((d, tn), lambda i, j: (0, j))],
        out_specs=[pl.BlockSpec((tm, d), lambda i, j: (i, 0)),
                   pl.BlockSpec((tn // hq, tm, hq), lambda i, j: (j, i, 0))],
        out_shape=[jax.ShapeDtypeStruct((t, d), BF16),
                   jax.ShapeDtypeStruct((nq // hq, t, hq), BF16)],
        scratch_shapes=[pltpu.VMEM((tm, d), BF16)],
        compiler_params=_cparams(("parallel", "arbitrary")),
        name="peer_query",
    )(x1, mod8, norm_g.reshape(1, d), w_q)


def _ret_kernel(lg_ref, q_ref, k_ref, v_ref, g_ref, gn_ref, *rest, n_chunks, has_init, has_dst, emit_state):
    rest = list(rest)
    s0_ref = rest.pop(0) if has_init else None
    if has_dst:
        rest.pop(0)
    o_ref = rest.pop(0)
    st_ref = rest.pop(0) if emit_state else None
    dmat_scr, vec_scr = rest[:2]
    if n_chunks > 1:
        o_scr, sf_scr, sb_scr = rest[2:]
    c = RET_CHUNK
    scale = DK_RET ** -0.5
    h = pl.program_id(0)
    lgf = lg_ref[0, h]
    lgb = lg_ref[1, h]

    @pl.when(pl.program_id(1) == 0)
    def _():
        row = lax.broadcasted_iota(jnp.int32, (c, c), 0)
        col = lax.broadcasted_iota(jnp.int32, (c, c), 1)
        dlt = (row - col).astype(F32)
        dmat_scr[...] = (jnp.where(dlt >= 0, jnp.exp(lgf * jnp.maximum(dlt, 0.0)), 0.0)
                         + jnp.where(dlt <= 0, jnp.exp(lgb * jnp.maximum(-dlt, 0.0)), 0.0)) * scale
        pos = lax.broadcasted_iota(jnp.int32, (c, DK_RET), 0).astype(F32)
        vec_scr[0] = jnp.exp(lgf * (pos + 1.0))
        vec_scr[1] = jnp.exp(lgb * (c - pos))
        vec_scr[2] = jnp.exp(lgf * (c - 1.0 - pos)) * scale
        vec_scr[3] = jnp.exp(lgb * pos) * scale

    dmat = dmat_scr[...]
    xi_f = vec_scr[0]
    xi_b = vec_scr[1]
    zeta_f = vec_scr[2]
    zeta_b = vec_scr[3]
    dec_f = jnp.exp(jnp.full((1, DV_RET), lgf, F32) * float(c))
    dec_b = jnp.exp(jnp.full((1, DV_RET), lgb, F32) * float(c))

    def rows(i):
        return pl.ds(pl.multiple_of(i * c, c), c)

    def intra(r):
        q = q_ref[r, :]
        k = k_ref[r, :]
        v = v_ref[r, :]
        s = _nt_dot(q, k) * dmat
        return q.astype(F32), k.astype(F32), v, _dot(s.astype(BF16), v)

    def state_inc(kf, zeta, v):
        return _dot((kf * zeta).T.astype(BF16), v)

    def finish(o, r):
        mu = jnp.mean(o, axis=-1, keepdims=True)
        oc = o - mu
        var = jnp.mean(oc * oc, axis=-1, keepdims=True)
        on = oc * lax.rsqrt(var + EPS)
        o_ref[r, :] = (on * gn_ref[...] * _silu(g_ref[r, :].astype(F32))).astype(o_ref.dtype)

    if n_chunks == 1:
        r = pl.ds(0, c)
        qf, kf, v, o = intra(r)
        s_f = state_inc(kf, zeta_f, v)
        s_b = state_inc(kf, zeta_b, v)
        if has_init:
            s0f = s0_ref[0]
            s0b = s0_ref[1]
            o = o + _dot((qf * xi_f).astype(BF16), s0f.astype(BF16))
            o = o + _dot((qf * xi_b).astype(BF16), s0b.astype(BF16))
            s_f = s_f + dec_f * s0f
            s_b = s_b + dec_b * s0b
        finish(o, r)
        if emit_state:
            st_ref[0] = s_f
            st_ref[1] = s_b
        return

    if has_init:
        sf_scr[...] = s0_ref[0]
        sb_scr[...] = s0_ref[1]
    else:
        sf_scr[...] = jnp.zeros_like(sf_scr)
        sb_scr[...] = jnp.zeros_like(sb_scr)

    def fwd(i, carry):
        r = rows(i)
        qf, kf, v, o = intra(r)
        s = sf_scr[...]
        o_scr[r, :] = o + _dot((qf * xi_f).astype(BF16), s.astype(BF16))
        sf_scr[...] = dec_f * s + state_inc(kf, zeta_f, v)
        return carry

    lax.fori_loop(0, n_chunks, fwd, 0)

    def bwd(ii, carry):
        i = n_chunks - 1 - ii
        r = rows(i)
        qf = q_ref[r, :].astype(F32)
        kf = k_ref[r, :].astype(F32)
        v = v_ref[r, :]
        s = sb_scr[...]
        o = o_scr[r, :] + _dot((qf * xi_b).astype(BF16), s.astype(BF16))
        sb_scr[...] = dec_b * s + state_inc(kf, zeta_b, v)
        finish(o, r)
        return carry

    lax.fori_loop(0, n_chunks, bwd, 0)
    if emit_state:
        st_ref[0] = sf_scr[...]
        st_ref[1] = sb_scr[...]


def _retention(proj, lg, ret_gn, s0, dst, *, n_batch, seq, row0, emit_state):
    n_chunks = seq // RET_CHUNK
    rb0 = row0 // seq
    has_init = s0 is not None
    aliases = {}
    in_specs = [pl.BlockSpec(memory_space=pltpu.SMEM),
                pl.BlockSpec((seq, DK_RET), lambda h, b: (rb0 + b, RQ0 // DK_RET + h)),
                pl.BlockSpec((seq, DK_RET), lambda h, b: (rb0 + b, RK0 // DK_RET + h)),
                pl.BlockSpec((seq, DV_RET), lambda h, b: (rb0 + b, RV0 // DV_RET + h)),
                pl.BlockSpec((seq, DV_RET), lambda h, b: (rb0 + b, RG0 // DV_RET + h)),
                pl.BlockSpec((1, DV_RET), lambda h, b: (0, h))]
    args = [lg, proj, proj, proj, proj, ret_gn.reshape(1, RET_V)]
    if has_init:
        in_specs.append(pl.BlockSpec((None, 2, None, DK_RET, DV_RET), lambda h, b: (b, 0, h, 0, 0)))
        args.append(s0)
    if dst is not None:
        in_specs.append(pl.BlockSpec(memory_space=pl.ANY))
        aliases[len(args)] = 0
        args.append(dst)
    out_specs = [pl.BlockSpec((seq, DV_RET), lambda h, b: (rb0 + b, h))]
    out_shape = [jax.ShapeDtypeStruct((proj.shape[0], RET_V), BF16)]
    if emit_state:
        out_specs.append(pl.BlockSpec((None, 2, None, DK_RET, DV_RET), lambda h, b: (b, 0, h, 0, 0)))
        out_shape.append(jax.ShapeDtypeStruct((n_batch, 2, H_RET, DK_RET, DV_RET), F32))
    scratch = [pltpu.VMEM((RET_CHUNK, RET_CHUNK), F32), pltpu.VMEM((4, RET_CHUNK, DK_RET), F32)]
    if n_chunks > 1:
        scratch += [pltpu.VMEM((seq, DV_RET), F32), pltpu.VMEM((DK_RET, DV_RET), F32),
                    pltpu.VMEM((DK_RET, DV_RET), F32)]
    return pl.pallas_call(
        functools.partial(_ret_kernel, n_chunks=n_chunks, has_init=has_init, has_dst=dst is not None,
                          emit_state=emit_state),
        grid=(H_RET, n_batch),
        in_specs=in_specs,
        out_specs=out_specs,
        out_shape=out_shape,
        input_output_aliases=aliases,
        scratch_shapes=scratch,
        compiler_params=_cparams(("parallel", "arbitrary")),
        name="retention_ctx" if emit_state else "retention_lat",
    )(*args)


def _dn_kernel(par_ref, q_ref, k_ref, v_ref, z_ref, gt_ref, cq_ref, ck_ref, cv_ref, nrm_ref, *rest,
               seq, row_len, heads, has_init, has_dst, emit_state):
    rest = list(rest)
    s0_ref = rest.pop(0) if has_init else None
    if has_dst:
        rest.pop(0)
    o_ref = rest.pop(0)
    st_ref = rest.pop(0) if emit_state else None
    rows_scr, o_scr, s_scr, qkv_scr, decay_scr = rest
    cb = DN_CHUNK
    n_blocks = seq // cb
    hg = pl.program_id(1)

    lane = lax.broadcasted_iota(jnp.int32, (1, seq), 1) & (cb - 1)

    def splat(v):
        return jnp.full((1, seq), v, F32)

    def prefix(x):
        s = 1
        while s < cb:
            x = x + jnp.where(lane >= s, pltpu.roll(x, s, 1), 0.0)
            s *= 2
        return x

    def suffix(x):
        s = 1
        while s < cb:
            x = x + jnp.where(lane < cb - s, pltpu.roll(x, seq - s, 1), 0.0)
            s *= 2
        return x

    assert 2 * heads <= 8
    if 2 * heads < 8:
        decay_scr[2 * heads:8, :] = jnp.zeros((8 - 2 * heads, seq), F32)
    for g in range(heads):
        h = hg * heads + g
        r0 = 8 * g
        decay_scr[2 * g:2 * g + 1, :] = (-jnp.exp(splat(par_ref[0, h]))
                                         * jax.nn.softplus(gt_ref[r0 + 2:r0 + 3, :] + splat(par_ref[2, h])))
        decay_scr[2 * g + 1:2 * g + 2, :] = (-jnp.exp(splat(par_ref[1, h]))
                                             * jax.nn.softplus(gt_ref[r0 + 3:r0 + 4, :] + splat(par_ref[3, h])))
    g_all = decay_scr[...]
    p_all = prefix(g_all)
    s_all = suffix(g_all)
    for g in range(heads):
        r0 = 8 * g
        f, b = slice(2 * g, 2 * g + 1), slice(2 * g + 1, 2 * g + 2)
        rows_scr[r0:r0 + 1, :] = jax.nn.sigmoid(gt_ref[r0:r0 + 1, :])
        rows_scr[r0 + 1:r0 + 2, :] = p_all[f, :]
        rows_scr[r0 + 2:r0 + 3, :] = s_all[f, :] - g_all[f, :]
        rows_scr[r0 + 3:r0 + 4, :] = jax.nn.sigmoid(gt_ref[r0 + 1:r0 + 2, :])
        rows_scr[r0 + 4:r0 + 5, :] = s_all[b, :]
        rows_scr[r0 + 5:r0 + 6, :] = p_all[b, :] - g_all[b, :]
        rows_scr[r0 + 6:r0 + 8, :] = jnp.zeros((2, seq), F32)
        for direction in range(2):
            if has_init:
                s_scr[g, direction] = s0_ref[direction, g]
            else:
                s_scr[g, direction] = jnp.zeros((DK_DN, DV_DN), F32)

    ri = lax.broadcasted_iota(jnp.int32, (cb, cb), 0)
    ci = lax.broadcasted_iota(jnp.int32, (cb, cb), 1)
    eye = (ri == ci).astype(F32)
    pair_masks = []
    sz = 2
    while sz < cb:
        pair = ((ri // (2 * sz)) == (ci // (2 * sz))) & ((ri // sz) != (ci // sz))
        pair_masks.append(jnp.where(pair, 1.0, 0.0).astype(BF16))
        sz *= 2
    n_small = (DN_SLAB // 2).bit_length() - 1
    same_slab = (ri // DN_SLAB) == (ci // DN_SLAB)
    slab_mask16 = jnp.where(same_slab, 1.0, 0.0).astype(BF16)
    tpos = lax.broadcasted_iota(jnp.int32, (cb, DK_DN), 0) & (row_len - 1)

    def conv(x, w_ref, lanes):
        acc = x * w_ref[CONV_K // 2:CONV_K // 2 + 1, lanes]
        for i in range(CONV_K):
            d = i - CONV_K // 2
            if d == 0:
                continue
            sh = pltpu.roll(x, (-d) % cb, 0)
            ok = (tpos + d >= 0) & (tpos + d < row_len)
            acc = acc + jnp.where(ok, sh, 0.0) * w_ref[i:i + 1, lanes]
        return _silu(acc)

    def l2n(x):
        return x * lax.rsqrt(jnp.sum(x * x, axis=-1, keepdims=True) + EPS)

    def col_of(row):
        parts = []
        for t in range(cb // LANES):
            parts.append(jnp.broadcast_to(row[:, t * LANES:(t + 1) * LANES], (LANES, LANES)).T)
        return jnp.concatenate(parts, axis=0)

    def rows_of(bi):
        return pl.ds(pl.multiple_of(bi * cb, cb), cb)

    def prepare(bi, carry):
        r = rows_of(bi)
        for g in range(heads):
            lanes = slice(g * DK_DN, (g + 1) * DK_DN)
            qkv_scr[g, 0, r, :] = l2n(conv(q_ref[r, lanes].astype(F32), cq_ref, lanes)) * (DK_DN ** -0.5)
            qkv_scr[g, 1, r, :] = l2n(conv(k_ref[r, lanes].astype(F32), ck_ref, lanes))
            qkv_scr[g, 2, r, :] = conv(v_ref[r, lanes].astype(F32), cv_ref, lanes)
        return carry

    if n_blocks == 1:
        prepare(0, 0)
    else:
        lax.fori_loop(0, n_blocks, prepare, 0)

    def blocks(chains):
        cs = []
        for g, bi, direction in chains:
            r = rows_of(bi)
            base = 8 * g + (0 if direction == 0 else 3)
            cs.append(dict(g=g, d=direction, r=r, beta_row=rows_scr[base:base + 1, r],
                           g_row=rows_scr[base + 1:base + 2, r], e_row=rows_scr[base + 2:base + 3, r]))
        for c in cs:
            c["q"] = qkv_scr[c["g"], 0, c["r"], :]
            c["k"] = qkv_scr[c["g"], 1, c["r"], :]
            c["v"] = qkv_scr[c["g"], 2, c["r"], :]
            c["k16"] = c["k"].astype(BF16)
            c["kk"] = _nt_dot(c["k16"], c["k16"])
        for c in cs:
            c["beta_c"] = col_of(c["beta_row"])
            c["g_c"] = col_of(c["g_row"])
            c["incl"] = (ri >= ci) if c["d"] == 0 else (ri <= ci)
            strict = (ri > ci) if c["d"] == 0 else (ri < ci)
            g_cw = jnp.concatenate([c["g_c"]] * (cb // LANES), axis=1)
            beta_cw = jnp.concatenate([c["beta_c"]] * (cb // LANES), axis=1)
            c["lmat"] = jnp.exp(jnp.where(c["incl"], g_cw - c["g_row"], NEG_INF))
            c["m"] = jnp.where(strict, c["kk"] * beta_cw * c["lmat"], 0.0)
            c["x"] = eye - jnp.where((ri // 2) == (ci // 2), c["m"], 0.0)
        n_slab = cb // DN_SLAB
        for c in cs:
            c["m16"] = c["m"].astype(BF16)
            c["xc"] = functools.reduce(
                lambda a, b: a + b, [c["x"][p * DN_SLAB:(p + 1) * DN_SLAB, :] for p in range(n_slab)])
        for pair16 in pair_masks[:n_small]:
            for c in cs:
                xc16 = c["xc"].astype(BF16)
                c["xbd16"] = jnp.concatenate([xc16] * n_slab, axis=0) * slab_mask16
                c["t16"] = _dot(xc16, c["m16"] * pair16).astype(BF16)
            for c in cs:
                c["xc"] = c["xc"] - _dot(c["t16"], c["xbd16"])
        for c in cs:
            c["x"] = jnp.where(same_slab, jnp.concatenate([c["xc"]] * n_slab, axis=0), 0.0)
        for pair16 in pair_masks[n_small:]:
            for c in cs:
                c["x16"] = c["x"].astype(BF16)
                c["t16"] = _dot(c["x16"], c["m16"] * pair16).astype(BF16)
            for c in cs:
                c["x"] = c["x"] - _dot(c["t16"], c["x16"])
        for c in cs:
            eg_c = jnp.exp(c["g_c"])
            rhs = jnp.concatenate([c["v"] * c["beta_c"], c["k"] * (c["beta_c"] * eg_c)], axis=1).astype(BF16)
            uw = _dot(c["x"].astype(BF16), rhs)
            c["u"] = uw[:, :DV_DN]
            c["w16"] = uw[:, DV_DN:].astype(BF16)
            c["qg16"] = (c["q"] * eg_c).astype(BF16)
            c["aqk"] = (_nt_dot(c["q"].astype(BF16), c["k16"]) * c["lmat"]).astype(BF16)
            c["kdec_t"] = (c["k"] * jnp.exp(col_of(c["e_row"]))).T.astype(BF16)
        outs = []
        for c in cs:
            s = s_scr[c["g"], c["d"]]
            s16 = s.astype(BF16)
            vn16 = (c["u"] - _dot(c["w16"], s16)).astype(BF16)
            outs.append(_dot(c["qg16"], s16) + _dot(c["aqk"], vn16))
            last = cb - 1 if c["d"] == 0 else 0
            s_scr[c["g"], c["d"]] = s * jnp.exp(c["g_c"][last:last + 1, :]) + _dot(c["kdec_t"], vn16)
        return outs

    def finish(g, bi, d):
        r = rows_of(bi)
        lanes = slice(g * DV_DN, (g + 1) * DV_DN)
        d = d * lax.rsqrt(jnp.mean(d * d, axis=-1, keepdims=True) + EPS)
        o_ref[r, lanes] = (d * nrm_ref[...] * _silu(z_ref[r, lanes].astype(F32))).astype(o_ref.dtype)

    if n_blocks == 1:
        outs = blocks([(g, 0, d) for g in range(heads) for d in range(2)])
        for g in range(heads):
            finish(g, 0, outs[2 * g] + outs[2 * g + 1])
    else:
        per = 2 if n_blocks % 4 == 0 else 1

        def chains_of(it):
            return [(g, per * it + j if d == 0 else n_blocks - 1 - per * it - j, d)
                    for g in range(heads) for d in range(2) for j in range(per)]

        def first_half(it, carry):
            chains = chains_of(it)
            for (g, bi, _), out in zip(chains, blocks(chains)):
                o_scr[g, rows_of(bi), :] = out
            return carry

        def second_half(it, carry):
            chains = chains_of(it)
            for (g, bi, _), out in zip(chains, blocks(chains)):
                finish(g, bi, out + o_scr[g, rows_of(bi), :])
            return carry

        n_it = n_blocks // per
        lax.fori_loop(0, n_it // 2, first_half, 0)
        lax.fori_loop(n_it // 2, n_it, second_half, 0)
    if emit_state:
        for g in range(heads):
            st_ref[0, g] = s_scr[g, 0]
            st_ref[1, g] = s_scr[g, 1]


def _deltanet(proj, gates_t, dn_par, dn_conv, dn_norm, s0, dst, *, n_batch, seq, row0, row_len, heads,
              emit_state):
    rb0 = row0 // seq
    has_init = s0 is not None
    aliases = {}
    wd = heads * DK_DN
    in_specs = [pl.BlockSpec(memory_space=pltpu.SMEM),
                pl.BlockSpec((seq, wd), lambda b, h: (rb0 + b, DQ0 // wd + h)),
                pl.BlockSpec((seq, wd), lambda b, h: (rb0 + b, DK0 // wd + h)),
                pl.BlockSpec((seq, wd), lambda b, h: (rb0 + b, DV0 // wd + h)),
                pl.BlockSpec((seq, wd), lambda b, h: (rb0 + b, DZ0 // wd + h)),
                pl.BlockSpec((8 * heads, seq), lambda b, h: (h, rb0 + b)),
                pl.BlockSpec((CONV_K, wd), lambda b, h: (0, h)),
                pl.BlockSpec((CONV_K, wd), lambda b, h: (0, DN_QK // wd + h)),
                pl.BlockSpec((CONV_K, wd), lambda b, h: (0, 2 * DN_QK // wd + h)),
                pl.BlockSpec((1, DV_DN), lambda b, h: (0, 0))]
    args = [dn_par, proj, proj, proj, proj, gates_t, dn_conv, dn_conv, dn_conv, dn_norm.reshape(1, DV_DN)]
    if has_init:
        in_specs.append(pl.BlockSpec((None, 2, heads, DK_DN, DV_DN), lambda b, h: (b, 0, h, 0, 0)))
        args.append(s0)
    if dst is not None:
        in_specs.append(pl.BlockSpec(memory_space=pl.ANY))
        aliases[len(args)] = 0
        args.append(dst)
    out_specs = [pl.BlockSpec((seq, wd), lambda b, h: (rb0 + b, h))]
    out_shape = [jax.ShapeDtypeStruct((proj.shape[0], DN_V), BF16)]
    if emit_state:
        out_specs.append(pl.BlockSpec((None, 2, heads, DK_DN, DV_DN), lambda b, h: (b, 0, h, 0, 0)))
        out_shape.append(jax.ShapeDtypeStruct((n_batch, 2, H_DN, DK_DN, DV_DN), F32))
    scratch = [pltpu.VMEM((8 * heads, seq), F32), pltpu.VMEM((heads, seq, DV_DN), F32),
               pltpu.VMEM((heads, 2, DK_DN, DV_DN), F32), pltpu.VMEM((heads, 3, seq, DK_DN), F32),
               pltpu.VMEM((8, seq), F32)]
    n_blocks = seq // DN_CHUNK
    assert seq % DN_CHUNK == 0 and (n_blocks == 1 or n_blocks % 2 == 0)
    assert DN_CHUNK % row_len == 0 and row_len & (row_len - 1) == 0 and H_DN % heads == 0
    return pl.pallas_call(
        functools.partial(_dn_kernel, seq=seq, row_len=row_len, heads=heads, has_init=has_init,
                          has_dst=dst is not None, emit_state=emit_state),
        grid=(n_batch, H_DN // heads),
        in_specs=in_specs,
        out_specs=out_specs,
        out_shape=out_shape,
        input_output_aliases=aliases,
        scratch_shapes=scratch,
        compiler_params=_cparams(("parallel", "arbitrary")),
        name="deltanet_ctx" if emit_state else "deltanet_lat",
    )(*args)


def _sort_network(n):
    pairs = []
    k = 2
    while k <= n:
        j = k // 2
        while j >= 1:
            for i in range(n):
                l = i ^ j
                if l > i:
                    pairs.append((i, l) if (i & k) == 0 else (l, i))
            j //= 2
        k *= 2
    return pairs


def _cand_pairs():
    n = PEER_TOPK + 1
    return [(a, b) for a in range(n) for b in range(n) if (a + 1) * (b + 1) <= n]


def _peer_score_kernel(q_ref, sk_ref, pb_ref, pf_ref, s_scr, sv_scr, gp_scr):
    def top_values_sorted(s, p, h):
        n_rows = 8
        v = [s[j * n_rows:(j + 1) * n_rows, :] for j in range(N_KEYS // n_rows)]
        for a, b in _sort_network(len(v)):
            v[a], v[b] = jnp.maximum(v[a], v[b]), jnp.minimum(v[a], v[b])
        for r in range(PEER_TOPK + 1):
            m = jnp.max(v[0], axis=0, keepdims=True)
            sv_scr[p, r, pl.ds(h, 1), :] = m
            hit = v[0] == m
            live = min(len(v), PEER_TOPK + 1 - r)
            for j in range(live - 1):
                v[j] = jnp.where(hit, v[j + 1], v[j])
            if live == len(v):
                v[live - 1] = jnp.where(hit, NEG_INF, v[live - 1])

    def per_head(h, carry):
        for p in range(2):
            s = _nt_dot(sk_ref[p], q_ref[2 * h + p])
            s_scr[p, h] = s
            top_values_sorted(s, p, h)
        rank = jnp.zeros(s.shape, F32)
        for r in range(PEER_TOPK):
            rank = rank + jnp.where(sv_scr[1, r, pl.ds(h, 1), :] > s, 1.0, 0.0)
        pb_ref[0, h] = rank.astype(BF16)
        return carry

    lax.fori_loop(0, PEER_HEADS, per_head, 0)
    cands = [sv_scr[0, a] + sv_scr[1, b] for a, b in _cand_pairs()]
    mx = cands[0]
    z = jnp.zeros_like(mx)
    for r in range(PEER_TOPK):
        m = functools.reduce(jnp.maximum, cands)
        z = z + jnp.exp(m - mx)
        cands = [jnp.where(c == m, NEG_INF, c) for c in cands]
    runner_up = functools.reduce(jnp.maximum, cands)
    gp_scr[0] = 0.5 * (m + runner_up)
    gp_scr[1] = 1.0 / z

    def finish_head(h, carry):
        row = pl.ds(h, 1)
        s1 = s_scr[0, h]
        pb_ref[1, h] = jnp.exp(s_scr[1, h] - sv_scr[1, 0, row, :]).astype(BF16)
        pf_ref[0, h] = jnp.exp(s1 - sv_scr[0, 0, row, :]) * gp_scr[1, row, :]
        need = gp_scr[0, row, :] - s1
        cnt = jnp.zeros_like(need)
        for r in range(PEER_TOPK):
            cnt = cnt + jnp.where(sv_scr[1, r, row, :] >= need, 1.0, 0.0)
        pf_ref[1, h] = cnt
        return carry

    lax.fori_loop(0, PEER_HEADS, finish_head, 0)


def _peer_scores(q3, subkeys, tb):
    nhp, t, _ = q3.shape
    return pl.pallas_call(
        _peer_score_kernel,
        grid=(t // tb,),
        in_specs=[pl.BlockSpec((nhp, tb, PEER_QDIM // 2), lambda i: (0, i, 0)),
                  pl.BlockSpec((2, N_KEYS, PEER_QDIM // 2), lambda i: (0, 0, 0))],
        out_specs=[pl.BlockSpec((2, PEER_HEADS, N_KEYS, tb), lambda i: (0, 0, 0, i)),
                   pl.BlockSpec((2, PEER_HEADS, N_KEYS, tb), lambda i: (0, 0, 0, i))],
        out_shape=[jax.ShapeDtypeStruct((2, PEER_HEADS, N_KEYS, t), BF16),
                   jax.ShapeDtypeStruct((2, PEER_HEADS, N_KEYS, t), F32)],
        scratch_shapes=[pltpu.VMEM((2, PEER_HEADS, N_KEYS, tb), F32),
                        pltpu.VMEM((2, PEER_TOPK + 1, PEER_HEADS, tb), F32),
                        pltpu.VMEM((2, PEER_HEADS, tb), F32)],
        compiler_params=_cparams(("parallel",)),
        name="peer_scores",
    )(q3, subkeys)


def _gelu_tanh(x):
    c = 0.7978845608028654
    h = 0.5 * x
    return h + h * jnp.tanh(x * (c + (c * 0.044715) * (x * x)))


def _peer_dense_kernel(h_ref, pb_ref, pf_ref, u_ref, vt_ref, o_ref, *, n_i1):
    e = pl.program_id(1)
    zero = jnp.zeros((), BF16)
    tb, d = h_ref.shape
    kt = V7X_MXU_DIM
    per_m = kt // N_KEYS

    @pl.when(e == 0)
    def _():
        o_ref[...] = jnp.zeros_like(o_ref)

    def gate_head(ii, h):
        row = pl.ds(e * n_i1 + ii, 1)

        def bcast(x):
            x = jnp.broadcast_to(x, (16, x.shape[1])).astype(BF16)
            return jnp.concatenate([x] * (N_KEYS // 16), axis=0)

        e1 = bcast(pf_ref[0, h, row, :])
        cnt = bcast(pf_ref[1, h, row, :])
        return jnp.where(pb_ref[0, h] < cnt, pb_ref[1, h] * e1, zero)

    wa_parts = []
    for m in range(n_i1 // per_m):
        todo = [(per_m * m + j, h) for j in range(per_m) for h in range(PEER_HEADS)]
        gate = {}

        def one_gate():
            ii, h = todo.pop(0)
            g = gate_head(ii, h)
            gate[ii] = g if h == 0 else gate[ii] + g

        accs = [None] * (tb // kt)
        for k in range(d // kt):
            for n in range(tb // kt):
                part = _nt_dot(u_ref[m * kt:(m + 1) * kt, k * kt:(k + 1) * kt],
                               h_ref[n * kt:(n + 1) * kt, k * kt:(k + 1) * kt])
                accs[n] = part if accs[n] is None else accs[n] + part
                if todo:
                    one_gate()
        while todo:
            one_gate()
        act = jnp.concatenate([_gelu_tanh(a).astype(BF16) for a in accs], axis=1)
        w = jnp.concatenate([gate[per_m * m + j] for j in range(per_m)], axis=0)
        wa_parts.append(w * act)
    o_ref[...] += _dot(vt_ref[...], jnp.concatenate(wa_parts, axis=0))


def _peer_dense(h2, pb, pf, u_tab, v_tab_t, tb, et):
    t, d = h2.shape
    ne = u_tab.shape[0]
    once = pl.Buffered(1)
    return pl.pallas_call(
        functools.partial(_peer_dense_kernel, n_i1=et // N_KEYS),
        grid=(t // tb, ne // et),
        in_specs=[pl.BlockSpec((tb, d), lambda i, e: (i, 0), pipeline_mode=once),
                  pl.BlockSpec((2, PEER_HEADS, N_KEYS, tb), lambda i, e: (0, 0, 0, i), pipeline_mode=once),
                  pl.BlockSpec((2, PEER_HEADS, N_KEYS, tb), lambda i, e: (0, 0, 0, i), pipeline_mode=once),
                  pl.BlockSpec((et, d), lambda i, e: (e, 0)),
                  pl.BlockSpec((d, et), lambda i, e: (0, e))],
        out_specs=pl.BlockSpec((d, tb), lambda i, e: (0, i)),
        out_shape=jax.ShapeDtypeStruct((d, t), F32),
        compiler_params=_cparams(("parallel", "arbitrary")),
        name="peer_dense",
    )(h2, pb, pf, u_tab, v_tab_t)


def _final_kernel(x_ref, p_ref, mod_ref, g_ref, oc_ref, ol_ref, *, n_ctx_blocks):
    x = x_ref[...] + mod_ref[5:6, :] * p_ref[...].T
    ms = jnp.mean(x * x, axis=-1, keepdims=True)
    y = (x * lax.rsqrt(ms + EPS)) * g_ref[...]
    i = pl.program_id(0)

    @pl.when(i < n_ctx_blocks)
    def _():
        oc_ref[...] = y

    @pl.when(i >= n_ctx_blocks)
    def _():
        ol_ref[...] = y


def _final(x1, peer_t, mod8, seg_of_block, tm_seg, final_norm, tc):
    t, d = x1.shape
    tm = FINAL_TILE
    sub = tm_seg // tm
    nc = tc // tm
    return pl.pallas_call(
        functools.partial(_final_kernel, n_ctx_blocks=nc),
        grid=(t // tm,),
        in_specs=[pl.BlockSpec((tm, d), lambda i: (i, 0)),
                  pl.BlockSpec((d, tm), lambda i: (0, i)),
                  pl.BlockSpec((None, 8, d), lambda i: (seg_of_block(i // sub), 0, 0)),
                  pl.BlockSpec((1, d), lambda i: (0, 0))],
        out_specs=[pl.BlockSpec((tm, d), lambda i: (jnp.minimum(i, nc - 1), 0)),
                   pl.BlockSpec((tm, d), lambda i: (jnp.maximum(i - nc, 0), 0))],
        out_shape=[jax.ShapeDtypeStruct((tc, d), F32), jax.ShapeDtypeStruct((t - tc, d), F32)],
        compiler_params=_cparams(("arbitrary",)),
        name="final_norm",
    )(x1, peer_t, mod8, final_norm.reshape(1, d))


def kernel(x_prompt, x_sample, state_ret, state_dn, c, c_ctx, w_ada, b_ada, norm_mix, norm_ffn, w_in, ret_logit,
           ret_gn, dn_conv, dn_a_log, dn_dt_bias, dn_norm, w_br_a, w_br_b, w_out, peer_wq, peer_subkeys, peer_u,
           peer_v, final_norm):
    bc, lc, d = x_prompt.shape
    bl, ll, _ = x_sample.shape
    depth = w_ada.shape[0]
    tc = bc * lc
    t = tc + bl * ll
    assert bl + 1 <= 8 and tc % ll == 0

    tm = _pick_tile(math.gcd(tc, ll), (1024, 512, 256))
    n_ctx_blocks = tc // tm
    lat_blocks = ll // tm

    def seg_of_block(i):
        return jnp.where(i < n_ctx_blocks, 0, 1 + (i - n_ctx_blocks) // lat_blocks)

    xc = x_prompt.reshape(tc, d)
    xl = x_sample.reshape(bl * ll, d)
    cvec = jnp.concatenate([c_ctx[None, :], c, jnp.zeros((8 - 1 - bl, d), F32)], axis=0)

    ret_states, dn_states = [], []
    for l in range(depth):
        mod = _adaln(cvec, w_ada[l], b_ada[l])
        mod8 = jnp.pad(mod.reshape(8, 6, d), ((0, 0), (0, 2), (0, 0)))

        w = w_in[l].astype(BF16)
        wg = w[:, DGL0:DGL0 + 4 * H_DN].reshape(d, 4, H_DN).transpose(2, 1, 0)
        w_gate_t = jnp.pad(wg, ((0, 0), (0, 4), (0, 0))).reshape(8 * H_DN, d)

        proj, gates_t = _inproj(xc, xl, mod8, seg_of_block, tm, norm_mix[l], w, w[:, DGL0 + 4 * H_DN:], w_gate_t)

        lg = jax.nn.log_sigmoid(ret_logit[l].astype(F32))
        ret_c, rs = _retention(proj, lg, ret_gn[l], None, jnp.zeros((t, RET_V), BF16), n_batch=bc, seq=lc, row0=0,
                               emit_state=True)
        (ret_out,) = _retention(proj, lg, ret_gn[l], state_ret[:, l], ret_c, n_batch=bl, seq=ll, row0=tc,
                                emit_state=False)
        dn_par = jnp.concatenate([dn_a_log[l], dn_dt_bias[l]], axis=0).astype(F32)
        dn_c, ds = _deltanet(proj, gates_t, dn_par, dn_conv[l], dn_norm[l], None, jnp.zeros((t, DN_V), BF16),
                             n_batch=bc, seq=lc, row0=0, row_len=lc, heads=4, emit_state=True)
        (dn_out,) = _deltanet(proj, gates_t, dn_par, dn_conv[l], dn_norm[l], state_dn[:, l], dn_c, n_batch=bl,
                              seq=ll, row0=tc, row_len=GRID_W, heads=2, emit_state=False)
        ret_states.append(rs)
        dn_states.append(ds)

        merged = _merge(ret_out, dn_out, w_br_a[l].astype(BF16), w_br_b[l].astype(BF16), proj, d, tm)
        x1 = _outproj(merged, w_out[l].astype(BF16), xc, xl, mod8, seg_of_block, tm)

        h2, q3 = _peerq(x1, mod8, seg_of_block, tm, norm_ffn[l], peer_wq[l].astype(BF16))
        tb = min(PEER_TOKEN_TILE, tm)
        pb, pf = _peer_scores(q3, peer_subkeys[l].astype(BF16), tb)
        peer_t = _peer_dense(h2, pb, pf, peer_u[l].astype(BF16), peer_v[l].astype(BF16).T, tb,
                             PEER_EXPERT_TILE)
        assert depth == 1

    y_c, y_l = _final(x1, peer_t, mod8, seg_of_block, tm, final_norm, tc)
    y_prompt = y_c.reshape(bc, lc, d)
    y_sample = y_l.reshape(bl, ll, d)
    new_state_ret = jnp.stack(ret_states, axis=1)
    new_state_dn = jnp.stack(dn_states, axis=1)
    return (y_prompt, y_sample, new_state_ret, new_state_dn)
```

```python
import functools
import math

import jax
import jax.numpy as jnp
from jax import lax
from jax.experimental import pallas as pl
from jax.experimental.pallas import tpu as pltpu

F32 = jnp.float32
BF16 = jnp.bfloat16

H_RET, DK_RET, DV_RET = 8, 128, 256
H_DN, DK_DN, DV_DN = 16, 128, 128
CONV_K = 5
GRID_W = 64
DN_CHUNK = 256
DN_SLAB = 64
RET_CHUNK = 256
N_KEYS = 128
PEER_HEADS = 8
PEER_QDIM = 256
PEER_TOPK = 16
EPS = 1e-6

RET_QK = H_RET * DK_RET
RET_V = H_RET * DV_RET
DN_QK = H_DN * DK_DN
DN_V = H_DN * DV_DN
RQ0 = 0
RK0 = RQ0 + RET_QK
RV0 = RK0 + RET_QK
RG0 = RV0 + RET_V
DQ0 = RG0 + RET_V
DK0 = DQ0 + DN_QK
DV0 = DK0 + DN_QK
DZ0 = DV0 + DN_V
DGL0 = DZ0 + DN_V
BG0 = DZ0 + DN_V

V7X_VMEM_BYTES = 64 * 1024 * 1024
VMEM_LIMIT = V7X_VMEM_BYTES - 8 * 1024 * 1024
V7X_MXU_DIM = 256
LANES = 128
PEER_TOKEN_TILE = 512
PEER_EXPERT_TILE = 1024
FINAL_TILE = 512
NEG_INF = float("-inf")


def _cparams(sem):
    return pltpu.CompilerParams(dimension_semantics=sem, vmem_limit_bytes=VMEM_LIMIT)


def _nt_dot(a, b):
    return lax.dot_general(a, b, (((1,), (1,)), ((), ())), preferred_element_type=F32)


def _dot(a, b):
    return jnp.dot(a, b, preferred_element_type=F32)


def _silu(x):
    h = 0.5 * x
    return h + h * jnp.tanh(h)


def _pick_tile(n, cands):
    for c in cands:
        if n % c == 0:
            return c
    raise ValueError(f"no tile for {n}")


def _adaln_kernel(c_ref, w_ref, b_ref, o_ref):
    a = _silu(c_ref[...])
    o_ref[...] = jnp.dot(a, w_ref[...], precision=lax.Precision.HIGHEST,
                         preferred_element_type=F32) + b_ref[...]


def _adaln(cvec, w_ada, b_ada):
    d, n = w_ada.shape
    tn = _pick_tile(n, (1024, 512, 256, 128))
    return pl.pallas_call(
        _adaln_kernel,
        grid=(n // tn,),
        in_specs=[pl.BlockSpec((8, d), lambda j: (0, 0)),
                  pl.BlockSpec((d, tn), lambda j: (0, j)),
                  pl.BlockSpec((1, tn), lambda j: (0, j))],
        out_specs=pl.BlockSpec((8, tn), lambda j: (0, j)),
        out_shape=jax.ShapeDtypeStruct((8, n), F32),
        compiler_params=_cparams(("parallel",)),
        name="adaln",
    )(cvec, w_ada, b_ada.reshape(1, n))


def _modulated_norm(x, g, shift, scale):
    ms = jnp.mean(x * x, axis=-1, keepdims=True)
    return (x * lax.rsqrt(ms + EPS)) * g * (1.0 + scale) + shift


def _inproj_kernel(xc_ref, xl_ref, mod_ref, g_ref, wa_ref, wb_ref, wg_ref, o_ref, gt_ref, h_scr, *,
                   n_ctx_blocks, n_head_tiles):
    i = pl.program_id(0)
    j = pl.program_id(1)

    def prepare(x_ref):
        h = _modulated_norm(x_ref[...], g_ref[...], mod_ref[0:1, :], mod_ref[1:2, :]).astype(BF16)
        h_scr[...] = h
        gt_ref[...] = _nt_dot(wg_ref[...], h)

    pl.when((j == 0) & (i < n_ctx_blocks))(functools.partial(prepare, xc_ref))
    pl.when((j == 0) & (i >= n_ctx_blocks))(functools.partial(prepare, xl_ref))

    @pl.when(j < n_head_tiles)
    def _():
        o_ref[...] = _dot(h_scr[...], wa_ref[...]).astype(o_ref.dtype)

    @pl.when(j >= n_head_tiles)
    def _():
        o_ref[...] = _dot(h_scr[...], wb_ref[...]).astype(o_ref.dtype)


def _inproj(xc, xl, mod8, seg_of_block, tm, norm_g, w_all, w_tail, w_gate_t):
    d = xc.shape[1]
    t = xc.shape[0] + xl.shape[0]
    nm = BG0 + w_tail.shape[1]
    tn = _pick_tile(math.gcd(BG0, w_tail.shape[1]), (1024, 512, 256))
    nc = xc.shape[0] // tm
    nh = BG0 // tn
    once = pl.Buffered(1)
    return pl.pallas_call(
        functools.partial(_inproj_kernel, n_ctx_blocks=nc, n_head_tiles=nh),
        grid=(t // tm, nm // tn),
        in_specs=[pl.BlockSpec((tm, d), lambda i, j: (jnp.minimum(i, nc - 1), 0), pipeline_mode=once),
                  pl.BlockSpec((tm, d), lambda i, j: (jnp.maximum(i - nc, 0), 0), pipeline_mode=once),
                  pl.BlockSpec((None, 8, d), lambda i, j: (seg_of_block(i), 0, 0)),
                  pl.BlockSpec((1, d), lambda i, j: (0, 0)),
                  pl.BlockSpec((d, tn), lambda i, j: (0, jnp.minimum(j, nh - 1))),
                  pl.BlockSpec((d, tn), lambda i, j: (0, jnp.maximum(j - nh, 0))),
                  pl.BlockSpec((8 * H_DN, d), lambda i, j: (0, 0))],
        out_specs=[pl.BlockSpec((tm, tn), lambda i, j: (i, j)),
                   pl.BlockSpec((8 * H_DN, tm), lambda i, j: (0, i))],
        out_shape=[jax.ShapeDtypeStruct((t, nm), BF16),
                   jax.ShapeDtypeStruct((8 * H_DN, t), F32)],
        scratch_shapes=[pltpu.VMEM((tm, d), BF16)],
        compiler_params=_cparams(("parallel", "arbitrary")),
        name="inproj",
    )(xc, xl, mod8, norm_g.reshape(1, d), w_all, w_tail, w_gate_t)


def _merge_kernel(a_ref, b_ref, wa_ref, wb_ref, ga_ref, gb_ref, o_ref):
    ya = _dot(a_ref[...], wa_ref[...])
    yb = _dot(b_ref[...], wb_ref[...])
    ga = jax.nn.sigmoid(ga_ref[...].astype(F32))
    gb = jax.nn.sigmoid(gb_ref[...].astype(F32))
    o_ref[...] = (ga * ya + gb * yb).astype(o_ref.dtype)


def _merge(ret_out, dn_out, w_a, w_b, proj, d, tm):
    t = ret_out.shape[0]
    tn = _pick_tile(d, (1024, 512, 256, 128))
    ga0 = BG0 // tn
    gb0 = (BG0 + d) // tn
    return pl.pallas_call(
        _merge_kernel,
        grid=(t // tm, d // tn),
        in_specs=[pl.BlockSpec((tm, RET_V), lambda i, j: (i, 0)),
                  pl.BlockSpec((tm, DN_V), lambda i, j: (i, 0)),
                  pl.BlockSpec((RET_V, tn), lambda i, j: (0, j)),
                  pl.BlockSpec((DN_V, tn), lambda i, j: (0, j)),
                  pl.BlockSpec((tm, tn), lambda i, j: (i, ga0 + j)),
                  pl.BlockSpec((tm, tn), lambda i, j: (i, gb0 + j))],
        out_specs=pl.BlockSpec((tm, tn), lambda i, j: (i, j)),
        out_shape=jax.ShapeDtypeStruct((t, d), BF16),
        compiler_params=_cparams(("parallel", "arbitrary")),
        name="merge",
    )(ret_out, dn_out, w_a, w_b, proj, proj)


def _outproj_kernel(m_ref, w_ref, xc_ref, xl_ref, mod_ref, o_ref, *, n_ctx_blocks):
    y = _dot(m_ref[...], w_ref[...])
    x = jnp.where(pl.program_id(0) < n_ctx_blocks, xc_ref[...], xl_ref[...])
    o_ref[...] = x + mod_ref[2:3, :] * y


def _outproj(merged, w_out, xc, xl, mod8, seg_of_block, tm):
    t, d = merged.shape
    tn = _pick_tile(d, (1024, 512, 256, 128))
    nc = xc.shape[0] // tm
    return pl.pallas_call(
        functools.partial(_outproj_kernel, n_ctx_blocks=nc),
        grid=(t // tm, d // tn),
        in_specs=[pl.BlockSpec((tm, d), lambda i, j: (i, 0)),
                  pl.BlockSpec((d, tn), lambda i, j: (0, j)),
                  pl.BlockSpec((tm, tn), lambda i, j: (jnp.minimum(i, nc - 1), jnp.where(i < nc, j, 0))),
                  pl.BlockSpec((tm, tn), lambda i, j: (jnp.maximum(i - nc, 0), jnp.where(i < nc, 0, j))),
                  pl.BlockSpec((None, 8, tn), lambda i, j: (seg_of_block(i), 0, j))],
        out_specs=pl.BlockSpec((tm, tn), lambda i, j: (i, j)),
        out_shape=jax.ShapeDtypeStruct((t, d), F32),
        compiler_params=_cparams(("parallel", "arbitrary")),
        name="outproj",
    )(merged, w_out, xc, xl, mod8)


def _peerq_kernel(x_ref, mod_ref, g_ref, w_ref, h_ref, q_ref, h_scr):
    @pl.when(pl.program_id(1) == 0)
    def _():
        h = _modulated_norm(x_ref[...], g_ref[...], mod_ref[3:4, :], mod_ref[4:5, :]).astype(BF16)
        h_scr[...] = h
        h_ref[...] = h

    q = _dot(h_scr[...], w_ref[...]).astype(BF16)
    hq = q_ref.shape[2]
    for i in range(q_ref.shape[0]):
        q_ref[i] = q[:, i * hq:(i + 1) * hq]


def _peerq(x1, mod8, seg_of_block, tm, norm_g, w_q):
    t, d = x1.shape
    nq = w_q.shape[1]
    hq = PEER_QDIM // 2
    tn = _pick_tile(nq, (1024, 512, 256, 128))
    return pl.pallas_call(
        _peerq_kernel,
        grid=(t // tm, nq // tn),
        in_specs=[pl.BlockSpec((tm, d), lambda i, j: (i, 0)),
                  pl.BlockSpec((None, 8, d), lambda i, j: (seg_of_block(i), 0, 0)),
                  pl.BlockSpec((1, d), lambda i, j: (0, 0)),
                  pl.BlockSpec((d, tn), lambda i, j: (0, j))],
        out_specs=[pl.BlockSpec((tm, d), lambda i, j: (i, 0)),
                   pl.BlockSpec((tn // hq, tm, hq), lambda i, j: (j, i, 0))],
        out_shape=[jax.ShapeDtypeStruct((t, d), BF16),
                   jax.ShapeDtypeStruct((nq // hq, t, hq), BF16)],
        scratch_shapes=[pltpu.VMEM((tm, d), BF16)],
        compiler_params=_cparams(("parallel", "arbitrary")),
        name="peer_query",
    )(x1, mod8, norm_g.reshape(1, d), w_q)


def _ret_kernel(lg_ref, q_ref, k_ref, v_ref, g_ref, gn_ref, *rest, n_chunks, n_seq, has_init, has_dst,
                emit_state):
    rest = list(rest)
    s0_ref = rest.pop(0) if has_init else None
    if has_dst:
        rest.pop(0)
    o_ref = rest.pop(0)
    st_ref = rest.pop(0) if emit_state else None
    dmat_scr, vec_scr = rest[:2]
    if n_chunks > 1:
        o_scr, sf_scr, sb_scr = rest[2:]
    c = RET_CHUNK
    scale = DK_RET ** -0.5
    h = pl.program_id(0)
    lgf = lg_ref[0, h]
    lgb = lg_ref[1, h]

    @pl.when(pl.program_id(1) == 0)
    def _():
        row = lax.broadcasted_iota(jnp.int32, (c, c), 0)
        col = lax.broadcasted_iota(jnp.int32, (c, c), 1)
        dlt = (row - col).astype(F32)
        dmat_scr[...] = (jnp.where(dlt >= 0, jnp.exp(lgf * jnp.maximum(dlt, 0.0)), 0.0)
                         + jnp.where(dlt <= 0, jnp.exp(lgb * jnp.maximum(-dlt, 0.0)), 0.0)) * scale
        pos = lax.broadcasted_iota(jnp.int32, (c, DK_RET), 0).astype(F32)
        vec_scr[0] = jnp.exp(lgf * (pos + 1.0))
        vec_scr[1] = jnp.exp(lgb * (c - pos))
        vec_scr[2] = jnp.exp(lgf * (c - 1.0 - pos)) * scale
        vec_scr[3] = jnp.exp(lgb * pos) * scale

    dmat = dmat_scr[...]
    xi_f = vec_scr[0]
    xi_b = vec_scr[1]
    zeta_f = vec_scr[2]
    zeta_b = vec_scr[3]
    dec_f = jnp.exp(jnp.full((1, DV_RET), lgf, F32) * float(c))
    dec_b = jnp.exp(jnp.full((1, DV_RET), lgb, F32) * float(c))

    def rows(i):
        return pl.ds(pl.multiple_of(i * c, c), c)

    def intra(r):
        q = q_ref[r, :]
        k = k_ref[r, :]
        v = v_ref[r, :]
        s = _nt_dot(q, k) * dmat
        return q.astype(F32), k.astype(F32), v, _dot(s.astype(BF16), v)

    def state_inc(kf, zeta, v):
        return _dot((kf * zeta).T.astype(BF16), v)

    def finish(o, r):
        mu = jnp.mean(o, axis=-1, keepdims=True)
        oc = o - mu
        var = jnp.mean(oc * oc, axis=-1, keepdims=True)
        on = oc * lax.rsqrt(var + EPS)
        o_ref[r, :] = (on * gn_ref[...] * _silu(g_ref[r, :].astype(F32))).astype(o_ref.dtype)

    if n_chunks == 1:
        for sq in range(n_seq):
            r = pl.ds(sq * c, c)
            qf, kf, v, o = intra(r)
            s_f = state_inc(kf, zeta_f, v)
            s_b = state_inc(kf, zeta_b, v)
            if has_init:
                s0f = s0_ref[0]
                s0b = s0_ref[1]
                o = o + _dot((qf * xi_f).astype(BF16), s0f.astype(BF16))
                o = o + _dot((qf * xi_b).astype(BF16), s0b.astype(BF16))
                s_f = s_f + dec_f * s0f
                s_b = s_b + dec_b * s0b
            finish(o, r)
            if emit_state:
                st_ref[sq, 0] = s_f
                st_ref[sq, 1] = s_b
        return

    if has_init:
        sf_scr[...] = s0_ref[0]
        sb_scr[...] = s0_ref[1]
    else:
        sf_scr[...] = jnp.zeros_like(sf_scr)
        sb_scr[...] = jnp.zeros_like(sb_scr)

    def fwd(i, carry):
        r = rows(i)
        qf, kf, v, o = intra(r)
        s = sf_scr[...]
        o_scr[r, :] = o + _dot((qf * xi_f).astype(BF16), s.astype(BF16))
        sf_scr[...] = dec_f * s + state_inc(kf, zeta_f, v)
        return carry

    lax.fori_loop(0, n_chunks, fwd, 0)

    def bwd(ii, carry):
        i = n_chunks - 1 - ii
        r = rows(i)
        qf = q_ref[r, :].astype(F32)
        kf = k_ref[r, :].astype(F32)
        v = v_ref[r, :]
        s = sb_scr[...]
        o = o_scr[r, :] + _dot((qf * xi_b).astype(BF16), s.astype(BF16))
        sb_scr[...] = dec_b * s + state_inc(kf, zeta_b, v)
        finish(o, r)
        return carry

    lax.fori_loop(0, n_chunks, bwd, 0)
    if emit_state:
        st_ref[0, 0] = sf_scr[...]
        st_ref[0, 1] = sb_scr[...]


def _retention(proj, lg, ret_gn, s0, dst, *, n_batch, seq, row0, emit_state):
    n_chunks = seq // RET_CHUNK
    has_init = s0 is not None
    n_seq = 1
    if n_chunks == 1 and not has_init:
        n_seq = next(n for n in (4, 2, 1) if n_batch % n == 0 and row0 % (n * seq) == 0)
    rows = seq * n_seq
    rb0 = row0 // rows
    aliases = {}
    in_specs = [pl.BlockSpec(memory_space=pltpu.SMEM),
                pl.BlockSpec((rows, DK_RET), lambda h, b: (rb0 + b, RQ0 // DK_RET + h)),
                pl.BlockSpec((rows, DK_RET), lambda h, b: (rb0 + b, RK0 // DK_RET + h)),
                pl.BlockSpec((rows, DV_RET), lambda h, b: (rb0 + b, RV0 // DV_RET + h)),
                pl.BlockSpec((rows, DV_RET), lambda h, b: (rb0 + b, RG0 // DV_RET + h)),
                pl.BlockSpec((1, DV_RET), lambda h, b: (0, h))]
    args = [lg, proj, proj, proj, proj, ret_gn.reshape(1, RET_V)]
    if has_init:
        in_specs.append(pl.BlockSpec((None, 2, None, DK_RET, DV_RET), lambda h, b: (b, 0, h, 0, 0)))
        args.append(s0)
    if dst is not None:
        in_specs.append(pl.BlockSpec(memory_space=pl.ANY))
        aliases[len(args)] = 0
        args.append(dst)
    out_specs = [pl.BlockSpec((rows, DV_RET), lambda h, b: (rb0 + b, h))]
    out_shape = [jax.ShapeDtypeStruct((proj.shape[0], RET_V), BF16)]
    if emit_state:
        out_specs.append(pl.BlockSpec((n_seq, 2, None, DK_RET, DV_RET), lambda h, b: (b, 0, h, 0, 0)))
        out_shape.append(jax.ShapeDtypeStruct((n_batch, 2, H_RET, DK_RET, DV_RET), F32))
    scratch = [pltpu.VMEM((RET_CHUNK, RET_CHUNK), F32), pltpu.VMEM((4, RET_CHUNK, DK_RET), F32)]
    if n_chunks > 1:
        scratch += [pltpu.VMEM((seq, DV_RET), F32), pltpu.VMEM((DK_RET, DV_RET), F32),
                    pltpu.VMEM((DK_RET, DV_RET), F32)]
    return pl.pallas_call(
        functools.partial(_ret_kernel, n_chunks=n_chunks, n_seq=n_seq, has_init=has_init, has_dst=dst is not None,
                          emit_state=emit_state),
        grid=(H_RET, n_batch // n_seq),
        in_specs=in_specs,
        out_specs=out_specs,
        out_shape=out_shape,
        input_output_aliases=aliases,
        scratch_shapes=scratch,
        compiler_params=_cparams(("parallel", "arbitrary")),
        name="retention_ctx" if emit_state else "retention_lat",
    )(*args)


def _dn_kernel(par_ref, q_ref, k_ref, v_ref, z_ref, gt_ref, cq_ref, ck_ref, cv_ref, nrm_ref, *rest,
               seq, row_len, heads, has_init, has_dst, emit_state):
    rest = list(rest)
    s0_ref = rest.pop(0) if has_init else None
    if has_dst:
        rest.pop(0)
    o_ref = rest.pop(0)
    st_ref = rest.pop(0) if emit_state else None
    rows_scr, o_scr, s_scr, qkv_scr, decay_scr = rest
    cb = DN_CHUNK
    n_blocks = seq // cb
    hg = pl.program_id(1)

    lane = lax.broadcasted_iota(jnp.int32, (1, seq), 1) & (cb - 1)

    def splat(v):
        return jnp.full((1, seq), v, F32)

    def prefix(x):
        s = 1
        while s < cb:
            x = x + jnp.where(lane >= s, pltpu.roll(x, s, 1), 0.0)
            s *= 2
        return x

    def suffix(x):
        s = 1
        while s < cb:
            x = x + jnp.where(lane < cb - s, pltpu.roll(x, seq - s, 1), 0.0)
            s *= 2
        return x

    assert 2 * heads <= 8
    if 2 * heads < 8:
        decay_scr[2 * heads:8, :] = jnp.zeros((8 - 2 * heads, seq), F32)
    for g in range(heads):
        h = hg * heads + g
        r0 = 8 * g
        decay_scr[2 * g:2 * g + 1, :] = (-jnp.exp(splat(par_ref[0, h]))
                                         * jax.nn.softplus(gt_ref[r0 + 2:r0 + 3, :] + splat(par_ref[2, h])))
        decay_scr[2 * g + 1:2 * g + 2, :] = (-jnp.exp(splat(par_ref[1, h]))
                                             * jax.nn.softplus(gt_ref[r0 + 3:r0 + 4, :] + splat(par_ref[3, h])))
    g_all = decay_scr[...]
    p_all = prefix(g_all)
    s_all = suffix(g_all)
    for g in range(heads):
        r0 = 8 * g
        f, b = slice(2 * g, 2 * g + 1), slice(2 * g + 1, 2 * g + 2)
        rows_scr[r0:r0 + 1, :] = jax.nn.sigmoid(gt_ref[r0:r0 + 1, :])
        rows_scr[r0 + 1:r0 + 2, :] = p_all[f, :]
        rows_scr[r0 + 2:r0 + 3, :] = s_all[f, :] - g_all[f, :]
        rows_scr[r0 + 3:r0 + 4, :] = jax.nn.sigmoid(gt_ref[r0 + 1:r0 + 2, :])
        rows_scr[r0 + 4:r0 + 5, :] = s_all[b, :]
        rows_scr[r0 + 5:r0 + 6, :] = p_all[b, :] - g_all[b, :]
        rows_scr[r0 + 6:r0 + 8, :] = jnp.zeros((2, seq), F32)
        for direction in range(2):
            if has_init:
                s_scr[g, direction] = s0_ref[direction, g]
            else:
                s_scr[g, direction] = jnp.zeros((DK_DN, DV_DN), F32)

    ri = lax.broadcasted_iota(jnp.int32, (cb, cb), 0)
    ci = lax.broadcasted_iota(jnp.int32, (cb, cb), 1)
    eye = (ri == ci).astype(F32)
    pair_masks = []
    sz = 2
    while sz < cb:
        pair = ((ri // (2 * sz)) == (ci // (2 * sz))) & ((ri // sz) != (ci // sz))
        pair_masks.append(jnp.where(pair, 1.0, 0.0).astype(BF16))
        sz *= 2
    n_small = (DN_SLAB // 2).bit_length() - 1
    same_slab = (ri // DN_SLAB) == (ci // DN_SLAB)
    slab_mask16 = jnp.where(same_slab, 1.0, 0.0).astype(BF16)
    tpos = lax.broadcasted_iota(jnp.int32, (cb, DK_DN), 0) & (row_len - 1)

    def conv(x, w_ref, lanes):
        acc = x * w_ref[CONV_K // 2:CONV_K // 2 + 1, lanes]
        for i in range(CONV_K):
            d = i - CONV_K // 2
            if d == 0:
                continue
            sh = pltpu.roll(x, (-d) % cb, 0)
            ok = (tpos + d >= 0) & (tpos + d < row_len)
            acc = acc + jnp.where(ok, sh, 0.0) * w_ref[i:i + 1, lanes]
        return _silu(acc)

    def l2n(x):
        return x * lax.rsqrt(jnp.sum(x * x, axis=-1, keepdims=True) + EPS)

    def col_of(row):
        parts = []
        for t in range(cb // LANES):
            parts.append(jnp.broadcast_to(row[:, t * LANES:(t + 1) * LANES], (LANES, LANES)).T)
        return jnp.concatenate(parts, axis=0)

    def rows_of(bi):
        return pl.ds(pl.multiple_of(bi * cb, cb), cb)

    def prepare(bi, carry):
        r = rows_of(bi)
        for g in range(heads):
            lanes = slice(g * DK_DN, (g + 1) * DK_DN)
            qkv_scr[g, 0, r, :] = l2n(conv(q_ref[r, lanes].astype(F32), cq_ref, lanes)) * (DK_DN ** -0.5)
            qkv_scr[g, 1, r, :] = l2n(conv(k_ref[r, lanes].astype(F32), ck_ref, lanes))
            qkv_scr[g, 2, r, :] = conv(v_ref[r, lanes].astype(F32), cv_ref, lanes)
        return carry

    if n_blocks == 1:
        prepare(0, 0)
    else:
        lax.fori_loop(0, n_blocks, prepare, 0)

    def blocks(chains):
        cs = []
        for g, bi, direction in chains:
            r = rows_of(bi)
            base = 8 * g + (0 if direction == 0 else 3)
            cs.append(dict(g=g, d=direction, r=r, beta_row=rows_scr[base:base + 1, r],
                           g_row=rows_scr[base + 1:base + 2, r], e_row=rows_scr[base + 2:base + 3, r]))
        for c in cs:
            c["q"] = qkv_scr[c["g"], 0, c["r"], :]
            c["k"] = qkv_scr[c["g"], 1, c["r"], :]
            c["v"] = qkv_scr[c["g"], 2, c["r"], :]
            c["k16"] = c["k"].astype(BF16)
            c["kk"] = _nt_dot(c["k16"], c["k16"])
        for c in cs:
            c["beta_c"] = col_of(c["beta_row"])
            c["g_c"] = col_of(c["g_row"])
            c["incl"] = (ri >= ci) if c["d"] == 0 else (ri <= ci)
            strict = (ri > ci) if c["d"] == 0 else (ri < ci)
            g_cw = jnp.concatenate([c["g_c"]] * (cb // LANES), axis=1)
            beta_cw = jnp.concatenate([c["beta_c"]] * (cb // LANES), axis=1)
            c["lmat"] = jnp.exp(jnp.where(c["incl"], g_cw - c["g_row"], NEG_INF))
            c["m"] = jnp.where(strict, c["kk"] * beta_cw * c["lmat"], 0.0)
            c["x"] = eye - jnp.where((ri // 2) == (ci // 2), c["m"], 0.0)
        n_slab = cb // DN_SLAB
        for c in cs:
            c["m16"] = c["m"].astype(BF16)
            c["xc"] = functools.reduce(
                lambda a, b: a + b, [c["x"][p * DN_SLAB:(p + 1) * DN_SLAB, :] for p in range(n_slab)])
        for pair16 in pair_masks[:n_small]:
            for c in cs:
                xc16 = c["xc"].astype(BF16)
                c["xbd16"] = jnp.concatenate([xc16] * n_slab, axis=0) * slab_mask16
                c["t16"] = _dot(xc16, c["m16"] * pair16).astype(BF16)
            for c in cs:
                c["xc"] = c["xc"] - _dot(c["t16"], c["xbd16"])
        for c in cs:
            c["x"] = jnp.where(same_slab, jnp.concatenate([c["xc"]] * n_slab, axis=0), 0.0)
        for pair16 in pair_masks[n_small:]:
            for c in cs:
                c["x16"] = c["x"].astype(BF16)
                c["t16"] = _dot(c["x16"], c["m16"] * pair16).astype(BF16)
            for c in cs:
                c["x"] = c["x"] - _dot(c["t16"], c["x16"])
        for c in cs:
            eg_c = jnp.exp(c["g_c"])
            rhs = jnp.concatenate([c["v"] * c["beta_c"], c["k"] * (c["beta_c"] * eg_c)], axis=1).astype(BF16)
            uw = _dot(c["x"].astype(BF16), rhs)
            c["u"] = uw[:, :DV_DN]
            c["w16"] = uw[:, DV_DN:].astype(BF16)
            c["qg16"] = (c["q"] * eg_c).astype(BF16)
            c["aqk"] = (_nt_dot(c["q"].astype(BF16), c["k16"]) * c["lmat"]).astype(BF16)
            c["kdec_t"] = (c["k"] * jnp.exp(col_of(c["e_row"]))).T.astype(BF16)
        outs = []
        for c in cs:
            s = s_scr[c["g"], c["d"]]
            s16 = s.astype(BF16)
            vn16 = (c["u"] - _dot(c["w16"], s16)).astype(BF16)
            outs.append(_dot(c["qg16"], s16) + _dot(c["aqk"], vn16))
            last = cb - 1 if c["d"] == 0 else 0
            s_scr[c["g"], c["d"]] = s * jnp.exp(c["g_c"][last:last + 1, :]) + _dot(c["kdec_t"], vn16)
        return outs

    def finish(g, bi, d):
        r = rows_of(bi)
        lanes = slice(g * DV_DN, (g + 1) * DV_DN)
        d = d * lax.rsqrt(jnp.mean(d * d, axis=-1, keepdims=True) + EPS)
        o_ref[r, lanes] = (d * nrm_ref[...] * _silu(z_ref[r, lanes].astype(F32))).astype(o_ref.dtype)

    if n_blocks == 1:
        outs = blocks([(g, 0, d) for g in range(heads) for d in range(2)])
        for g in range(heads):
            finish(g, 0, outs[2 * g] + outs[2 * g + 1])
    else:
        per = 2 if n_blocks % 4 == 0 else 1

        def chains_of(it):
            return [(g, per * it + j if d == 0 else n_blocks - 1 - per * it - j, d)
                    for g in range(heads) for d in range(2) for j in range(per)]

        def first_half(it, carry):
            chains = chains_of(it)
            for (g, bi, _), out in zip(chains, blocks(chains)):
                o_scr[g, rows_of(bi), :] = out
            return carry

        def second_half(it, carry):
            chains = chains_of(it)
            for (g, bi, _), out in zip(chains, blocks(chains)):
                finish(g, bi, out + o_scr[g, rows_of(bi), :])
            return carry

        n_it = n_blocks // per
        lax.fori_loop(0, n_it // 2, first_half, 0)
        lax.fori_loop(n_it // 2, n_it, second_half, 0)
    if emit_state:
        for g in range(heads):
            st_ref[0, g] = s_scr[g, 0]
            st_ref[1, g] = s_scr[g, 1]


def _deltanet(proj, gates_t, dn_par, dn_conv, dn_norm, s0, dst, *, n_batch, seq, row0, row_len, heads,
              emit_state):
    rb0 = row0 // seq
    has_init = s0 is not None
    aliases = {}
    wd = heads * DK_DN
    in_specs = [pl.BlockSpec(memory_space=pltpu.SMEM),
                pl.BlockSpec((seq, wd), lambda b, h: (rb0 + b, DQ0 // wd + h)),
                pl.BlockSpec((seq, wd), lambda b, h: (rb0 + b, DK0 // wd + h)),
                pl.BlockSpec((seq, wd), lambda b, h: (rb0 + b, DV0 // wd + h)),
                pl.BlockSpec((seq, wd), lambda b, h: (rb0 + b, DZ0 // wd + h)),
                pl.BlockSpec((8 * heads, seq), lambda b, h: (h, rb0 + b)),
                pl.BlockSpec((CONV_K, wd), lambda b, h: (0, h)),
                pl.BlockSpec((CONV_K, wd), lambda b, h: (0, DN_QK // wd + h)),
                pl.BlockSpec((CONV_K, wd), lambda b, h: (0, 2 * DN_QK // wd + h)),
                pl.BlockSpec((1, DV_DN), lambda b, h: (0, 0))]
    args = [dn_par, proj, proj, proj, proj, gates_t, dn_conv, dn_conv, dn_conv, dn_norm.reshape(1, DV_DN)]
    if has_init:
        in_specs.append(pl.BlockSpec((None, 2, heads, DK_DN, DV_DN), lambda b, h: (b, 0, h, 0, 0)))
        args.append(s0)
    if dst is not None:
        in_specs.append(pl.BlockSpec(memory_space=pl.ANY))
        aliases[len(args)] = 0
        args.append(dst)
    out_specs = [pl.BlockSpec((seq, wd), lambda b, h: (rb0 + b, h))]
    out_shape = [jax.ShapeDtypeStruct((proj.shape[0], DN_V), BF16)]
    if emit_state:
        out_specs.append(pl.BlockSpec((None, 2, heads, DK_DN, DV_DN), lambda b, h: (b, 0, h, 0, 0)))
        out_shape.append(jax.ShapeDtypeStruct((n_batch, 2, H_DN, DK_DN, DV_DN), F32))
    scratch = [pltpu.VMEM((8 * heads, seq), F32), pltpu.VMEM((heads, seq, DV_DN), F32),
               pltpu.VMEM((heads, 2, DK_DN, DV_DN), F32), pltpu.VMEM((heads, 3, seq, DK_DN), F32),
               pltpu.VMEM((8, seq), F32)]
    n_blocks = seq // DN_CHUNK
    assert seq % DN_CHUNK == 0 and (n_blocks == 1 or n_blocks % 2 == 0)
    assert DN_CHUNK % row_len == 0 and row_len & (row_len - 1) == 0 and H_DN % heads == 0
    return pl.pallas_call(
        functools.partial(_dn_kernel, seq=seq, row_len=row_len, heads=heads, has_init=has_init,
                          has_dst=dst is not None, emit_state=emit_state),
        grid=(n_batch, H_DN // heads),
        in_specs=in_specs,
        out_specs=out_specs,
        out_shape=out_shape,
        input_output_aliases=aliases,
        scratch_shapes=scratch,
        compiler_params=_cparams(("parallel", "arbitrary")),
        name="deltanet_ctx" if emit_state else "deltanet_lat",
    )(*args)


def _sort_network(n):
    pairs = []
    k = 2
    while k <= n:
        j = k // 2
        while j >= 1:
            for i in range(n):
                l = i ^ j
                if l > i:
                    pairs.append((i, l) if (i & k) == 0 else (l, i))
            j //= 2
        k *= 2
    return pairs


def _cand_pairs():
    n = PEER_TOPK + 1
    return [(a, b) for a in range(n) for b in range(n) if (a + 1) * (b + 1) <= n]


def _peer_score_kernel(q_ref, sk_ref, pb_ref, pf_ref, s_scr, sv_scr, gp_scr):
    def top_values_sorted(s, p, h):
        n_rows = 8
        v = [s[j * n_rows:(j + 1) * n_rows, :] for j in range(N_KEYS // n_rows)]
        for a, b in _sort_network(len(v)):
            v[a], v[b] = jnp.maximum(v[a], v[b]), jnp.minimum(v[a], v[b])
        for r in range(PEER_TOPK + 1):
            m = jnp.max(v[0], axis=0, keepdims=True)
            sv_scr[p, r, pl.ds(h, 1), :] = m
            hit = v[0] == m
            live = min(len(v), PEER_TOPK + 1 - r)
            for j in range(live - 1):
                v[j] = jnp.where(hit, v[j + 1], v[j])
            if live == len(v):
                v[live - 1] = jnp.where(hit, NEG_INF, v[live - 1])

    def per_head(h, carry):
        for p in range(2):
            s = _nt_dot(sk_ref[p], q_ref[2 * h + p])
            s_scr[p, h] = s
            top_values_sorted(s, p, h)
        rank = jnp.zeros(s.shape, F32)
        for r in range(PEER_TOPK):
            rank = rank + jnp.where(sv_scr[1, r, pl.ds(h, 1), :] > s, 1.0, 0.0)
        pb_ref[0, h] = rank.astype(BF16)
        return carry

    lax.fori_loop(0, PEER_HEADS, per_head, 0)
    cands = [sv_scr[0, a] + sv_scr[1, b] for a, b in _cand_pairs()]
    mx = cands[0]
    z = jnp.zeros_like(mx)
    for r in range(PEER_TOPK):
        m = functools.reduce(jnp.maximum, cands)
        z = z + jnp.exp(m - mx)
        cands = [jnp.where(c == m, NEG_INF, c) for c in cands]
    runner_up = functools.reduce(jnp.maximum, cands)
    gp_scr[0] = 0.5 * (m + runner_up)
    gp_scr[1] = 1.0 / z

    def finish_head(h, carry):
        row = pl.ds(h, 1)
        s1 = s_scr[0, h]
        pb_ref[1, h] = jnp.exp(s_scr[1, h] - sv_scr[1, 0, row, :]).astype(BF16)
        pf_ref[0, h] = jnp.exp(s1 - sv_scr[0, 0, row, :]) * gp_scr[1, row, :]
        need = gp_scr[0, row, :] - s1
        cnt = jnp.zeros_like(need)
        for r in range(PEER_TOPK):
            cnt = cnt + jnp.where(sv_scr[1, r, row, :] >= need, 1.0, 0.0)
        pf_ref[1, h] = cnt
        return carry

    lax.fori_loop(0, PEER_HEADS, finish_head, 0)


def _peer_scores(q3, subkeys, tb):
    nhp, t, _ = q3.shape
    return pl.pallas_call(
        _peer_score_kernel,
        grid=(t // tb,),
        in_specs=[pl.BlockSpec((nhp, tb, PEER_QDIM // 2), lambda i: (0, i, 0)),
                  pl.BlockSpec((2, N_KEYS, PEER_QDIM // 2), lambda i: (0, 0, 0))],
        out_specs=[pl.BlockSpec((2, PEER_HEADS, N_KEYS, tb), lambda i: (0, 0, 0, i)),
                   pl.BlockSpec((2, PEER_HEADS, N_KEYS, tb), lambda i: (0, 0, 0, i))],
        out_shape=[jax.ShapeDtypeStruct((2, PEER_HEADS, N_KEYS, t), BF16),
                   jax.ShapeDtypeStruct((2, PEER_HEADS, N_KEYS, t), F32)],
        scratch_shapes=[pltpu.VMEM((2, PEER_HEADS, N_KEYS, tb), F32),
                        pltpu.VMEM((2, PEER_TOPK + 1, PEER_HEADS, tb), F32),
                        pltpu.VMEM((2, PEER_HEADS, tb), F32)],
        compiler_params=_cparams(("parallel",)),
        name="peer_scores",
    )(q3, subkeys)


def _gelu_tanh(x):
    c = 0.7978845608028654
    h = 0.5 * x
    return h + h * jnp.tanh(x * (c + (c * 0.044715) * (x * x)))


def _peer_dense_kernel(h_ref, pb_ref, pf_ref, u_ref, vt_ref, o_ref, *, n_i1):
    e = pl.program_id(1)
    zero = jnp.zeros((), BF16)
    tb, d = h_ref.shape
    kt = V7X_MXU_DIM
    per_m = kt // N_KEYS

    @pl.when(e == 0)
    def _():
        o_ref[...] = jnp.zeros_like(o_ref)

    def gate_head(ii, h):
        row = pl.ds(e * n_i1 + ii, 1)

        def bcast(x):
            x = jnp.broadcast_to(x, (16, x.shape[1])).astype(BF16)
            return jnp.concatenate([x] * (N_KEYS // 16), axis=0)

        e1 = bcast(pf_ref[0, h, row, :])
        cnt = bcast(pf_ref[1, h, row, :])
        return jnp.where(pb_ref[0, h] < cnt, pb_ref[1, h] * e1, zero)

    wa_parts = []
    for m in range(n_i1 // per_m):
        todo = [(per_m * m + j, h) for j in range(per_m) for h in range(PEER_HEADS)]
        gate = {}

        def one_gate():
            ii, h = todo.pop(0)
            g = gate_head(ii, h)
            gate[ii] = g if h == 0 else gate[ii] + g

        accs = [None] * (tb // kt)
        for k in range(d // kt):
            for n in range(tb // kt):
                part = _nt_dot(u_ref[m * kt:(m + 1) * kt, k * kt:(k + 1) * kt],
                               h_ref[n * kt:(n + 1) * kt, k * kt:(k + 1) * kt])
                accs[n] = part if accs[n] is None else accs[n] + part
                if todo:
                    one_gate()
        while todo:
            one_gate()
        act = jnp.concatenate([_gelu_tanh(a).astype(BF16) for a in accs], axis=1)
        w = jnp.concatenate([gate[per_m * m + j] for j in range(per_m)], axis=0)
        wa_parts.append(w * act)
    o_ref[...] += _dot(vt_ref[...], jnp.concatenate(wa_parts, axis=0))


def _peer_dense(h2, pb, pf, u_tab, v_tab_t, tb, et):
    t, d = h2.shape
    ne = u_tab.shape[0]
    return pl.pallas_call(
        functools.partial(_peer_dense_kernel, n_i1=et // N_KEYS),
        grid=(t // tb, ne // et),
        in_specs=[pl.BlockSpec((tb, d), lambda i, e: (i, 0)),
                  pl.BlockSpec((2, PEER_HEADS, N_KEYS, tb), lambda i, e: (0, 0, 0, i)),
                  pl.BlockSpec((2, PEER_HEADS, N_KEYS, tb), lambda i, e: (0, 0, 0, i)),
                  pl.BlockSpec((et, d), lambda i, e: (e, 0)),
                  pl.BlockSpec((d, et), lambda i, e: (0, e))],
        out_specs=pl.BlockSpec((d, tb), lambda i, e: (0, i)),
        out_shape=jax.ShapeDtypeStruct((d, t), F32),
        compiler_params=_cparams(("parallel", "arbitrary")),
        name="peer_dense",
    )(h2, pb, pf, u_tab, v_tab_t)


def _final_kernel(x_ref, p_ref, mod_ref, g_ref, oc_ref, ol_ref, *, n_ctx_blocks):
    x = x_ref[...] + mod_ref[5:6, :] * p_ref[...].T
    ms = jnp.mean(x * x, axis=-1, keepdims=True)
    y = (x * lax.rsqrt(ms + EPS)) * g_ref[...]
    i = pl.program_id(0)

    @pl.when(i < n_ctx_blocks)
    def _():
        oc_ref[...] = y

    @pl.when(i >= n_ctx_blocks)
    def _():
        ol_ref[...] = y


def _final(x1, peer_t, mod8, seg_of_block, tm_seg, final_norm, tc):
    t, d = x1.shape
    tm = FINAL_TILE
    sub = tm_seg // tm
    nc = tc // tm
    return pl.pallas_call(
        functools.partial(_final_kernel, n_ctx_blocks=nc),
        grid=(t // tm,),
        in_specs=[pl.BlockSpec((tm, d), lambda i: (i, 0)),
                  pl.BlockSpec((d, tm), lambda i: (0, i)),
                  pl.BlockSpec((None, 8, d), lambda i: (seg_of_block(i // sub), 0, 0)),
                  pl.BlockSpec((1, d), lambda i: (0, 0))],
        out_specs=[pl.BlockSpec((tm, d), lambda i: (jnp.minimum(i, nc - 1), 0)),
                   pl.BlockSpec((tm, d), lambda i: (jnp.maximum(i - nc, 0), 0))],
        out_shape=[jax.ShapeDtypeStruct((tc, d), F32), jax.ShapeDtypeStruct((t - tc, d), F32)],
        compiler_params=_cparams(("arbitrary",)),
        name="final_norm",
    )(x1, peer_t, mod8, final_norm.reshape(1, d))


def kernel(x_prompt, x_sample, state_ret, state_dn, c, c_ctx, w_ada, b_ada, norm_mix, norm_ffn, w_in, ret_logit,
           ret_gn, dn_conv, dn_a_log, dn_dt_bias, dn_norm, w_br_a, w_br_b, w_out, peer_wq, peer_subkeys, peer_u,
           peer_v, final_norm):
    bc, lc, d = x_prompt.shape
    bl, ll, _ = x_sample.shape
    depth = w_ada.shape[0]
    tc = bc * lc
    t = tc + bl * ll
    assert bl + 1 <= 8 and tc % ll == 0

    tm = _pick_tile(math.gcd(tc, ll), (1024, 512, 256))
    n_ctx_blocks = tc // tm
    lat_blocks = ll // tm

    def seg_of_block(i):
        return jnp.where(i < n_ctx_blocks, 0, 1 + (i - n_ctx_blocks) // lat_blocks)

    xc = x_prompt.reshape(tc, d)
    xl = x_sample.reshape(bl * ll, d)
    cvec = jnp.concatenate([c_ctx[None, :], c, jnp.zeros((8 - 1 - bl, d), F32)], axis=0)

    ret_states, dn_states = [], []
    for l in range(depth):
        mod = _adaln(cvec, w_ada[l], b_ada[l])
        mod8 = jnp.pad(mod.reshape(8, 6, d), ((0, 0), (0, 2), (0, 0)))

        w = w_in[l].astype(BF16)
        wg = w[:, DGL0:DGL0 + 4 * H_DN].reshape(d, 4, H_DN).transpose(2, 1, 0)
        w_gate_t = jnp.pad(wg, ((0, 0), (0, 4), (0, 0))).reshape(8 * H_DN, d)

        proj, gates_t = _inproj(xc, xl, mod8, seg_of_block, tm, norm_mix[l], w, w[:, DGL0 + 4 * H_DN:], w_gate_t)

        lg = jax.nn.log_sigmoid(ret_logit[l].astype(F32))
        ret_c, rs = _retention(proj, lg, ret_gn[l], None, jnp.zeros((t, RET_V), BF16), n_batch=bc, seq=lc, row0=0,
                               emit_state=True)
        (ret_out,) = _retention(proj, lg, ret_gn[l], state_ret[:, l], ret_c, n_batch=bl, seq=ll, row0=tc,
                                emit_state=False)
        dn_par = jnp.concatenate([dn_a_log[l], dn_dt_bias[l]], axis=0).astype(F32)
        dn_c, ds = _deltanet(proj, gates_t, dn_par, dn_conv[l], dn_norm[l], None, jnp.zeros((t, DN_V), BF16),
                             n_batch=bc, seq=lc, row0=0, row_len=lc, heads=4, emit_state=True)
        (dn_out,) = _deltanet(proj, gates_t, dn_par, dn_conv[l], dn_norm[l], state_dn[:, l], dn_c, n_batch=bl,
                              seq=ll, row0=tc, row_len=GRID_W, heads=2, emit_state=False)
        ret_states.append(rs)
        dn_states.append(ds)

        merged = _merge(ret_out, dn_out, w_br_a[l].astype(BF16), w_br_b[l].astype(BF16), proj, d, tm)
        x1 = _outproj(merged, w_out[l].astype(BF16), xc, xl, mod8, seg_of_block, tm)

        h2, q3 = _peerq(x1, mod8, seg_of_block, tm, norm_ffn[l], peer_wq[l].astype(BF16))
        tb = min(PEER_TOKEN_TILE, tm)
        pb, pf = _peer_scores(q3, peer_subkeys[l].astype(BF16), tb)
        peer_t = _peer_dense(h2, pb, pf, peer_u[l].astype(BF16), peer_v[l].astype(BF16).T, tb,
                             PEER_EXPERT_TILE)
        assert depth == 1

    y_c, y_l = _final(x1, peer_t, mod8, seg_of_block, tm, final_norm, tc)
    y_prompt = y_c.reshape(bc, lc, d)
    y_sample = y_l.reshape(bl, ll, d)
    new_state_ret = jnp.stack(ret_states, axis=1)
    new_state_dn = jnp.stack(dn_states, axis=1)
    return (y_prompt, y_sample, new_state_ret, new_state_dn)
```
